```python
import math
import jax, jax.numpy as jnp
from jax import lax
import numpy as np


D_MODEL = 4096
BATCH = 4
SEQ = 4096
DEPTH = 1

HEAD_DIM_SWA = 64
N_HEADS_SWA = (D_MODEL // 2) // HEAD_DIM_SWA
N_KV_SWA = N_HEADS_SWA // 8
WINDOW = 128
ROPE_THETA = 150000.0
HEAD_DIM_MOBA = 128
N_HEADS_MOBA = (D_MODEL // 2) // HEAD_DIM_MOBA
N_KV_MOBA = N_HEADS_MOBA // 4
MOBA_BLOCK = 256
MOBA_TOPK = 3
MOBA_QCHUNK = 16
REL_BUCKETS = 32
REL_MAX_DIST = 1024
N_EXPERTS = 32
TOP_K = 4
D_FF_EXPERT = (3 * D_MODEL) // 8
SWIGLU_LIMIT = 7.0
SWIGLU_ALPHA = 1.702
MOE_BLOCK = 256
RMS_EPS = 1e-6

Q_SWA = N_HEADS_SWA * HEAD_DIM_SWA
KV_SWA = N_KV_SWA * HEAD_DIM_SWA
Q_MOBA = N_HEADS_MOBA * HEAD_DIM_MOBA
KV_MOBA = N_KV_MOBA * HEAD_DIM_MOBA
SPLIT_SIZES = (Q_SWA, KV_SWA, KV_SWA, Q_MOBA, KV_MOBA, KV_MOBA, D_MODEL, D_MODEL)
D_IN = Q_SWA + 2 * KV_SWA + Q_MOBA + 2 * KV_MOBA + 2 * D_MODEL

kernel_name = 'hybrid_swa_moba_moe_block'


def rms_norm(x, g):
    xf = x.astype(jnp.float32)
    y = xf * lax.rsqrt(jnp.mean(xf * xf, axis=-1, keepdims=True) + RMS_EPS)
    return (y * g.astype(jnp.float32)).astype(x.dtype)


def split_cols(p):
    outs, off = [], 0
    for n in SPLIT_SIZES:
        outs.append(p[..., off:off + n])
        off += n
    return outs


def rope(x, pos):
    half = x.shape[-1] // 2
    inv = ROPE_THETA ** (-jnp.arange(half, dtype=jnp.float32) / half)
    ang = pos.astype(jnp.float32)[:, None] * inv[None, :]
    cos = jnp.cos(ang)[None, :, None, :]
    sin = jnp.sin(ang)[None, :, None, :]
    x1, x2 = x[..., :half], x[..., half:]
    return jnp.concatenate([x1 * cos - x2 * sin, x2 * cos + x1 * sin], axis=-1).astype(x.dtype)


def t5_bucket(dist):
    d = jnp.maximum(dist, 0)
    max_exact = REL_BUCKETS // 2
    log_ratio = jnp.log(jnp.maximum(d, max_exact).astype(jnp.float32) / max_exact) / math.log(REL_MAX_DIST / max_exact)
    large = jnp.minimum(max_exact + (log_ratio * (REL_BUCKETS - max_exact)).astype(jnp.int32), REL_BUCKETS - 1)
    return jnp.where(d < max_exact, d, large)


def swa_attention(q, k, v, sinks):
    B, S, H, Dh = q.shape
    Hkv = k.shape[2]
    G = H // Hkv
    nb = S // WINDOW
    qb = q.reshape(B, nb, WINDOW, Hkv, G, Dh)
    pad = ((0, 0), (WINDOW, 0), (0, 0), (0, 0))
    kk = jnp.concatenate([jnp.pad(k, pad)[:, :S].reshape(B, nb, WINDOW, Hkv, Dh),
                          k.reshape(B, nb, WINDOW, Hkv, Dh)], axis=2)
    vv = jnp.concatenate([jnp.pad(v, pad)[:, :S].reshape(B, nb, WINDOW, Hkv, Dh),
                          v.reshape(B, nb, WINDOW, Hkv, Dh)], axis=2)
    s = jnp.einsum('bnqhgd,bnkhd->bnhgqk', qb, kk).astype(jnp.float32) * (1.0 / math.sqrt(Dh))
    qi = jnp.arange(WINDOW)[:, None]
    kj = jnp.arange(2 * WINDOW)[None, :]
    diff = WINDOW + qi - kj
    band = (diff >= 0) & (diff < WINDOW)
    key_pos = jnp.arange(nb)[:, None] * WINDOW - WINDOW + kj
    mask = band[None] & (key_pos >= 0)[:, None, :]
    s = jnp.where(mask[None, :, None, None], s, -jnp.inf)
    sink = jnp.broadcast_to(sinks.astype(jnp.float32).reshape(1, 1, Hkv, G, 1, 1), s.shape[:-1] + (1,))
    p = jax.nn.softmax(jnp.concatenate([s, sink], axis=-1), axis=-1)[..., :-1]
    o = jnp.einsum('bnhgqk,bnkhd->bnqhgd', p.astype(v.dtype), vv)
    return o.reshape(B, S, H * Dh)


def moba_attention(q, k, v, rel_bias):
    B, S, H, Dh = q.shape
    Hkv = k.shape[2]
    G = H // Hkv
    nblk = -(-S // MOBA_BLOCK)
    s_pad = nblk * MOBA_BLOCK
    pad = ((0, 0), (0, s_pad - S), (0, 0), (0, 0))
    kb = jnp.pad(k, pad).reshape(B, nblk, MOBA_BLOCK, Hkv, Dh).transpose(0, 1, 3, 2, 4)
    vb = jnp.pad(v, pad).reshape(B, nblk, MOBA_BLOCK, Hkv, Dh).transpose(0, 1, 3, 2, 4)
    kmean = jnp.mean(kb.astype(jnp.float32), axis=3)
    pos = jnp.arange(S)
    gate = jnp.einsum('bshgd,bnhd->bshgn', q.reshape(B, S, Hkv, G, Dh).astype(jnp.float32), kmean).reshape(B, S, H, nblk)
    past = jnp.arange(nblk)[None, :] < (pos // MOBA_BLOCK)[:, None]
    gate = jnp.where(past[None, :, None, :], gate, -jnp.inf)
    n_gate = max(nblk, MOBA_TOPK)
    gate = jnp.pad(gate, ((0, 0), (0, 0), (0, 0), (0, n_gate - nblk)), constant_values=-jnp.inf)
    _, sel = lax.top_k(gate, MOBA_TOPK)
    sel = jnp.minimum(sel, nblk - 1)
    nc = S // MOBA_QCHUNK
    q_c = q.reshape(B, nc, MOBA_QCHUNK, H, Dh).transpose(1, 0, 2, 3, 4)
    sel_c = sel.reshape(B, nc, MOBA_QCHUNK, H, MOBA_TOPK).transpose(1, 0, 2, 3, 4)
    table = rel_bias.T
    b_idx = jnp.arange(B)[:, None, None, None]
    hk_idx = (jnp.arange(H) // G)[None, None, :, None]
    h_idx = jnp.arange(H)[None, None, :, None, None]
    scale = 1.0 / math.sqrt(Dh)
    n_sel = MOBA_TOPK * MOBA_BLOCK

    def chunk(args):
        qi, si, cid = args
        pos_q = cid * MOBA_QCHUNK + jnp.arange(MOBA_QCHUNK)
        own = (cid * MOBA_QCHUNK) // MOBA_BLOCK
        valid = jnp.arange(MOBA_TOPK)[None, :] < (pos_q // MOBA_BLOCK)[:, None]
        k_sel = kb[b_idx, si, hk_idx]
        v_sel = vb[b_idx, si, hk_idx]
        key_pos = si[..., None] * MOBA_BLOCK + jnp.arange(MOBA_BLOCK)
        dist = pos_q[None, :, None, None, None] - key_pos
        s_sel = jnp.einsum('bqhd,bqhkmd->bqhkm', qi, k_sel).astype(jnp.float32) * scale + table[h_idx, t5_bucket(dist)]
        s_sel = jnp.where(valid[None, :, None, :, None], s_sel, -jnp.inf)
        k_own = lax.dynamic_index_in_dim(kb, own, axis=1, keepdims=False)
        v_own = lax.dynamic_index_in_dim(vb, own, axis=1, keepdims=False)
        q_g = qi.reshape(B, MOBA_QCHUNK, Hkv, G, Dh)
        s_own = jnp.einsum('bqhgd,bhmd->bqhgm', q_g, k_own).astype(jnp.float32).reshape(B, MOBA_QCHUNK, H, MOBA_BLOCK) * scale
        dist_own = pos_q[:, None] - (own * MOBA_BLOCK + jnp.arange(MOBA_BLOCK))[None, :]
        s_own = s_own + table[:, t5_bucket(dist_own)].transpose(1, 0, 2)[None]
        s_own = jnp.where((dist_own >= 0)[None, :, None, :], s_own, -jnp.inf)
        scores = jnp.concatenate([s_sel.reshape(B, MOBA_QCHUNK, H, n_sel), s_own], axis=-1)
        p = jax.nn.softmax(scores, axis=-1).astype(v.dtype)
        p_sel = p[..., :n_sel].reshape(B, MOBA_QCHUNK, H, MOBA_TOPK, MOBA_BLOCK)
        p_own = p[..., n_sel:].reshape(B, MOBA_QCHUNK, Hkv, G, MOBA_BLOCK)
        o = jnp.einsum('bqhkm,bqhkmd->bqhd', p_sel, v_sel)
        o = o + jnp.einsum('bqhgm,bhmd->bqhgd', p_own, v_own).reshape(B, MOBA_QCHUNK, H, Dh)
        return o

    o = lax.map(chunk, (q_c, sel_c, jnp.arange(nc)))
    return o.transpose(1, 0, 2, 3, 4).reshape(B, S, H * Dh)


def mixer_block(h, pos, w_in, b_in, sinks, rel_bias, w_o_swa, w_o_moba, w_out):
    B, S, _ = h.shape
    proj = h @ w_in + b_in
    q_a, k_a, v_a, q_b, k_b, v_b, g_a, g_b = split_cols(proj)
    q_a = rope(q_a.reshape(B, S, N_HEADS_SWA, HEAD_DIM_SWA), pos)
    k_a = rope(k_a.reshape(B, S, N_KV_SWA, HEAD_DIM_SWA), pos)
    v_a = v_a.reshape(B, S, N_KV_SWA, HEAD_DIM_SWA)
    o_a = swa_attention(q_a, k_a, v_a, sinks)
    o_b = moba_attention(q_b.reshape(B, S, N_HEADS_MOBA, HEAD_DIM_MOBA),
                         k_b.reshape(B, S, N_KV_MOBA, HEAD_DIM_MOBA),
                         v_b.reshape(B, S, N_KV_MOBA, HEAD_DIM_MOBA), rel_bias)
    merged = jax.nn.sigmoid(g_a) * (o_a @ w_o_swa) + jax.nn.sigmoid(g_b) * (o_b @ w_o_moba)
    return merged @ w_out


def moe_ffn(h, w_router, b_router, w_gate_up, b_gate_up, w_down, b_down):
    T, D = h.shape
    logits = (h @ w_router + b_router).astype(jnp.float32)
    top_val, top_idx = lax.top_k(logits, TOP_K)
    top_w = jax.nn.softmax(top_val, axis=-1)
    A = T * TOP_K
    flat_e = top_idx.reshape(A)
    flat_tok = (jnp.arange(A) // TOP_K).astype(jnp.int32)
    flat_w = top_w.reshape(A)
    order = jnp.argsort(flat_e, stable=True)
    e_sorted = flat_e[order]
    counts = jnp.bincount(flat_e, length=N_EXPERTS)
    padded = (counts + MOE_BLOCK - 1) // MOE_BLOCK * MOE_BLOCK
    pad_end = jnp.cumsum(padded)
    pad_start = pad_end - padded
    grp_start = jnp.cumsum(counts) - counts
    dest = pad_start[e_sorted] + jnp.arange(A) - grp_start[e_sorted]
    n_blocks = -(-A // MOE_BLOCK) + N_EXPERTS
    R = n_blocks * MOE_BLOCK
    row_tok = jnp.full((R,), T, jnp.int32).at[dest].set(flat_tok[order])
    row_w = jnp.zeros((R,), jnp.float32).at[dest].set(flat_w[order])
    block_e = jnp.minimum(jnp.searchsorted(pad_end, jnp.arange(n_blocks) * MOE_BLOCK, side='right'), N_EXPERTS - 1)
    h_pad = jnp.concatenate([h, jnp.zeros((1, D), h.dtype)], axis=0)

    def expert_block(acc, blk):
        rows, wts, e = blk
        gu = h_pad[rows] @ w_gate_up[e] + b_gate_up[e]
        glu = jnp.minimum(gu[:, 0::2], SWIGLU_LIMIT)
        lin = jnp.clip(gu[:, 1::2], -SWIGLU_LIMIT, SWIGLU_LIMIT)
        act = glu * jax.nn.sigmoid(SWIGLU_ALPHA * glu) * (lin + 1)
        y = act @ w_down[e] + b_down[e]
        return acc.at[rows].add(y.astype(jnp.float32) * wts[:, None]), None

    acc, _ = lax.scan(expert_block, jnp.zeros((T + 1, D), jnp.float32),
                      (row_tok.reshape(n_blocks, MOE_BLOCK), row_w.reshape(n_blocks, MOE_BLOCK), block_e))
    return acc[:T].astype(h.dtype)


def setup_inputs(seed: int = 0) -> dict:
    key = jax.random.key(seed)
    ks = jax.random.split(key, 22)
    L, D = DEPTH, D_MODEL

    def nrm(k, shape, scale):
        return jax.random.normal(k, shape, jnp.float32) * scale

    return {
        'x': nrm(ks[0], (BATCH, SEQ, D), 1.0),
        'c': nrm(ks[1], (BATCH, D), 1.0),
        'w_ada': nrm(ks[2], (L, D, 6 * D), 0.5 * D ** -0.5),
        'b_ada': nrm(ks[3], (L, 6 * D), 0.01),
        'g_pre_mix': 1.0 + nrm(ks[4], (L, D), 0.05),
        'g_post_mix': 1.0 + nrm(ks[5], (L, D), 0.05),
        'g_pre_ffn': 1.0 + nrm(ks[6], (L, D), 0.05),
        'g_post_ffn': 1.0 + nrm(ks[7], (L, D), 0.05),
        'w_in': nrm(ks[8], (L, D, D_IN), D ** -0.5),
        'b_in': nrm(ks[9], (L, D_IN), 0.01),
        'sinks': nrm(ks[10], (L, N_HEADS_SWA), 1.0),
        'rel_bias': nrm(ks[11], (REL_BUCKETS, N_HEADS_MOBA), 0.5),
        'w_o_swa': nrm(ks[12], (L, Q_SWA, D), Q_SWA ** -0.5),
        'w_o_moba': nrm(ks[13], (L, Q_MOBA, D), Q_MOBA ** -0.5),
        'w_out': nrm(ks[14], (L, D, D), D ** -0.5),
        'w_router': nrm(ks[15], (L, D, N_EXPERTS), D ** -0.5),
        'b_router': nrm(ks[16], (L, N_EXPERTS), 0.01),
        'w_gate_up': nrm(ks[17], (L, N_EXPERTS, D, 2 * D_FF_EXPERT), D ** -0.5),
        'b_gate_up': nrm(ks[18], (L, N_EXPERTS, 2 * D_FF_EXPERT), 0.01),
        'w_down': nrm(ks[19], (L, N_EXPERTS, D_FF_EXPERT, D), D_FF_EXPERT ** -0.5),
        'b_down': nrm(ks[20], (L, N_EXPERTS, D), 0.01),
    }


def reference(x, c, w_ada, b_ada, g_pre_mix, g_post_mix, g_pre_ffn, g_post_ffn, w_in, b_in, sinks, rel_bias,
              w_o_swa, w_o_moba, w_out, w_router, b_router, w_gate_up, b_gate_up, w_down, b_down):
    B, S, D = x.shape
    pos = jnp.arange(S)
    silu_c = jax.nn.silu(c)
    for l in range(DEPTH):
        mod = silu_c @ w_ada[l] + b_ada[l]
        sh1, sc1, gt1, sh2, sc2, gt2 = jnp.split(mod[:, None, :], 6, axis=-1)
        h = rms_norm(x, g_pre_mix[l]) * (1 + sc1) + sh1
        y = mixer_block(h, pos, w_in[l], b_in[l], sinks[l], rel_bias, w_o_swa[l], w_o_moba[l], w_out[l])
        x = x + gt1 * rms_norm(y, g_post_mix[l])
        h = rms_norm(x, g_pre_ffn[l]) * (1 + sc2) + sh2
        y = moe_ffn(h.reshape(B * S, D), w_router[l], b_router[l], w_gate_up[l], b_gate_up[l], w_down[l], b_down[l]).reshape(B, S, D)
        x = x + gt2 * rms_norm(y, g_post_ffn[l])
    return x
```

```python
import functools
import math

import numpy as np
import jax
import jax.numpy as jnp
from jax import lax
from jax.experimental import pallas as pl
from jax.experimental.pallas import tpu as pltpu

F32 = jnp.float32
BF16 = jnp.bfloat16
I32 = jnp.int32

HEAD_DIM_SWA = 64
SWA_GROUP = 8
WINDOW = 128
ROPE_THETA = 150000.0
HEAD_DIM_MOBA = 128
MOBA_GROUP = 4
MOBA_BLOCK = 256
MOBA_TOPK = 3
REL_BUCKETS = 32
REL_MAX_DIST = 1024
N_EXPERTS = 32
TOP_K = 4
SWIGLU_LIMIT = 7.0
SWIGLU_ALPHA = 1.702
MOE_BLOCK = 256
RMS_EPS = 1e-6

LANES = 128
SUBLANES = 8
VMEM_BYTES_V7X = 64 * 1024 * 1024
VMEM_CAP = VMEM_BYTES_V7X - 8 * 1024 * 1024

NEG_INF = float("-inf")
N_BIAS_TILES = (REL_MAX_DIST + MOBA_BLOCK - 1) // MOBA_BLOCK + 2


def _nbytes(shape, dtype):
    return int(np.prod(shape)) * jnp.dtype(dtype).itemsize


def _params(semantics, blocks, temps=()):
    need = 2 * sum(_nbytes(s, d) for s, d in blocks) + sum(_nbytes(s, d) for s, d in temps)
    need = max(need + need // 4, 16 * 1024 * 1024)
    return pltpu.CompilerParams(dimension_semantics=semantics, vmem_limit_bytes=min(need, VMEM_CAP))


def _tile(n, pref):
    t = min(n, pref)
    assert n % t == 0, (n, pref)
    return t


def _ada_kernel(c_ref, w_ref, b_ref, o_ref):
    c = c_ref[...]
    s = c * jax.nn.sigmoid(c)
    o_ref[...] = jnp.dot(s, w_ref[...], preferred_element_type=F32,
                         precision=lax.Precision.HIGHEST) + b_ref[...]


def _ada_mod(c, w, b):
    bsz, d = c.shape
    n = w.shape[1]
    rows = -(-bsz // SUBLANES) * SUBLANES
    cp = jnp.pad(c, ((0, rows - bsz), (0, 0)))
    tn = _tile(n, 512)
    out = pl.pallas_call(
        _ada_kernel,
        grid=(n // tn,),
        in_specs=[pl.BlockSpec((rows, d), lambda j: (0, 0)),
                  pl.BlockSpec((d, tn), lambda j: (0, j)),
                  pl.BlockSpec((1, tn), lambda j: (0, j))],
        out_specs=pl.BlockSpec((rows, tn), lambda j: (0, j)),
        out_shape=jax.ShapeDtypeStruct((rows, n), F32),
        compiler_params=_params(("arbitrary",), [((d, tn), F32), ((rows, d), F32)], [((d, tn), F32)]),
        name="ada_mod",
    )(cp, w, b.reshape(1, n))
    return out[:bsz].reshape(bsz, 6, d)


def _rms(x, g):
    return x * lax.rsqrt(jnp.mean(x * x, axis=-1, keepdims=True) + RMS_EPS) * g


def _prenorm_kernel(x_ref, g_ref, mod_ref, o_ref, *, shift_row, scale_row):
    m = mod_ref[0]
    y = _rms(x_ref[...], g_ref[...])
    o_ref[...] = (y * (1.0 + m[scale_row:scale_row + 1, :]) + m[shift_row:shift_row + 1, :]).astype(o_ref.dtype)


def _prenorm(x2, g, mod, seq, shift_row, scale_row):
    t, d = x2.shape
    tm = _tile(seq, 256)
    return pl.pallas_call(
        functools.partial(_prenorm_kernel, shift_row=shift_row, scale_row=scale_row),
        grid=(t // tm,),
        in_specs=[pl.BlockSpec((tm, d), lambda i: (i, 0)),
                  pl.BlockSpec((1, d), lambda i: (0, 0)),
                  pl.BlockSpec((1, 6, d), lambda i: (i * tm // seq, 0, 0))],
        out_specs=pl.BlockSpec((tm, d), lambda i: (i, 0)),
        out_shape=jax.ShapeDtypeStruct((t, d), BF16),
        compiler_params=_params(("parallel",), [((tm, d), F32), ((tm, d), BF16)], [((tm, d), F32)] * 2),
        name="prenorm",
    )(x2, g.reshape(1, d), mod)


def _mm_kernel(a_ref, w_ref, b_ref, o_ref):
    acc = jnp.dot(a_ref[...], w_ref[...], preferred_element_type=F32)
    o_ref[...] = (acc + b_ref[...]).astype(o_ref.dtype)


def _matmul(a, w, bias, out_dtype, tm, tn, name):
    m, k = a.shape
    n = w.shape[1]
    tm, tn = _tile(m, tm), _tile(n, tn)
    return pl.pallas_call(
        _mm_kernel,
        grid=(m // tm, n // tn),
        in_specs=[pl.BlockSpec((tm, k), lambda i, j: (i, 0)),
                  pl.BlockSpec((k, tn), lambda i, j: (0, j)),
                  pl.BlockSpec((1, tn), lambda i, j: (0, j))],
        out_specs=pl.BlockSpec((tm, tn), lambda i, j: (i, j)),
        out_shape=jax.ShapeDtypeStruct((m, n), out_dtype),
        compiler_params=_params(("parallel", "arbitrary"),
                                [((tm, k), a.dtype), ((k, tn), w.dtype), ((tm, tn), out_dtype)],
                                [((tm, tn), F32)]),
        name=name,
    )(a, w, bias.reshape(1, n).astype(F32))


def _rope_tables(seq):
    half = HEAD_DIM_SWA // 2
    inv = ROPE_THETA ** (-jnp.arange(half, dtype=F32) / half)
    ang = jnp.arange(seq, dtype=F32)[:, None] * inv[None, :]
    cos, sin = jnp.cos(ang), jnp.sin(ang)
    cos_t = jnp.concatenate([cos, cos, cos, cos], axis=-1)
    sin_t = jnp.concatenate([-sin, sin, -sin, sin], axis=-1)
    return cos_t, sin_t


def _swa_kernel(sinks_ref, q_ref, kp_ref, kc_ref, vp_ref, vc_ref, cq_ref, sq_ref, cp_ref, sp_ref, o_ref,
                *, n_kv):
    n = pl.program_id(1)
    w = WINDOW
    half = HEAD_DIM_SWA // 2
    lane = lax.broadcasted_iota(I32, (1, LANES), 1)
    first_half = (lane % HEAD_DIM_SWA) < half
    low_head = lane < HEAD_DIM_SWA

    def rope(xf, cos, sin):
        partner = jnp.where(first_half, pltpu.roll(xf, LANES - half, 1), pltpu.roll(xf, half, 1))
        return xf * cos + partner * sin

    cos_q, sin_q = cq_ref[...], sq_ref[...]
    cos_k = jnp.concatenate([cp_ref[...], cos_q], axis=0)
    sin_k = jnp.concatenate([sp_ref[...], sin_q], axis=0)
    ri = lax.broadcasted_iota(I32, (w, 2 * w), 0)
    cj = lax.broadcasted_iota(I32, (w, 2 * w), 1)
    diff = w + ri - cj
    mask = (diff >= 0) & (diff < w) & ((cj >= w) | (n > 0))
    pairs = SWA_GROUP // 2

    for g in range(n_kv):
        slab = g // 2
        cols = slice(slab * LANES, (slab + 1) * LANES)
        k2 = rope(jnp.concatenate([kp_ref[:, cols], kc_ref[:, cols]], axis=0).astype(F32), cos_k, sin_k)
        v2 = jnp.concatenate([vp_ref[:, cols], vc_ref[:, cols]], axis=0).astype(F32)
        k2s, v2s = pltpu.roll(k2, HEAD_DIM_SWA, 1), pltpu.roll(v2, HEAD_DIM_SWA, 1)
        if g % 2 == 0:
            k_lo, v_lo = jnp.where(low_head, k2, 0.0), jnp.where(low_head, v2, 0.0)
            k_hi, v_hi = jnp.where(low_head, 0.0, k2s), jnp.where(low_head, 0.0, v2s)
        else:
            k_lo, v_lo = jnp.where(low_head, k2s, 0.0), jnp.where(low_head, v2s, 0.0)
            k_hi, v_hi = jnp.where(low_head, 0.0, k2), jnp.where(low_head, 0.0, v2)
        kv = ((k_lo.astype(BF16), v_lo.astype(BF16)), (k_hi.astype(BF16), v_hi.astype(BF16)))
        for p in range(pairs):
            qs = g * pairs + p
            qcols = slice(qs * LANES, (qs + 1) * LANES)
            q = rope(q_ref[:, qcols].astype(F32), cos_q, sin_q) * (1.0 / math.sqrt(HEAD_DIM_SWA))
            q = q.astype(BF16)
            o = jnp.zeros((w, LANES), F32)
            for hh, (kk, vv) in enumerate(kv):
                s = lax.dot_general(q, kk, (((1,), (1,)), ((), ())), preferred_element_type=F32)
                s = jnp.where(mask, s, NEG_INF)
                sink = sinks_ref[2 * qs + hh]
                m = jnp.maximum(jnp.max(s, axis=-1, keepdims=True), sink)
                e = jnp.exp(s - m)
                den = jnp.sum(e, axis=-1, keepdims=True) + jnp.exp(sink - m)
                o = o + jnp.dot(e.astype(BF16), vv, preferred_element_type=F32) / den
            o_ref[:, qcols] = o.astype(o_ref.dtype)


def _swa_attention(proj, sinks, cos_t, sin_t, bsz, seq, d):
    q_w, kv_w = d // 2, d // 16
    n_kv = kv_w // HEAD_DIM_SWA
    assert n_kv % 2 == 0 and q_w % kv_w == 0
    nb = seq // WINDOW
    k_col, v_col = q_w // kv_w, q_w // kv_w + 1
    cur = lambda b, n: b * nb + n
    prev = lambda b, n: b * nb + jnp.maximum(n - 1, 0)
    return pl.pallas_call(
        functools.partial(_swa_kernel, n_kv=n_kv),
        grid=(bsz, nb),
        in_specs=[pl.BlockSpec(memory_space=pltpu.SMEM),
                  pl.BlockSpec((WINDOW, q_w), lambda b, n: (cur(b, n), 0)),
                  pl.BlockSpec((WINDOW, kv_w), lambda b, n: (prev(b, n), k_col)),
                  pl.BlockSpec((WINDOW, kv_w), lambda b, n: (cur(b, n), k_col)),
                  pl.BlockSpec((WINDOW, kv_w), lambda b, n: (prev(b, n), v_col)),
                  pl.BlockSpec((WINDOW, kv_w), lambda b, n: (cur(b, n), v_col)),
                  pl.BlockSpec((WINDOW, LANES), lambda b, n: (n, 0)),
                  pl.BlockSpec((WINDOW, LANES), lambda b, n: (n, 0)),
                  pl.BlockSpec((WINDOW, LANES), lambda b, n: (jnp.maximum(n - 1, 0), 0)),
                  pl.BlockSpec((WINDOW, LANES), lambda b, n: (jnp.maximum(n - 1, 0), 0))],
        out_specs=pl.BlockSpec((WINDOW, q_w), lambda b, n: (cur(b, n), 0)),
        out_shape=jax.ShapeDtypeStruct((bsz * seq, q_w), BF16),
        compiler_params=_params(("parallel", "arbitrary"),
                                [((WINDOW, q_w), BF16)] * 2 + [((WINDOW, kv_w), BF16)] * 4,
                                [((2 * WINDOW, 2 * WINDOW), F32)] * 16),
        name="swa_attn",
    )(sinks, proj, proj, proj, proj, proj, cos_t, sin_t, cos_t, sin_t)


def _t5_thresholds():
    exact = REL_BUCKETS // 2
    d = np.arange(exact, 2 * REL_MAX_DIST, dtype=np.float64)
    large = exact + np.floor(np.log(d / exact) / math.log(REL_MAX_DIST / exact) * (REL_BUCKETS - exact)).astype(np.int64)
    large = np.minimum(large, REL_BUCKETS - 1)
    return [int(exact + np.argmax(large >= b)) for b in range(exact + 1, REL_BUCKETS)]


def _bias_tile_kernel(table_ref, o_ref):
    dd = pl.program_id(0)
    h = pl.program_id(1)
    exact = REL_BUCKETS // 2
    r = lax.broadcasted_iota(I32, (MOBA_BLOCK, MOBA_BLOCK), 0)
    c = lax.broadcasted_iota(I32, (MOBA_BLOCK, MOBA_BLOCK), 1)
    dist = jnp.maximum(dd * MOBA_BLOCK + r - c, 0)
    bucket = jnp.minimum(dist, exact)
    for thr in _t5_thresholds():
        bucket = bucket + (dist >= thr).astype(I32)
    val = jnp.full((MOBA_BLOCK, MOBA_BLOCK), table_ref[REL_BUCKETS - 1, h], F32)
    for b in range(REL_BUCKETS - 2, -1, -1):
        val = jnp.where(bucket == b, table_ref[b, h], val)
    o_ref[0, 0] = val


def _bias_tiles(rel_bias):
    n_heads = rel_bias.shape[1]
    return pl.pallas_call(
        _bias_tile_kernel,
        grid=(N_BIAS_TILES, n_heads),
        in_specs=[pl.BlockSpec(memory_space=pltpu.SMEM)],
        out_specs=pl.BlockSpec((1, 1, MOBA_BLOCK, MOBA_BLOCK), lambda dd, h: (h, dd, 0, 0)),
        out_shape=jax.ShapeDtypeStruct((n_heads, N_BIAS_TILES, MOBA_BLOCK, MOBA_BLOCK), F32),
        compiler_params=_params(("arbitrary", "arbitrary"), [((MOBA_BLOCK, MOBA_BLOCK), F32)],
                                [((MOBA_BLOCK, MOBA_BLOCK), F32)] * 4),
        name="moba_bias_tiles",
    )(rel_bias)


def _kmean_kernel(k_ref, o_ref, *, nblk):
    for j in range(nblk):
        rows = k_ref[j * MOBA_BLOCK:(j + 1) * MOBA_BLOCK, :].astype(F32)
        o_ref[j:j + 1, :] = jnp.mean(rows, axis=0, keepdims=True)


def _kmean(proj, bsz, seq, d):
    kv_w = d // 8
    nblk = seq // MOBA_BLOCK
    col = (9 * d // 8) // kv_w
    return pl.pallas_call(
        functools.partial(_kmean_kernel, nblk=nblk),
        grid=(bsz,),
        in_specs=[pl.BlockSpec((seq, kv_w), lambda b: (b, col))],
        out_specs=pl.BlockSpec((nblk, kv_w), lambda b: (b, 0)),
        out_shape=jax.ShapeDtypeStruct((bsz * nblk, kv_w), F32),
        compiler_params=_params(("parallel",), [((seq, kv_w), BF16)], [((MOBA_BLOCK, kv_w), F32)] * 2),
        name="moba_kmean",
    )(proj)


def _moba_kernel(q_ref, k_ref, v_ref, km_ref, bias_ref, o_ref, *, nblk):
    i = pl.program_id(2)
    mb, dh = MOBA_BLOCK, HEAD_DIM_MOBA
    scale = 1.0 / math.sqrt(dh)
    nt = (((1,), (1,)), ((), ()))
    blk = lax.broadcasted_iota(I32, (mb, nblk), 1)
    ri = lax.broadcasted_iota(I32, (mb, mb), 0)
    cj = lax.broadcasted_iota(I32, (mb, mb), 1)
    causal = ri >= cj
    kmean = km_ref[...]
    own = pl.multiple_of(i * mb, mb)
    k_own, v_own = k_ref[pl.ds(own, mb), :], v_ref[pl.ds(own, mb), :]

    for hh in range(MOBA_GROUP):
        cols = slice(hh * dh, (hh + 1) * dh)
        q = q_ref[:, cols]
        gate = lax.dot_general(q.astype(F32), kmean, nt, preferred_element_type=F32,
                               precision=lax.Precision.HIGHEST)
        gate = jnp.where(blk < i, gate, NEG_INF)
        picks = []
        for t in range(MOBA_TOPK):
            gmax = jnp.max(gate, axis=-1, keepdims=True)
            idx = jnp.min(jnp.where(gate == gmax, blk, nblk), axis=-1, keepdims=True)
            picks.append(jnp.where(t < i, idx, -1))
            gate = jnp.where(blk == idx, NEG_INF, gate)

        s = lax.dot_general(q, k_own, nt, preferred_element_type=F32) * scale + bias_ref[hh, 0]
        s = jnp.where(causal, s, NEG_INF)
        m0 = jnp.max(s, axis=-1, keepdims=True)
        e = jnp.exp(s - m0)
        l0 = jnp.sum(e, axis=-1, keepdims=True)
        a0 = jnp.dot(e.astype(BF16), v_own, preferred_element_type=F32)

        def past(j, carry, q=q, picks=picks, hh=hh):
            m, l, acc = carry
            start = pl.multiple_of(j * mb, mb)
            kj, vj = k_ref[pl.ds(start, mb), :], v_ref[pl.ds(start, mb), :]
            bias = bias_ref[hh, jnp.minimum(i - j, N_BIAS_TILES - 1)]
            s = lax.dot_general(q, kj, nt, preferred_element_type=F32) * scale + bias
            chosen = (picks[0] == j) | (picks[1] == j) | (picks[2] == j)
            s = jnp.where(chosen, s, NEG_INF)
            m_new = jnp.maximum(m, jnp.max(s, axis=-1, keepdims=True))
            alpha = jnp.exp(m - m_new)
            e = jnp.exp(s - m_new)
            l = alpha * l + jnp.sum(e, axis=-1, keepdims=True)
            acc = alpha * acc + jnp.dot(e.astype(BF16), vj, preferred_element_type=F32)
            return m_new, l, acc

        _, l, acc = lax.fori_loop(0, i, past, (m0, l0, a0))
        o_ref[:, cols] = (acc / l).astype(o_ref.dtype)


def _moba_attention(proj, kmean, bias_tiles, bsz, seq, d):
    gw = MOBA_GROUP * HEAD_DIM_MOBA
    n_kv = (d // 8) // HEAD_DIM_MOBA
    nblk = seq // MOBA_BLOCK
    q_off, k_off, v_off = 5 * d // 8, 9 * d // 8, 10 * d // 8
    assert q_off % gw == 0 and MOBA_TOPK == 3
    qc, kc, vc = q_off // gw, k_off // HEAD_DIM_MOBA, v_off // HEAD_DIM_MOBA
    tile_shape = (MOBA_GROUP, N_BIAS_TILES, MOBA_BLOCK, MOBA_BLOCK)
    return pl.pallas_call(
        functools.partial(_moba_kernel, nblk=nblk),
        grid=(n_kv, bsz, nblk),
        in_specs=[pl.BlockSpec((MOBA_BLOCK, gw), lambda g, b, i: (b * nblk + i, qc + g)),
                  pl.BlockSpec((seq, HEAD_DIM_MOBA), lambda g, b, i: (b, kc + g)),
                  pl.BlockSpec((seq, HEAD_DIM_MOBA), lambda g, b, i: (b, vc + g)),
                  pl.BlockSpec((nblk, HEAD_DIM_MOBA), lambda g, b, i: (b, g)),
                  pl.BlockSpec(tile_shape, lambda g, b, i: (g, 0, 0, 0))],
        out_specs=pl.BlockSpec((MOBA_BLOCK, gw), lambda g, b, i: (b * nblk + i, g)),
        out_shape=jax.ShapeDtypeStruct((bsz * seq, d // 2), BF16),
        compiler_params=_params(("parallel", "parallel", "arbitrary"),
                                [(tile_shape, F32), ((seq, HEAD_DIM_MOBA), BF16), ((seq, HEAD_DIM_MOBA), BF16),
                                 ((MOBA_BLOCK, gw), BF16), ((MOBA_BLOCK, gw), BF16)],
                                [((MOBA_BLOCK, MOBA_BLOCK), F32)] * 8),
        name="moba_attn",
    )(proj, proj, proj, kmean, bias_tiles)


def _merge_kernel(oa_ref, ob_ref, wa_ref, wb_ref, ga_ref, gb_ref, o_ref):
    ya = jnp.dot(oa_ref[...], wa_ref[...], preferred_element_type=F32)
    yb = jnp.dot(ob_ref[...], wb_ref[...], preferred_element_type=F32)
    merged = jax.nn.sigmoid(ga_ref[...].astype(F32)) * ya + jax.nn.sigmoid(gb_ref[...].astype(F32)) * yb
    o_ref[...] = merged.astype(o_ref.dtype)


def _merge(o_a, o_b, w_a, w_b, proj, d):
    t, kdim = o_a.shape
    tm, tn = _tile(t, 1024), _tile(d, 512)
    ga_col, gb_col = (11 * d // 8) // tn, (19 * d // 8) // tn
    assert (11 * d // 8) % tn == 0 and (19 * d // 8) % tn == 0
    return pl.pallas_call(
        _merge_kernel,
        grid=(t // tm, d // tn),
        in_specs=[pl.BlockSpec((tm, kdim), lambda i, j: (i, 0)),
                  pl.BlockSpec((tm, kdim), lambda i, j: (i, 0)),
                  pl.BlockSpec((kdim, tn), lambda i, j: (0, j)),
                  pl.BlockSpec((kdim, tn), lambda i, j: (0, j)),
                  pl.BlockSpec((tm, tn), lambda i, j: (i, ga_col + j)),
                  pl.BlockSpec((tm, tn), lambda i, j: (i, gb_col + j))],
        out_specs=pl.BlockSpec((tm, tn), lambda i, j: (i, j)),
        out_shape=jax.ShapeDtypeStruct((t, d), BF16),
        compiler_params=_params(("parallel", "arbitrary"),
                                [((tm, kdim), BF16)] * 2 + [((kdim, tn), BF16)] * 2 + [((tm, tn), BF16)] * 3,
                                [((tm, tn), F32)] * 3),
        name="merge",
    )(o_a, o_b, w_a, w_b, proj, proj)


def _post_mix_kernel(y_ref, x_ref, gpost_ref, gpre_ref, mod_ref, wr_ref, br_ref,
                     x1_ref, h2_ref, idx_ref, wts_ref):
    m = mod_ref[0]
    x1 = x_ref[...] + m[2:3, :] * _rms(y_ref[...], gpost_ref[...])
    x1_ref[...] = x1
    h2 = _rms(x1, gpre_ref[...]) * (1.0 + m[4:5, :]) + m[3:4, :]
    h2_ref[...] = h2
    logits = jnp.dot(h2, wr_ref[...], preferred_element_type=F32,
                     precision=lax.Precision.HIGHEST) + br_ref[...]
    lane = lax.broadcasted_iota(I32, logits.shape, 1)
    logits = jnp.where(lane < N_EXPERTS, logits, NEG_INF)
    idx_out = jnp.zeros(logits.shape, I32)
    val_out = jnp.zeros(logits.shape, F32)
    top = None
    den = jnp.zeros((logits.shape[0], 1), F32)
    for k in range(TOP_K):
        vmax = jnp.max(logits, axis=-1, keepdims=True)
        idx = jnp.min(jnp.where(logits == vmax, lane, LANES), axis=-1, keepdims=True)
        top = vmax if top is None else top
        e = jnp.exp(vmax - top)
        den = den + e
        idx_out = jnp.where(lane == k, idx, idx_out)
        val_out = jnp.where(lane == k, e, val_out)
        logits = jnp.where(lane == idx, NEG_INF, logits)
    idx_ref[...] = idx_out
    wts_ref[...] = val_out / den


def _post_mix(y, x2, g_post, g_pre, mod, w_router, b_router, seq):
    t, d = x2.shape
    tm = _tile(seq, 256)
    wr = jnp.pad(w_router, ((0, 0), (0, LANES - N_EXPERTS)))
    br = jnp.pad(b_router, (0, LANES - N_EXPERTS)).reshape(1, LANES)
    row = lambda i: (i, 0)
    const = lambda i: (0, 0)
    return pl.pallas_call(
        _post_mix_kernel,
        grid=(t // tm,),
        in_specs=[pl.BlockSpec((tm, d), row), pl.BlockSpec((tm, d), row),
                  pl.BlockSpec((1, d), const), pl.BlockSpec((1, d), const),
                  pl.BlockSpec((1, 6, d), lambda i: (i * tm // seq, 0, 0)),
                  pl.BlockSpec((d, LANES), const), pl.BlockSpec((1, LANES), const)],
        out_specs=[pl.BlockSpec((tm, d), row), pl.BlockSpec((tm, d), row),
                   pl.BlockSpec((tm, LANES), row), pl.BlockSpec((tm, LANES), row)],
        out_shape=[jax.ShapeDtypeStruct((t, d), F32), jax.ShapeDtypeStruct((t, d), F32),
                   jax.ShapeDtypeStruct((t, LANES), I32), jax.ShapeDtypeStruct((t, LANES), F32)],
        compiler_params=_params(("parallel",), [((tm, d), F32)] * 4 + [((d, LANES), F32)],
                                [((tm, d), F32)] * 3),
        name="post_mix_router",
    )(y, x2, g_post.reshape(1, d), g_pre.reshape(1, d), mod, wr, br)


def _routing(top_idx, n_blocks):
    flat_e = top_idx.reshape(-1)
    onehot = (flat_e[:, None] == jnp.arange(N_EXPERTS, dtype=I32)[None, :]).astype(I32)
    csum = jnp.cumsum(onehot, axis=0)
    counts = csum[-1]
    rank = jnp.sum(csum * onehot, axis=1) - 1
    padded = (counts + MOE_BLOCK - 1) // MOE_BLOCK * MOE_BLOCK
    pad_end = jnp.cumsum(padded)
    pad_start = pad_end - padded
    dest = jnp.sum(onehot * pad_start[None, :], axis=1) + rank
    n_real = (pad_end[-1] // MOE_BLOCK).astype(I32)
    blk_start = jnp.arange(n_blocks, dtype=I32) * MOE_BLOCK
    block_e = jnp.minimum(jnp.sum((pad_end[None, :] <= blk_start[:, None]).astype(I32), axis=1), N_EXPERTS - 1)
    last_blk = jnp.maximum(pad_end // MOE_BLOCK - 1, 0).astype(I32)
    return dest.astype(I32), block_e.astype(I32), n_real.reshape(1), last_blk


def _zero_tail_kernel(last_ref, o_ref):
    o_ref[...] = jnp.zeros(o_ref.shape, o_ref.dtype)


def _zero_tail_blocks(last_blk, rows, d):
    return pl.pallas_call(
        _zero_tail_kernel,
        grid_spec=pltpu.PrefetchScalarGridSpec(
            num_scalar_prefetch=1, grid=(N_EXPERTS,), in_specs=[],
            out_specs=pl.BlockSpec((MOE_BLOCK, d), lambda e, last: (last[e], 0))),
        out_shape=jax.ShapeDtypeStruct((rows, d), F32),
        compiler_params=_params(("arbitrary",), [((MOE_BLOCK, d), F32)]),
        name="moe_zero_tail",
    )(last_blk)


def _dispatch_kernel(dest_ref, h_hbm, xg_in, xg_hbm, sem, *, tokens):
    del xg_in
    t0 = pl.program_id(0) * tokens

    def issue(t, carry):
        for k in range(TOP_K):
            row = dest_ref[0, 0, t * TOP_K + k]
            pltpu.make_async_copy(h_hbm.at[pl.ds(t0 + t, 1)], xg_hbm.at[pl.ds(row, 1)], sem).start()
        return carry

    lax.fori_loop(0, tokens, issue, 0)

    def drain(t, carry):
        for k in range(TOP_K):
            pltpu.make_async_copy(h_hbm.at[pl.ds(0, 1)], xg_hbm.at[pl.ds(0, 1)], sem).wait()
        return carry

    lax.fori_loop(0, tokens, drain, 0)


def _dispatch(h2, dest, xg):
    t, d = h2.shape
    tokens = _tile(t, 512)
    steps = t // tokens
    return pl.pallas_call(
        functools.partial(_dispatch_kernel, tokens=tokens),
        grid=(steps,),
        in_specs=[pl.BlockSpec((1, 1, tokens * TOP_K), lambda i: (i, 0, 0), memory_space=pltpu.SMEM),
                  pl.BlockSpec(memory_space=pl.ANY),
                  pl.BlockSpec(memory_space=pl.ANY)],
        out_specs=pl.BlockSpec(memory_space=pl.ANY),
        out_shape=jax.ShapeDtypeStruct(xg.shape, xg.dtype),
        scratch_shapes=[pltpu.SemaphoreType.DMA(())],
        input_output_aliases={2: 0},
        compiler_params=pltpu.CompilerParams(dimension_semantics=("arbitrary",), has_side_effects=True),
        name="moe_dispatch",
    )(dest.reshape(steps, 1, tokens * TOP_K), h2, xg)


def _gate_up_kernel(be_ref, nr_ref, x_ref, wg_ref, wl_ref, bg_ref, bl_ref, o_ref):
    @pl.when(pl.program_id(1) < nr_ref[0])
    def _():
        x = x_ref[...].astype(BF16)
        g = jnp.dot(x, wg_ref[0], preferred_element_type=F32) + bg_ref[0]
        lin = jnp.dot(x, wl_ref[0], preferred_element_type=F32) + bl_ref[0]
        glu = jnp.minimum(g, SWIGLU_LIMIT)
        lin = jnp.clip(lin, -SWIGLU_LIMIT, SWIGLU_LIMIT)
        o_ref[...] = (glu * jax.nn.sigmoid(SWIGLU_ALPHA * glu) * (lin + 1.0)).astype(o_ref.dtype)


def _gate_up(xg, w_g, w_l, b_g, b_l, block_e, n_real, n_blocks):
    rows, d = xg.shape
    f = w_g.shape[2]
    passes = 2
    tf = f // passes
    blk = lambda p, m, be, nr: jnp.minimum(m, nr[0] - 1)
    wmap = lambda p, m, be, nr: (be[blk(p, m, be, nr)], 0, p)
    return pl.pallas_call(
        _gate_up_kernel,
        grid_spec=pltpu.PrefetchScalarGridSpec(
            num_scalar_prefetch=2, grid=(passes, n_blocks),
            in_specs=[pl.BlockSpec((MOE_BLOCK, d), lambda p, m, be, nr: (blk(p, m, be, nr), 0)),
                      pl.BlockSpec((1, d, tf), wmap), pl.BlockSpec((1, d, tf), wmap),
                      pl.BlockSpec((1, 1, tf), wmap), pl.BlockSpec((1, 1, tf), wmap)],
            out_specs=pl.BlockSpec((MOE_BLOCK, tf), lambda p, m, be, nr: (blk(p, m, be, nr), p))),
        out_shape=jax.ShapeDtypeStruct((rows, f), BF16),
        compiler_params=_params(("arbitrary", "arbitrary"),
                                [((MOE_BLOCK, d), F32), ((d, tf), BF16), ((d, tf), BF16), ((MOE_BLOCK, tf), BF16)],
                                [((MOE_BLOCK, d), BF16)] + [((MOE_BLOCK, tf), F32)] * 4),
        name="moe_gate_up",
    )(block_e, n_real, xg, w_g, w_l, b_g, b_l)


def _down_kernel(be_ref, nr_ref, a_ref, w_ref, b_ref, o_ref):
    @pl.when(pl.program_id(0) < nr_ref[0])
    def _():
        o_ref[...] = jnp.dot(a_ref[...], w_ref[0], preferred_element_type=F32) + b_ref[0]


def _down(act, w_d, b_d, block_e, n_real, n_blocks):
    rows, f = act.shape
    d = w_d.shape[2]
    blk = lambda m, be, nr: jnp.minimum(m, nr[0] - 1)
    wmap = lambda m, be, nr: (be[blk(m, be, nr)], 0, 0)
    return pl.pallas_call(
        _down_kernel,
        grid_spec=pltpu.PrefetchScalarGridSpec(
            num_scalar_prefetch=2, grid=(n_blocks,),
            in_specs=[pl.BlockSpec((MOE_BLOCK, f), lambda m, be, nr: (blk(m, be, nr), 0)),
                      pl.BlockSpec((1, f, d), wmap), pl.BlockSpec((1, 1, d), wmap)],
            out_specs=pl.BlockSpec((MOE_BLOCK, d), lambda m, be, nr: (blk(m, be, nr), 0))),
        out_shape=jax.ShapeDtypeStruct((rows, d), F32),
        compiler_params=_params(("arbitrary",),
                                [((MOE_BLOCK, f), BF16), ((f, d), BF16), ((MOE_BLOCK, d), F32)],
                                [((MOE_BLOCK, d), F32)]),
        name="moe_down",
    )(block_e, n_real, act, w_d, b_d)


def _combine_kernel(dest_ref, w_ref, x1_ref, g_ref, mod_ref, y_hbm, o_ref, buf, sem, *, tokens):
    def issue(t, carry):
        for k in range(TOP_K):
            row = dest_ref[0, 0, t * TOP_K + k]
            pltpu.make_async_copy(y_hbm.at[pl.ds(row, 1)], buf.at[k, pl.ds(t, 1)], sem).start()
        return carry

    lax.fori_loop(0, tokens, issue, 0)

    def drain(t, carry):
        for k in range(TOP_K):
            pltpu.make_async_copy(y_hbm.at[pl.ds(0, 1)], buf.at[k, pl.ds(0, 1)], sem).wait()
        return carry

    lax.fori_loop(0, tokens, drain, 0)

    w = w_ref[...]
    acc = buf[0] * w[:, 0:1]
    for k in range(1, TOP_K):
        acc = acc + buf[k] * w[:, k:k + 1]
    m = mod_ref[0]
    o_ref[...] = x1_ref[...] + m[5:6, :] * _rms(acc, g_ref[...])


def _combine(y_sorted, dest, top_w, x1, g_post, mod, seq):
    t, d = x1.shape
    tokens = _tile(seq, 128)
    steps = t // tokens
    row = lambda i: (i, 0)
    return pl.pallas_call(
        functools.partial(_combine_kernel, tokens=tokens),
        grid=(steps,),
        in_specs=[pl.BlockSpec((1, 1, tokens * TOP_K), lambda i: (i, 0, 0), memory_space=pltpu.SMEM),
                  pl.BlockSpec((tokens, LANES), row),
                  pl.BlockSpec((tokens, d), row),
                  pl.BlockSpec((1, d), lambda i: (0, 0)),
                  pl.BlockSpec((1, 6, d), lambda i: (i * tokens // seq, 0, 0)),
                  pl.BlockSpec(memory_space=pl.ANY)],
        out_specs=pl.BlockSpec((tokens, d), row),
        out_shape=jax.ShapeDtypeStruct((t, d), F32),
        scratch_shapes=[pltpu.VMEM((TOP_K, tokens, d), F32), pltpu.SemaphoreType.DMA(())],
        compiler_params=_params(("arbitrary",), [((tokens, d), F32)] * 2,
                                [((TOP_K, tokens, d), F32)] + [((tokens, d), F32)] * 2),
        name="moe_combine",
    )(dest.reshape(steps, 1, tokens * TOP_K), top_w, x1, g_post.reshape(1, d), mod, y_sorted)


def _moe(h2, top_idx, top_w, x1, g_post, mod, w_gu, b_gu, w_dn, b_dn, seq):
    t, d = h2.shape
    f = w_dn.shape[1]
    n_blocks = -(-(t * TOP_K) // MOE_BLOCK) + N_EXPERTS
    rows = n_blocks * MOE_BLOCK
    dest, block_e, n_real, last_blk = _routing(top_idx, n_blocks)
    xg = _dispatch(h2, dest, _zero_tail_blocks(last_blk, rows, d))
    w_g, w_l = w_gu[:, :, 0::2].astype(BF16), w_gu[:, :, 1::2].astype(BF16)
    b_g = b_gu[:, 0::2].reshape(N_EXPERTS, 1, f)
    b_l = b_gu[:, 1::2].reshape(N_EXPERTS, 1, f)
    act = _gate_up(xg, w_g, w_l, b_g, b_l, block_e, n_real, n_blocks)
    y_sorted = _down(act, w_dn.astype(BF16), b_dn.reshape(N_EXPERTS, 1, d), block_e, n_real, n_blocks)
    return _combine(y_sorted, dest, top_w, x1, g_post, mod, seq)


def kernel(x, c, w_ada, b_ada, g_pre_mix, g_post_mix, g_pre_ffn, g_post_ffn, w_in, b_in, sinks, rel_bias,
           w_o_swa, w_o_moba, w_out, w_router, b_router, w_gate_up, b_gate_up, w_down, b_down):
    bsz, seq, d = x.shape
    depth = w_ada.shape[0]
    assert seq % MOBA_BLOCK == 0 and d % 1024 == 0
    x2 = x.reshape(bsz * seq, d)
    cos_t, sin_t = _rope_tables(seq)
    bias_tiles = _bias_tiles(rel_bias)
    for l in range(depth):
        mod = _ada_mod(c, w_ada[l], b_ada[l])
        h = _prenorm(x2, g_pre_mix[l], mod, seq, shift_row=0, scale_row=1)
        proj = _matmul(h, w_in[l].astype(BF16), b_in[l], BF16, 1024, 768, "in_proj")
        o_a = _swa_attention(proj, sinks[l], cos_t, sin_t, bsz, seq, d)
        o_b = _moba_attention(proj, _kmean(proj, bsz, seq, d), bias_tiles, bsz, seq, d)
        merged = _merge(o_a, o_b, w_o_swa[l].astype(BF16), w_o_moba[l].astype(BF16), proj, d)
        y = _matmul(merged, w_out[l].astype(BF16), jnp.zeros((d,), F32), F32, 1024, 1024, "out_proj")
        x1, h2, idx, wts = _post_mix(y, x2, g_post_mix[l], g_pre_ffn[l], mod, w_router[l], b_router[l], seq)
        x2 = _moe(h2, idx[:, :TOP_K], wts, x1, g_post_ffn[l], mod,
                  w_gate_up[l], b_gate_up[l], w_down[l], b_down[l], seq)
    return x2.reshape(bsz, seq, d)
```

```python
import functools
import math

import numpy as np
import jax
import jax.numpy as jnp
from jax import lax
from jax.experimental import pallas as pl
from jax.experimental.pallas import tpu as pltpu

F32 = jnp.float32
BF16 = jnp.bfloat16
I32 = jnp.int32

HEAD_DIM_SWA = 64
SWA_GROUP = 8
WINDOW = 128
ROPE_THETA = 150000.0
HEAD_DIM_MOBA = 128
MOBA_GROUP = 4
MOBA_BLOCK = 256
MOBA_TOPK = 3
REL_BUCKETS = 32
REL_MAX_DIST = 1024
N_EXPERTS = 32
TOP_K = 4
SWIGLU_LIMIT = 7.0
SWIGLU_ALPHA = 1.702
MOE_BLOCK = 256
RMS_EPS = 1e-6

LANES = 128
SUBLANES = 8
VMEM_BYTES_V7X = 64 * 1024 * 1024
VMEM_CAP = VMEM_BYTES_V7X - 8 * 1024 * 1024

NEG_INF = float("-inf")
N_BIAS_TILES = (REL_MAX_DIST + MOBA_BLOCK - 1) // MOBA_BLOCK + 2


def _nbytes(shape, dtype):
    return int(np.prod(shape)) * jnp.dtype(dtype).itemsize


def _params(semantics, blocks, temps=()):
    need = 2 * sum(_nbytes(s, d) for s, d in blocks) + sum(_nbytes(s, d) for s, d in temps)
    need = max(need + need // 4, 16 * 1024 * 1024)
    return pltpu.CompilerParams(dimension_semantics=semantics, vmem_limit_bytes=min(need, VMEM_CAP))


def _tile(n, pref):
    t = min(n, pref)
    assert n % t == 0, (n, pref)
    return t


def _ada_kernel(c_ref, w_ref, b_ref, o_ref):
    c = c_ref[...]
    s = c * jax.nn.sigmoid(c)
    o_ref[...] = jnp.dot(s, w_ref[...], preferred_element_type=F32,
                         precision=lax.Precision.HIGHEST) + b_ref[...]


def _ada_mod(c, w, b):
    bsz, d = c.shape
    n = w.shape[1]
    rows = -(-bsz // SUBLANES) * SUBLANES
    cp = jnp.pad(c, ((0, rows - bsz), (0, 0)))
    tn = _tile(n, 512)
    out = pl.pallas_call(
        _ada_kernel,
        grid=(n // tn,),
        in_specs=[pl.BlockSpec((rows, d), lambda j: (0, 0)),
                  pl.BlockSpec((d, tn), lambda j: (0, j)),
                  pl.BlockSpec((1, tn), lambda j: (0, j))],
        out_specs=pl.BlockSpec((rows, tn), lambda j: (0, j)),
        out_shape=jax.ShapeDtypeStruct((rows, n), F32),
        compiler_params=_params(("arbitrary",), [((d, tn), F32), ((rows, d), F32)], [((d, tn), F32)]),
        name="ada_mod",
    )(cp, w, b.reshape(1, n))
    return out[:bsz].reshape(bsz, 6, d)


def _rms(x, g):
    return x * lax.rsqrt(jnp.mean(x * x, axis=-1, keepdims=True) + RMS_EPS) * g


def _prenorm_kernel(x_ref, g_ref, mod_ref, o_ref, *, shift_row, scale_row):
    m = mod_ref[0]
    y = _rms(x_ref[...], g_ref[...])
    o_ref[...] = (y * (1.0 + m[scale_row:scale_row + 1, :]) + m[shift_row:shift_row + 1, :]).astype(o_ref.dtype)


def _prenorm(x2, g, mod, seq, shift_row, scale_row):
    t, d = x2.shape
    tm = _tile(seq, 256)
    return pl.pallas_call(
        functools.partial(_prenorm_kernel, shift_row=shift_row, scale_row=scale_row),
        grid=(t // tm,),
        in_specs=[pl.BlockSpec((tm, d), lambda i: (i, 0)),
                  pl.BlockSpec((1, d), lambda i: (0, 0)),
                  pl.BlockSpec((1, 6, d), lambda i: (i * tm // seq, 0, 0))],
        out_specs=pl.BlockSpec((tm, d), lambda i: (i, 0)),
        out_shape=jax.ShapeDtypeStruct((t, d), BF16),
        compiler_params=_params(("parallel",), [((tm, d), F32), ((tm, d), BF16)], [((tm, d), F32)] * 2),
        name="prenorm",
    )(x2, g.reshape(1, d), mod)


def _mm_kernel(a_ref, w_ref, b_ref, o_ref):
    acc = jnp.dot(a_ref[...], w_ref[...], preferred_element_type=F32)
    o_ref[...] = (acc + b_ref[...]).astype(o_ref.dtype)


def _matmul(a, w, bias, out_dtype, tm, tn, name):
    m, k = a.shape
    n = w.shape[1]
    tm, tn = _tile(m, tm), _tile(n, tn)
    return pl.pallas_call(
        _mm_kernel,
        grid=(m // tm, n // tn),
        in_specs=[pl.BlockSpec((tm, k), lambda i, j: (i, 0)),
                  pl.BlockSpec((k, tn), lambda i, j: (0, j)),
                  pl.BlockSpec((1, tn), lambda i, j: (0, j))],
        out_specs=pl.BlockSpec((tm, tn), lambda i, j: (i, j)),
        out_shape=jax.ShapeDtypeStruct((m, n), out_dtype),
        compiler_params=_params(("parallel", "arbitrary"),
                                [((tm, k), a.dtype), ((k, tn), w.dtype), ((tm, tn), out_dtype)],
                                [((tm, tn), F32)]),
        name=name,
    )(a, w, bias.reshape(1, n).astype(F32))


def _rope_tables(seq):
    half = HEAD_DIM_SWA // 2
    inv = ROPE_THETA ** (-jnp.arange(half, dtype=F32) / half)
    ang = jnp.arange(seq, dtype=F32)[:, None] * inv[None, :]
    cos, sin = jnp.cos(ang), jnp.sin(ang)
    cos_t = jnp.concatenate([cos, cos, cos, cos], axis=-1)
    sin_t = jnp.concatenate([-sin, sin, -sin, sin], axis=-1)
    return cos_t, sin_t


def _swa_kernel(sinks_ref, q_ref, kp_ref, kc_ref, vp_ref, vc_ref, cq_ref, sq_ref, cp_ref, sp_ref, o_ref,
                *, n_kv):
    n = pl.program_id(1)
    w = WINDOW
    half = HEAD_DIM_SWA // 2
    lane = lax.broadcasted_iota(I32, (1, LANES), 1)
    first_half = (lane % HEAD_DIM_SWA) < half
    low_head = lane < HEAD_DIM_SWA

    def rope(xf, cos, sin):
        partner = jnp.where(first_half, pltpu.roll(xf, LANES - half, 1), pltpu.roll(xf, half, 1))
        return xf * cos + partner * sin

    cos_q, sin_q = cq_ref[...], sq_ref[...]
    cos_k = jnp.concatenate([cp_ref[...], cos_q], axis=0)
    sin_k = jnp.concatenate([sp_ref[...], sin_q], axis=0)
    ri = lax.broadcasted_iota(I32, (w, 2 * w), 0)
    cj = lax.broadcasted_iota(I32, (w, 2 * w), 1)
    diff = w + ri - cj
    mask = (diff >= 0) & (diff < w) & ((cj >= w) | (n > 0))
    pairs = SWA_GROUP // 2

    for g in range(n_kv):
        slab = g // 2
        cols = slice(slab * LANES, (slab + 1) * LANES)
        k2 = rope(jnp.concatenate([kp_ref[:, cols], kc_ref[:, cols]], axis=0).astype(F32), cos_k, sin_k)
        v2 = jnp.concatenate([vp_ref[:, cols], vc_ref[:, cols]], axis=0).astype(F32)
        k2s, v2s = pltpu.roll(k2, HEAD_DIM_SWA, 1), pltpu.roll(v2, HEAD_DIM_SWA, 1)
        if g % 2 == 0:
            k_lo, v_lo = jnp.where(low_head, k2, 0.0), jnp.where(low_head, v2, 0.0)
            k_hi, v_hi = jnp.where(low_head, 0.0, k2s), jnp.where(low_head, 0.0, v2s)
        else:
            k_lo, v_lo = jnp.where(low_head, k2s, 0.0), jnp.where(low_head, v2s, 0.0)
            k_hi, v_hi = jnp.where(low_head, 0.0, k2), jnp.where(low_head, 0.0, v2)
        kv = ((k_lo.astype(BF16), v_lo.astype(BF16)), (k_hi.astype(BF16), v_hi.astype(BF16)))
        for p in range(pairs):
            qs = g * pairs + p
            qcols = slice(qs * LANES, (qs + 1) * LANES)
            q = rope(q_ref[:, qcols].astype(F32), cos_q, sin_q) * (1.0 / math.sqrt(HEAD_DIM_SWA))
            q = q.astype(BF16)
            o = jnp.zeros((w, LANES), F32)
            for hh, (kk, vv) in enumerate(kv):
                s = lax.dot_general(q, kk, (((1,), (1,)), ((), ())), preferred_element_type=F32)
                s = jnp.where(mask, s, NEG_INF)
                sink = sinks_ref[2 * qs + hh]
                m = jnp.maximum(jnp.max(s, axis=-1, keepdims=True), sink)
                e = jnp.exp(s - m)
                den = jnp.sum(e, axis=-1, keepdims=True) + jnp.exp(sink - m)
                o = o + jnp.dot(e.astype(BF16), vv, preferred_element_type=F32) / den
            o_ref[:, qcols] = o.astype(o_ref.dtype)


def _swa_attention(proj, sinks, cos_t, sin_t, bsz, seq, d):
    q_w, kv_w = d // 2, d // 16
    n_kv = kv_w // HEAD_DIM_SWA
    assert n_kv % 2 == 0 and q_w % kv_w == 0
    nb = seq // WINDOW
    k_col, v_col = q_w // kv_w, q_w // kv_w + 1
    cur = lambda b, n: b * nb + n
    prev = lambda b, n: b * nb + jnp.maximum(n - 1, 0)
    return pl.pallas_call(
        functools.partial(_swa_kernel, n_kv=n_kv),
        grid=(bsz, nb),
        in_specs=[pl.BlockSpec(memory_space=pltpu.SMEM),
                  pl.BlockSpec((WINDOW, q_w), lambda b, n: (cur(b, n), 0)),
                  pl.BlockSpec((WINDOW, kv_w), lambda b, n: (prev(b, n), k_col)),
                  pl.BlockSpec((WINDOW, kv_w), lambda b, n: (cur(b, n), k_col)),
                  pl.BlockSpec((WINDOW, kv_w), lambda b, n: (prev(b, n), v_col)),
                  pl.BlockSpec((WINDOW, kv_w), lambda b, n: (cur(b, n), v_col)),
                  pl.BlockSpec((WINDOW, LANES), lambda b, n: (n, 0)),
                  pl.BlockSpec((WINDOW, LANES), lambda b, n: (n, 0)),
                  pl.BlockSpec((WINDOW, LANES), lambda b, n: (jnp.maximum(n - 1, 0), 0)),
                  pl.BlockSpec((WINDOW, LANES), lambda b, n: (jnp.maximum(n - 1, 0), 0))],
        out_specs=pl.BlockSpec((WINDOW, q_w), lambda b, n: (cur(b, n), 0)),
        out_shape=jax.ShapeDtypeStruct((bsz * seq, q_w), BF16),
        compiler_params=_params(("parallel", "arbitrary"),
                                [((WINDOW, q_w), BF16)] * 2 + [((WINDOW, kv_w), BF16)] * 4,
                                [((2 * WINDOW, 2 * WINDOW), F32)] * 16),
        name="swa_attn",
    )(sinks, proj, proj, proj, proj, proj, cos_t, sin_t, cos_t, sin_t)


def _t5_thresholds():
    exact = REL_BUCKETS // 2
    d = np.arange(exact, 2 * REL_MAX_DIST, dtype=np.float64)
    large = exact + np.floor(np.log(d / exact) / math.log(REL_MAX_DIST / exact) * (REL_BUCKETS - exact)).astype(np.int64)
    large = np.minimum(large, REL_BUCKETS - 1)
    return [int(exact + np.argmax(large >= b)) for b in range(exact + 1, REL_BUCKETS)]


def _bias_tile_kernel(table_ref, o_ref):
    dd = pl.program_id(0)
    h = pl.program_id(1)
    exact = REL_BUCKETS // 2
    r = lax.broadcasted_iota(I32, (MOBA_BLOCK, MOBA_BLOCK), 0)
    c = lax.broadcasted_iota(I32, (MOBA_BLOCK, MOBA_BLOCK), 1)
    dist = jnp.maximum(dd * MOBA_BLOCK + r - c, 0)
    bucket = jnp.minimum(dist, exact)
    for thr in _t5_thresholds():
        bucket = bucket + (dist >= thr).astype(I32)
    val = jnp.full((MOBA_BLOCK, MOBA_BLOCK), table_ref[REL_BUCKETS - 1, h], F32)
    for b in range(REL_BUCKETS - 2, -1, -1):
        val = jnp.where(bucket == b, table_ref[b, h], val)
    o_ref[0, 0] = val


def _bias_tiles(rel_bias):
    n_heads = rel_bias.shape[1]
    return pl.pallas_call(
        _bias_tile_kernel,
        grid=(N_BIAS_TILES, n_heads),
        in_specs=[pl.BlockSpec(memory_space=pltpu.SMEM)],
        out_specs=pl.BlockSpec((1, 1, MOBA_BLOCK, MOBA_BLOCK), lambda dd, h: (h, dd, 0, 0)),
        out_shape=jax.ShapeDtypeStruct((n_heads, N_BIAS_TILES, MOBA_BLOCK, MOBA_BLOCK), F32),
        compiler_params=_params(("arbitrary", "arbitrary"), [((MOBA_BLOCK, MOBA_BLOCK), F32)],
                                [((MOBA_BLOCK, MOBA_BLOCK), F32)] * 4),
        name="moba_bias_tiles",
    )(rel_bias)


def _kmean_kernel(k_ref, o_ref, *, nblk):
    for j in range(nblk):
        rows = k_ref[j * MOBA_BLOCK:(j + 1) * MOBA_BLOCK, :].astype(F32)
        o_ref[j:j + 1, :] = jnp.mean(rows, axis=0, keepdims=True)


def _kmean(proj, bsz, seq, d):
    kv_w = d // 8
    nblk = seq // MOBA_BLOCK
    col = (9 * d // 8) // kv_w
    return pl.pallas_call(
        functools.partial(_kmean_kernel, nblk=nblk),
        grid=(bsz,),
        in_specs=[pl.BlockSpec((seq, kv_w), lambda b: (b, col))],
        out_specs=pl.BlockSpec((nblk, kv_w), lambda b: (b, 0)),
        out_shape=jax.ShapeDtypeStruct((bsz * nblk, kv_w), F32),
        compiler_params=_params(("parallel",), [((seq, kv_w), BF16)], [((MOBA_BLOCK, kv_w), F32)] * 2),
        name="moba_kmean",
    )(proj)


def _moba_kernel(q_ref, k_ref, v_ref, km_ref, bias_ref, o_ref, *, nblk):
    i = pl.program_id(2)
    mb, dh = MOBA_BLOCK, HEAD_DIM_MOBA
    scale = 1.0 / math.sqrt(dh)
    nt = (((1,), (1,)), ((), ()))
    blk = lax.broadcasted_iota(I32, (mb, nblk), 1)
    ri = lax.broadcasted_iota(I32, (mb, mb), 0)
    cj = lax.broadcasted_iota(I32, (mb, mb), 1)
    causal = ri >= cj
    kmean = km_ref[...]
    km_hi = kmean.astype(BF16)
    rest = kmean - km_hi.astype(F32)
    km_mid = rest.astype(BF16)
    km_lo = (rest - km_mid.astype(F32)).astype(BF16)
    own = pl.multiple_of(i * mb, mb)
    k_own, v_own = k_ref[pl.ds(own, mb), :], v_ref[pl.ds(own, mb), :]

    qs, picks, init = [], [], []
    for hh in range(MOBA_GROUP):
        q = q_ref[:, hh * dh:(hh + 1) * dh]
        gate = (lax.dot_general(q, km_hi, nt, preferred_element_type=F32)
                + lax.dot_general(q, km_mid, nt, preferred_element_type=F32)
                + lax.dot_general(q, km_lo, nt, preferred_element_type=F32))
        gate = jnp.where(blk < i, gate, NEG_INF)
        pick = []
        for t in range(MOBA_TOPK):
            gmax = jnp.max(gate, axis=-1, keepdims=True)
            idx = jnp.min(jnp.where(gate == gmax, blk, nblk), axis=-1, keepdims=True)
            pick.append(jnp.where(t < i, idx, -1))
            gate = jnp.where(blk == idx, NEG_INF, gate)

        s = lax.dot_general(q, k_own, nt, preferred_element_type=F32) * scale + bias_ref[hh, 0]
        s = jnp.where(causal, s, NEG_INF)
        m0 = jnp.max(s, axis=-1, keepdims=True)
        e = jnp.exp(s - m0)
        l0 = jnp.sum(e, axis=-1, keepdims=True)
        a0 = jnp.dot(e.astype(BF16), v_own, preferred_element_type=F32)
        qs.append(q)
        picks.append(pick)
        init.append((m0, l0, a0))

    def past(j, carry):
        start = pl.multiple_of(j * mb, mb)
        kj, vj = k_ref[pl.ds(start, mb), :], v_ref[pl.ds(start, mb), :]
        tile = jnp.minimum(i - j, N_BIAS_TILES - 1)
        out = []
        for hh in range(MOBA_GROUP):
            m, l, acc = carry[hh]
            s = lax.dot_general(qs[hh], kj, nt, preferred_element_type=F32) * scale + bias_ref[hh, tile]
            chosen = (picks[hh][0] == j) | (picks[hh][1] == j) | (picks[hh][2] == j)
            s = jnp.where(chosen, s, NEG_INF)
            m_new = jnp.maximum(m, jnp.max(s, axis=-1, keepdims=True))
            alpha = jnp.exp(m - m_new)
            e = jnp.exp(s - m_new)
            l = alpha * l + jnp.sum(e, axis=-1, keepdims=True)
            acc = alpha * acc + jnp.dot(e.astype(BF16), vj, preferred_element_type=F32)
            out.append((m_new, l, acc))
        return tuple(out)

    final = lax.fori_loop(0, i, past, tuple(init))
    for hh in range(MOBA_GROUP):
        _, l, acc = final[hh]
        o_ref[:, hh * dh:(hh + 1) * dh] = (acc / l).astype(o_ref.dtype)


def _moba_attention(proj, kmean, bias_tiles, bsz, seq, d):
    gw = MOBA_GROUP * HEAD_DIM_MOBA
    n_kv = (d // 8) // HEAD_DIM_MOBA
    nblk = seq // MOBA_BLOCK
    q_off, k_off, v_off = 5 * d // 8, 9 * d // 8, 10 * d // 8
    assert q_off % gw == 0 and MOBA_TOPK == 3
    qc, kc, vc = q_off // gw, k_off // HEAD_DIM_MOBA, v_off // HEAD_DIM_MOBA
    tile_shape = (MOBA_GROUP, N_BIAS_TILES, MOBA_BLOCK, MOBA_BLOCK)
    return pl.pallas_call(
        functools.partial(_moba_kernel, nblk=nblk),
        grid=(n_kv, bsz, nblk),
        in_specs=[pl.BlockSpec((MOBA_BLOCK, gw), lambda g, b, i: (b * nblk + i, qc + g)),
                  pl.BlockSpec((seq, HEAD_DIM_MOBA), lambda g, b, i: (b, kc + g)),
                  pl.BlockSpec((seq, HEAD_DIM_MOBA), lambda g, b, i: (b, vc + g)),
                  pl.BlockSpec((nblk, HEAD_DIM_MOBA), lambda g, b, i: (b, g)),
                  pl.BlockSpec(tile_shape, lambda g, b, i: (g, 0, 0, 0))],
        out_specs=pl.BlockSpec((MOBA_BLOCK, gw), lambda g, b, i: (b * nblk + i, g)),
        out_shape=jax.ShapeDtypeStruct((bsz * seq, d // 2), BF16),
        compiler_params=_params(("parallel", "parallel", "arbitrary"),
                                [(tile_shape, F32), ((seq, HEAD_DIM_MOBA), BF16), ((seq, HEAD_DIM_MOBA), BF16),
                                 ((MOBA_BLOCK, gw), BF16), ((MOBA_BLOCK, gw), BF16)],
                                [((MOBA_BLOCK, MOBA_BLOCK), F32)] * 8),
        name="moba_attn",
    )(proj, proj, proj, kmean, bias_tiles)


def _merge_kernel(oa_ref, ob_ref, wa_ref, wb_ref, ga_ref, gb_ref, o_ref):
    ya = jnp.dot(oa_ref[...], wa_ref[...], preferred_element_type=F32)
    yb = jnp.dot(ob_ref[...], wb_ref[...], preferred_element_type=F32)
    merged = jax.nn.sigmoid(ga_ref[...].astype(F32)) * ya + jax.nn.sigmoid(gb_ref[...].astype(F32)) * yb
    o_ref[...] = merged.astype(o_ref.dtype)


def _merge(o_a, o_b, w_a, w_b, proj, d):
    t, kdim = o_a.shape
    tm, tn = _tile(t, 1024), _tile(d, 512)
    ga_col, gb_col = (11 * d // 8) // tn, (19 * d // 8) // tn
    assert (11 * d // 8) % tn == 0 and (19 * d // 8) % tn == 0
    return pl.pallas_call(
        _merge_kernel,
        grid=(t // tm, d // tn),
        in_specs=[pl.BlockSpec((tm, kdim), lambda i, j: (i, 0)),
                  pl.BlockSpec((tm, kdim), lambda i, j: (i, 0)),
                  pl.BlockSpec((kdim, tn), lambda i, j: (0, j)),
                  pl.BlockSpec((kdim, tn), lambda i, j: (0, j)),
                  pl.BlockSpec((tm, tn), lambda i, j: (i, ga_col + j)),
                  pl.BlockSpec((tm, tn), lambda i, j: (i, gb_col + j))],
        out_specs=pl.BlockSpec((tm, tn), lambda i, j: (i, j)),
        out_shape=jax.ShapeDtypeStruct((t, d), BF16),
        compiler_params=_params(("parallel", "arbitrary"),
                                [((tm, kdim), BF16)] * 2 + [((kdim, tn), BF16)] * 2 + [((tm, tn), BF16)] * 3,
                                [((tm, tn), F32)] * 3),
        name="merge",
    )(o_a, o_b, w_a, w_b, proj, proj)


def _post_mix_kernel(y_ref, x_ref, gpost_ref, gpre_ref, mod_ref, wr_ref, br_ref,
                     x1_ref, h2_ref, idx_ref, wts_ref):
    m = mod_ref[0]
    x1 = x_ref[...] + m[2:3, :] * _rms(y_ref[...], gpost_ref[...])
    x1_ref[...] = x1
    h2 = _rms(x1, gpre_ref[...]) * (1.0 + m[4:5, :]) + m[3:4, :]
    half = h2.shape[1] // 2
    h2_ref[...] = _pack_bf16_pair(h2[:, :half], h2[:, half:])
    logits = jnp.dot(h2, wr_ref[...], preferred_element_type=F32,
                     precision=lax.Precision.HIGHEST) + br_ref[...]
    lane = lax.broadcasted_iota(I32, logits.shape, 1)
    logits = jnp.where(lane < N_EXPERTS, logits, NEG_INF)
    idx_out = jnp.zeros(logits.shape, I32)
    val_out = jnp.zeros(logits.shape, F32)
    top = None
    den = jnp.zeros((logits.shape[0], 1), F32)
    for k in range(TOP_K):
        vmax = jnp.max(logits, axis=-1, keepdims=True)
        idx = jnp.min(jnp.where(logits == vmax, lane, LANES), axis=-1, keepdims=True)
        top = vmax if top is None else top
        e = jnp.exp(vmax - top)
        den = den + e
        idx_out = jnp.where(lane == k, idx, idx_out)
        val_out = jnp.where(lane == k, e, val_out)
        logits = jnp.where(lane == idx, NEG_INF, logits)
    idx_ref[...] = idx_out
    wts_ref[...] = val_out / den


def _post_mix(y, x2, g_post, g_pre, mod, w_router, b_router, seq):
    t, d = x2.shape
    tm = _tile(seq, 256)
    wr = jnp.pad(w_router, ((0, 0), (0, LANES - N_EXPERTS)))
    br = jnp.pad(b_router, (0, LANES - N_EXPERTS)).reshape(1, LANES)
    row = lambda i: (i, 0)
    const = lambda i: (0, 0)
    return pl.pallas_call(
        _post_mix_kernel,
        grid=(t // tm,),
        in_specs=[pl.BlockSpec((tm, d), row), pl.BlockSpec((tm, d), row),
                  pl.BlockSpec((1, d), const), pl.BlockSpec((1, d), const),
                  pl.BlockSpec((1, 6, d), lambda i: (i * tm // seq, 0, 0)),
                  pl.BlockSpec((d, LANES), const), pl.BlockSpec((1, LANES), const)],
        out_specs=[pl.BlockSpec((tm, d), row), pl.BlockSpec((tm, d // 2), row),
                   pl.BlockSpec((tm, LANES), row), pl.BlockSpec((tm, LANES), row)],
        out_shape=[jax.ShapeDtypeStruct((t, d), F32), jax.ShapeDtypeStruct((t, d // 2), jnp.uint32),
                   jax.ShapeDtypeStruct((t, LANES), I32), jax.ShapeDtypeStruct((t, LANES), F32)],
        compiler_params=_params(("parallel",), [((tm, d), F32)] * 4 + [((d, LANES), F32)],
                                [((tm, d), F32)] * 3),
        name="post_mix_router",
    )(y, x2, g_post.reshape(1, d), g_pre.reshape(1, d), mod, wr, br)


def _routing(top_idx, n_blocks):
    flat_e = top_idx.reshape(-1)
    onehot = (flat_e[:, None] == jnp.arange(N_EXPERTS, dtype=I32)[None, :]).astype(I32)
    csum = jnp.cumsum(onehot, axis=0)
    counts = csum[-1]
    rank = jnp.sum(csum * onehot, axis=1) - 1
    padded = (counts + MOE_BLOCK - 1) // MOE_BLOCK * MOE_BLOCK
    pad_end = jnp.cumsum(padded)
    pad_start = pad_end - padded
    dest = jnp.sum(onehot * pad_start[None, :], axis=1) + rank
    n_real = (pad_end[-1] // MOE_BLOCK).astype(I32)
    blk_start = jnp.arange(n_blocks, dtype=I32) * MOE_BLOCK
    block_e = jnp.minimum(jnp.sum((pad_end[None, :] <= blk_start[:, None]).astype(I32), axis=1), N_EXPERTS - 1)
    last_blk = jnp.where(counts > 0, pad_end // MOE_BLOCK - 1, -1).astype(I32)
    return dest.astype(I32), block_e.astype(I32), n_real.reshape(1), last_blk


def _pack_bf16_pair(lo, hi):
    lo_bits = lax.bitcast_convert_type(lo.astype(BF16).astype(F32), jnp.uint32) >> 16
    hi_bits = lax.bitcast_convert_type(hi.astype(BF16).astype(F32), jnp.uint32) & jnp.uint32(0xFFFF0000)
    return hi_bits | lo_bits


def _unpack_bf16_pair(p):
    lo = lax.bitcast_convert_type(p << 16, F32).astype(BF16)
    hi = lax.bitcast_convert_type(p & jnp.uint32(0xFFFF0000), F32).astype(BF16)
    return lo, hi


def _dispatch_kernel(last_ref, nr_ref, dest_ref, h_ref, xg_hbm, zeros, zsem, sem, *, tokens, n_blocks):
    def zero_block(m):
        rows = pl.ds(pl.multiple_of(m * MOE_BLOCK, MOE_BLOCK), MOE_BLOCK)
        return pltpu.make_async_copy(zeros, xg_hbm.at[rows], zsem)

    @pl.when(pl.program_id(0) == 0)
    def _():
        zeros[...] = jnp.zeros(zeros.shape, zeros.dtype)
        for phase in ("start", "wait"):
            for e in range(N_EXPERTS):
                @pl.when(last_ref[e] >= 0)
                def _(e=e, phase=phase):
                    getattr(zero_block(last_ref[e]), phase)()

            def tail(m, carry, phase=phase):
                getattr(zero_block(m), phase)()
                return carry

            lax.fori_loop(nr_ref[0], n_blocks, tail, 0)

    def row_copy(t, row):
        return pltpu.make_async_copy(h_ref.at[pl.ds(t, 1)], xg_hbm.at[pl.ds(row, 1)], sem)

    def issue(t, carry):
        for k in range(TOP_K):
            row_copy(t, dest_ref[0, 0, t * TOP_K + k]).start()
        return carry

    lax.fori_loop(0, tokens, issue, 0)

    def drain(t, carry):
        for k in range(TOP_K):
            row_copy(0, 0).wait()
        return carry

    lax.fori_loop(0, tokens, drain, 0)


def _dispatch(h2p, dest, last_blk, n_real, n_blocks):
    t, dp = h2p.shape
    tokens = _tile(t, 256)
    steps = t // tokens
    return pl.pallas_call(
        functools.partial(_dispatch_kernel, tokens=tokens, n_blocks=n_blocks),
        grid_spec=pltpu.PrefetchScalarGridSpec(
            num_scalar_prefetch=2, grid=(steps,),
            in_specs=[pl.BlockSpec((1, 1, tokens * TOP_K), lambda i, last, nr: (i, 0, 0), memory_space=pltpu.SMEM),
                      pl.BlockSpec((tokens, dp), lambda i, last, nr: (i, 0))],
            out_specs=pl.BlockSpec(memory_space=pl.ANY),
            scratch_shapes=[pltpu.VMEM((MOE_BLOCK, dp), h2p.dtype),
                            pltpu.SemaphoreType.DMA(()), pltpu.SemaphoreType.DMA(())]),
        out_shape=jax.ShapeDtypeStruct((n_blocks * MOE_BLOCK, dp), h2p.dtype),
        compiler_params=_params(("arbitrary",), [((tokens, dp), h2p.dtype)], [((MOE_BLOCK, dp), h2p.dtype)]),
        name="moe_dispatch",
    )(last_blk, n_real, dest.reshape(steps, 1, tokens * TOP_K), h2p)


def _deinterleave_kernel(w_ref, p_ref, wg_ref, wl_ref):
    perm = p_ref[...]
    group = perm.shape[0]
    half = group // 2
    for c in range(w_ref.shape[2] // group):
        t = jnp.dot(w_ref[0, :, c * group:(c + 1) * group].astype(BF16), perm, preferred_element_type=F32)
        wg_ref[0, :, c * half:(c + 1) * half] = t[:, :half].astype(BF16)
        wl_ref[0, :, c * half:(c + 1) * half] = t[:, half:].astype(BF16)


def _deinterleave(w_gu):
    e, d, f2 = w_gu.shape
    group = 2 * LANES
    perm = np.zeros((group, group), np.float32)
    perm[2 * np.arange(LANES), np.arange(LANES)] = 1.0
    perm[2 * np.arange(LANES) + 1, LANES + np.arange(LANES)] = 1.0
    tk = _tile(d, 512)
    out = jax.ShapeDtypeStruct((e, d, f2 // 2), BF16)
    return pl.pallas_call(
        _deinterleave_kernel,
        grid=(e, d // tk),
        in_specs=[pl.BlockSpec((1, tk, f2), lambda i, j: (i, j, 0)),
                  pl.BlockSpec((group, group), lambda i, j: (0, 0))],
        out_specs=[pl.BlockSpec((1, tk, f2 // 2), lambda i, j: (i, j, 0))] * 2,
        out_shape=[out, out],
        compiler_params=_params(("parallel", "arbitrary"), [((tk, f2), F32), ((tk, f2), BF16)],
                                [((tk, group), F32)] * 4),
        name="moe_deinterleave",
    )(w_gu, jnp.asarray(perm, BF16))


def _gate_up_kernel(be_ref, nr_ref, x_ref, wg_ref, wl_ref, bg_ref, bl_ref, o_ref):
    real = pl.program_id(1) < nr_ref[0]

    @pl.when(real)
    def _():
        lo, hi = _unpack_bf16_pair(x_ref[...])
        half = lo.shape[1]

        def proj(w_ref, b_ref):
            return (jnp.dot(lo, w_ref[0, :half, :], preferred_element_type=F32)
                    + jnp.dot(hi, w_ref[0, half:, :], preferred_element_type=F32) + b_ref[0])

        glu = jnp.minimum(proj(wg_ref, bg_ref), SWIGLU_LIMIT)
        lin = jnp.clip(proj(wl_ref, bl_ref), -SWIGLU_LIMIT, SWIGLU_LIMIT)
        o_ref[...] = (glu * jax.nn.sigmoid(SWIGLU_ALPHA * glu) * (lin + 1.0)).astype(o_ref.dtype)

    @pl.when(jnp.logical_not(real))
    def _():
        o_ref[...] = jnp.zeros(o_ref.shape, o_ref.dtype)


def _gate_up(xg, w_g, w_l, b_g, b_l, block_e, n_real, n_blocks):
    rows, dp = xg.shape
    d, f = w_g.shape[1], w_g.shape[2]
    passes = 2
    tf = f // passes
    blk = lambda p, m, be, nr: jnp.minimum(m, nr[0] - 1)
    wmap = lambda p, m, be, nr: (be[blk(p, m, be, nr)], 0, p)
    return pl.pallas_call(
        _gate_up_kernel,
        grid_spec=pltpu.PrefetchScalarGridSpec(
            num_scalar_prefetch=2, grid=(passes, n_blocks),
            in_specs=[pl.BlockSpec((MOE_BLOCK, dp), lambda p, m, be, nr: (blk(p, m, be, nr), 0)),
                      pl.BlockSpec((1, d, tf), wmap), pl.BlockSpec((1, d, tf), wmap),
                      pl.BlockSpec((1, 1, tf), wmap), pl.BlockSpec((1, 1, tf), wmap)],
            out_specs=pl.BlockSpec((MOE_BLOCK, tf), lambda p, m, be, nr: (m, p))),
        out_shape=jax.ShapeDtypeStruct((rows, f), BF16),
        compiler_params=_params(("arbitrary", "arbitrary"),
                                [((MOE_BLOCK, dp), xg.dtype), ((d, tf), BF16), ((d, tf), BF16),
                                 ((MOE_BLOCK, tf), BF16)],
                                [((MOE_BLOCK, d), BF16)] + [((MOE_BLOCK, tf), F32)] * 4),
        name="moe_gate_up",
    )(block_e, n_real, xg, w_g, w_l, b_g, b_l)


def _down_kernel(be_ref, nr_ref, a_ref, w_ref, b_ref, o_ref):
    real = pl.program_id(0) < nr_ref[0]

    @pl.when(real)
    def _():
        o_ref[...] = jnp.dot(a_ref[...], w_ref[0], preferred_element_type=F32) + b_ref[0]

    @pl.when(jnp.logical_not(real))
    def _():
        o_ref[...] = jnp.zeros(o_ref.shape, o_ref.dtype)


def _down(act, w_d, b_d, block_e, n_real, n_blocks):
    rows, f = act.shape
    d = w_d.shape[2]
    blk = lambda m, be, nr: jnp.minimum(m, nr[0] - 1)
    wmap = lambda m, be, nr: (be[blk(m, be, nr)], 0, 0)
    return pl.pallas_call(
        _down_kernel,
        grid_spec=pltpu.PrefetchScalarGridSpec(
            num_scalar_prefetch=2, grid=(n_blocks,),
            in_specs=[pl.BlockSpec((MOE_BLOCK, f), lambda m, be, nr: (m, 0)),
                      pl.BlockSpec((1, f, d), wmap), pl.BlockSpec((1, 1, d), wmap)],
            out_specs=pl.BlockSpec((MOE_BLOCK, d), lambda m, be, nr: (m, 0))),
        out_shape=jax.ShapeDtypeStruct((rows, d), F32),
        compiler_params=_params(("arbitrary",),
                                [((MOE_BLOCK, f), BF16), ((f, d), BF16), ((MOE_BLOCK, d), F32)],
                                [((MOE_BLOCK, d), F32)]),
        name="moe_down",
    )(block_e, n_real, act, w_d, b_d)


def _combine_kernel(dest_ref, w_ref, x1_ref, g_ref, mod_ref, y_hbm, o_ref, buf, sem, *, tokens):
    def issue(t, carry):
        for k in range(TOP_K):
            row = dest_ref[0, 0, t * TOP_K + k]
            pltpu.make_async_copy(y_hbm.at[pl.ds(row, 1)], buf.at[k, pl.ds(t, 1)], sem).start()
        return carry

    lax.fori_loop(0, tokens, issue, 0)

    def drain(t, carry):
        for k in range(TOP_K):
            pltpu.make_async_copy(y_hbm.at[pl.ds(0, 1)], buf.at[k, pl.ds(0, 1)], sem).wait()
        return carry

    lax.fori_loop(0, tokens, drain, 0)

    w = w_ref[...]
    acc = buf[0] * w[:, 0:1]
    for k in range(1, TOP_K):
        acc = acc + buf[k] * w[:, k:k + 1]
    m = mod_ref[0]
    o_ref[...] = x1_ref[...] + m[5:6, :] * _rms(acc, g_ref[...])


def _combine(y_sorted, dest, top_w, x1, g_post, mod, seq):
    t, d = x1.shape
    tokens = _tile(seq, 128)
    steps = t // tokens
    row = lambda i: (i, 0)
    return pl.pallas_call(
        functools.partial(_combine_kernel, tokens=tokens),
        grid=(steps,),
        in_specs=[pl.BlockSpec((1, 1, tokens * TOP_K), lambda i: (i, 0, 0), memory_space=pltpu.SMEM),
                  pl.BlockSpec((tokens, LANES), row),
                  pl.BlockSpec((tokens, d), row),
                  pl.BlockSpec((1, d), lambda i: (0, 0)),
                  pl.BlockSpec((1, 6, d), lambda i: (i * tokens // seq, 0, 0)),
                  pl.BlockSpec(memory_space=pl.ANY)],
        out_specs=pl.BlockSpec((tokens, d), row),
        out_shape=jax.ShapeDtypeStruct((t, d), F32),
        scratch_shapes=[pltpu.VMEM((TOP_K, tokens, d), F32), pltpu.SemaphoreType.DMA(())],
        compiler_params=_params(("arbitrary",), [((tokens, d), F32)] * 2,
                                [((TOP_K, tokens, d), F32)] + [((tokens, d), F32)] * 2),
        name="moe_combine",
    )(dest.reshape(steps, 1, tokens * TOP_K), top_w, x1, g_post.reshape(1, d), mod, y_sorted)


def _moe(h2, top_idx, top_w, x1, g_post, mod, w_gu, b_gu, w_dn, b_dn, seq):
    t, d = x1.shape
    f = w_dn.shape[1]
    n_blocks = -(-(t * TOP_K) // MOE_BLOCK) + N_EXPERTS
    dest, block_e, n_real, last_blk = _routing(top_idx, n_blocks)
    xg = _dispatch(h2, dest, last_blk, n_real, n_blocks)
    w_g, w_l = _deinterleave(w_gu)
    b_g = b_gu[:, 0::2].reshape(N_EXPERTS, 1, f)
    b_l = b_gu[:, 1::2].reshape(N_EXPERTS, 1, f)
    act = _gate_up(xg, w_g, w_l, b_g, b_l, block_e, n_real, n_blocks)
    y_sorted = _down(act, w_dn.astype(BF16), b_dn.reshape(N_EXPERTS, 1, d), block_e, n_real, n_blocks)
    return _combine(y_sorted, dest, top_w, x1, g_post, mod, seq)


def kernel(x, c, w_ada, b_ada, g_pre_mix, g_post_mix, g_pre_ffn, g_post_ffn, w_in, b_in, sinks, rel_bias,
           w_o_swa, w_o_moba, w_out, w_router, b_router, w_gate_up, b_gate_up, w_down, b_down):
    bsz, seq, d = x.shape
    depth = w_ada.shape[0]
    assert seq % MOBA_BLOCK == 0 and d % 1024 == 0
    x2 = x.reshape(bsz * seq, d)
    cos_t, sin_t = _rope_tables(seq)
    bias_tiles = _bias_tiles(rel_bias)
    for l in range(depth):
        mod = _ada_mod(c, w_ada[l], b_ada[l])
        h = _prenorm(x2, g_pre_mix[l], mod, seq, shift_row=0, scale_row=1)
        proj = _matmul(h, w_in[l].astype(BF16), b_in[l], BF16, 1024, 768, "in_proj")
        o_a = _swa_attention(proj, sinks[l], cos_t, sin_t, bsz, seq, d)
        o_b = _moba_attention(proj, _kmean(proj, bsz, seq, d), bias_tiles, bsz, seq, d)
        merged = _merge(o_a, o_b, w_o_swa[l].astype(BF16), w_o_moba[l].astype(BF16), proj, d)
        y = _matmul(merged, w_out[l].astype(BF16), jnp.zeros((d,), F32), F32, 1024, 1024, "out_proj")
        x1, h2, idx, wts = _post_mix(y, x2, g_post_mix[l], g_pre_ffn[l], mod, w_router[l], b_router[l], seq)
        x2 = _moe(h2, idx[:, :TOP_K], wts, x1, g_post_ffn[l], mod,
                  w_gate_up[l], b_gate_up[l], w_down[l], b_down[l], seq)
    return x2.reshape(bsz, seq, d)
```

```python
import functools
import math

import numpy as np
import jax
import jax.numpy as jnp
from jax import lax
from jax.experimental import pallas as pl
from jax.experimental.pallas import tpu as pltpu

F32 = jnp.float32
BF16 = jnp.bfloat16
I32 = jnp.int32

HEAD_DIM_SWA = 64
SWA_GROUP = 8
WINDOW = 128
ROPE_THETA = 150000.0
HEAD_DIM_MOBA = 128
MOBA_GROUP = 4
MOBA_BLOCK = 256
MOBA_TOPK = 3
REL_BUCKETS = 32
REL_MAX_DIST = 1024
N_EXPERTS = 32
TOP_K = 4
SWIGLU_LIMIT = 7.0
SWIGLU_ALPHA = 1.702
MOE_BLOCK = 256
RMS_EPS = 1e-6

LANES = 128
SUBLANES = 8
VMEM_BYTES_V7X = 64 * 1024 * 1024
VMEM_CAP = VMEM_BYTES_V7X - 8 * 1024 * 1024

NEG_INF = float("-inf")
N_BIAS_TILES = (REL_MAX_DIST + MOBA_BLOCK - 1) // MOBA_BLOCK + 2


def _nbytes(shape, dtype):
    return int(np.prod(shape)) * jnp.dtype(dtype).itemsize


def _params(semantics, blocks, temps=()):
    need = 2 * sum(_nbytes(s, d) for s, d in blocks) + sum(_nbytes(s, d) for s, d in temps)
    need = max(need + need // 4, 16 * 1024 * 1024)
    return pltpu.CompilerParams(dimension_semantics=semantics, vmem_limit_bytes=min(need, VMEM_CAP))


def _tile(n, pref):
    t = min(n, pref)
    assert n % t == 0, (n, pref)
    return t


def _ada_kernel(c_ref, w_ref, b_ref, o_ref):
    c = c_ref[...]
    s = c * jax.nn.sigmoid(c)
    o_ref[...] = jnp.dot(s, w_ref[...], preferred_element_type=F32,
                         precision=lax.Precision.HIGHEST) + b_ref[...]


def _ada_mod(c, w, b):
    bsz, d = c.shape
    n = w.shape[1]
    rows = -(-bsz // SUBLANES) * SUBLANES
    cp = jnp.pad(c, ((0, rows - bsz), (0, 0)))
    tn = _tile(n, 512)
    out = pl.pallas_call(
        _ada_kernel,
        grid=(n // tn,),
        in_specs=[pl.BlockSpec((rows, d), lambda j: (0, 0)),
                  pl.BlockSpec((d, tn), lambda j: (0, j)),
                  pl.BlockSpec((1, tn), lambda j: (0, j))],
        out_specs=pl.BlockSpec((rows, tn), lambda j: (0, j)),
        out_shape=jax.ShapeDtypeStruct((rows, n), F32),
        compiler_params=_params(("arbitrary",), [((d, tn), F32), ((rows, d), F32)], [((d, tn), F32)]),
        name="ada_mod",
    )(cp, w, b.reshape(1, n))
    return out[:bsz].reshape(bsz, 6, d)


def _rms(x, g):
    return x * lax.rsqrt(jnp.mean(x * x, axis=-1, keepdims=True) + RMS_EPS) * g


def _prenorm_kernel(x_ref, g_ref, mod_ref, o_ref, *, shift_row, scale_row):
    m = mod_ref[0]
    y = _rms(x_ref[...], g_ref[...])
    o_ref[...] = (y * (1.0 + m[scale_row:scale_row + 1, :]) + m[shift_row:shift_row + 1, :]).astype(o_ref.dtype)


def _prenorm(x2, g, mod, seq, shift_row, scale_row):
    t, d = x2.shape
    tm = _tile(seq, 256)
    return pl.pallas_call(
        functools.partial(_prenorm_kernel, shift_row=shift_row, scale_row=scale_row),
        grid=(t // tm,),
        in_specs=[pl.BlockSpec((tm, d), lambda i: (i, 0)),
                  pl.BlockSpec((1, d), lambda i: (0, 0)),
                  pl.BlockSpec((1, 6, d), lambda i: (i * tm // seq, 0, 0))],
        out_specs=pl.BlockSpec((tm, d), lambda i: (i, 0)),
        out_shape=jax.ShapeDtypeStruct((t, d), BF16),
        compiler_params=_params(("parallel",), [((tm, d), F32), ((tm, d), BF16)], [((tm, d), F32)] * 2),
        name="prenorm",
    )(x2, g.reshape(1, d), mod)


def _mm_kernel(a_ref, w_ref, b_ref, o_ref):
    acc = jnp.dot(a_ref[...], w_ref[...], preferred_element_type=F32)
    o_ref[...] = (acc + b_ref[...]).astype(o_ref.dtype)


def _matmul(a, w, bias, out_dtype, tm, tn, name):
    m, k = a.shape
    n = w.shape[1]
    tm, tn = _tile(m, tm), _tile(n, tn)
    return pl.pallas_call(
        _mm_kernel,
        grid=(m // tm, n // tn),
        in_specs=[pl.BlockSpec((tm, k), lambda i, j: (i, 0)),
                  pl.BlockSpec((k, tn), lambda i, j: (0, j)),
                  pl.BlockSpec((1, tn), lambda i, j: (0, j))],
        out_specs=pl.BlockSpec((tm, tn), lambda i, j: (i, j)),
        out_shape=jax.ShapeDtypeStruct((m, n), out_dtype),
        compiler_params=_params(("parallel", "arbitrary"),
                                [((tm, k), a.dtype), ((k, tn), w.dtype), ((tm, tn), out_dtype)],
                                [((tm, tn), F32)]),
        name=name,
    )(a, w, bias.reshape(1, n).astype(F32))


def _rope_tables(seq):
    half = HEAD_DIM_SWA // 2
    inv = ROPE_THETA ** (-jnp.arange(half, dtype=F32) / half)
    ang = jnp.arange(seq, dtype=F32)[:, None] * inv[None, :]
    cos, sin = jnp.cos(ang), jnp.sin(ang)
    cos_t = jnp.concatenate([cos, cos, cos, cos], axis=-1)
    sin_t = jnp.concatenate([-sin, sin, -sin, sin], axis=-1)
    return cos_t, sin_t


def _swa_kernel(sinks_ref, q_ref, kp_ref, kc_ref, vp_ref, vc_ref, cq_ref, sq_ref, cp_ref, sp_ref, o_ref,
                *, n_kv):
    n = pl.program_id(1)
    w = WINDOW
    half = HEAD_DIM_SWA // 2
    lane = lax.broadcasted_iota(I32, (1, LANES), 1)
    first_half = (lane % HEAD_DIM_SWA) < half
    low_head = lane < HEAD_DIM_SWA

    def rope(xf, cos, sin):
        partner = jnp.where(first_half, pltpu.roll(xf, LANES - half, 1), pltpu.roll(xf, half, 1))
        return xf * cos + partner * sin

    cos_q, sin_q = cq_ref[...], sq_ref[...]
    cos_k = jnp.concatenate([cp_ref[...], cos_q], axis=0)
    sin_k = jnp.concatenate([sp_ref[...], sin_q], axis=0)
    ri = lax.broadcasted_iota(I32, (w, 2 * w), 0)
    cj = lax.broadcasted_iota(I32, (w, 2 * w), 1)
    diff = w + ri - cj
    mask = (diff >= 0) & (diff < w) & ((cj >= w) | (n > 0))
    pairs = SWA_GROUP // 2

    for g in range(n_kv):
        slab = g // 2
        cols = slice(slab * LANES, (slab + 1) * LANES)
        k2 = rope(jnp.concatenate([kp_ref[:, cols], kc_ref[:, cols]], axis=0).astype(F32), cos_k, sin_k)
        v2 = jnp.concatenate([vp_ref[:, cols], vc_ref[:, cols]], axis=0).astype(F32)
        k2s, v2s = pltpu.roll(k2, HEAD_DIM_SWA, 1), pltpu.roll(v2, HEAD_DIM_SWA, 1)
        if g % 2 == 0:
            k_lo, v_lo = jnp.where(low_head, k2, 0.0), jnp.where(low_head, v2, 0.0)
            k_hi, v_hi = jnp.where(low_head, 0.0, k2s), jnp.where(low_head, 0.0, v2s)
        else:
            k_lo, v_lo = jnp.where(low_head, k2s, 0.0), jnp.where(low_head, v2s, 0.0)
            k_hi, v_hi = jnp.where(low_head, 0.0, k2), jnp.where(low_head, 0.0, v2)
        kv = ((k_lo.astype(BF16), v_lo.astype(BF16)), (k_hi.astype(BF16), v_hi.astype(BF16)))
        for p in range(pairs):
            qs = g * pairs + p
            qcols = slice(qs * LANES, (qs + 1) * LANES)
            q = rope(q_ref[:, qcols].astype(F32), cos_q, sin_q) * (1.0 / math.sqrt(HEAD_DIM_SWA))
            q = q.astype(BF16)
            o = jnp.zeros((w, LANES), F32)
            for hh, (kk, vv) in enumerate(kv):
                s = lax.dot_general(q, kk, (((1,), (1,)), ((), ())), preferred_element_type=F32)
                s = jnp.where(mask, s, NEG_INF)
                sink = sinks_ref[2 * qs + hh]
                m = jnp.maximum(jnp.max(s, axis=-1, keepdims=True), sink)
                e = jnp.exp(s - m)
                den = jnp.sum(e, axis=-1, keepdims=True) + jnp.exp(sink - m)
                o = o + jnp.dot(e.astype(BF16), vv, preferred_element_type=F32) / den
            o_ref[:, qcols] = o.astype(o_ref.dtype)


def _swa_attention(proj, sinks, cos_t, sin_t, bsz, seq, d):
    q_w, kv_w = d // 2, d // 16
    n_kv = kv_w // HEAD_DIM_SWA
    assert n_kv % 2 == 0 and q_w % kv_w == 0
    nb = seq // WINDOW
    k_col, v_col = q_w // kv_w, q_w // kv_w + 1
    cur = lambda b, n: b * nb + n
    prev = lambda b, n: b * nb + jnp.maximum(n - 1, 0)
    return pl.pallas_call(
        functools.partial(_swa_kernel, n_kv=n_kv),
        grid=(bsz, nb),
        in_specs=[pl.BlockSpec(memory_space=pltpu.SMEM),
                  pl.BlockSpec((WINDOW, q_w), lambda b, n: (cur(b, n), 0)),
                  pl.BlockSpec((WINDOW, kv_w), lambda b, n: (prev(b, n), k_col)),
                  pl.BlockSpec((WINDOW, kv_w), lambda b, n: (cur(b, n), k_col)),
                  pl.BlockSpec((WINDOW, kv_w), lambda b, n: (prev(b, n), v_col)),
                  pl.BlockSpec((WINDOW, kv_w), lambda b, n: (cur(b, n), v_col)),
                  pl.BlockSpec((WINDOW, LANES), lambda b, n: (n, 0)),
                  pl.BlockSpec((WINDOW, LANES), lambda b, n: (n, 0)),
                  pl.BlockSpec((WINDOW, LANES), lambda b, n: (jnp.maximum(n - 1, 0), 0)),
                  pl.BlockSpec((WINDOW, LANES), lambda b, n: (jnp.maximum(n - 1, 0), 0))],
        out_specs=pl.BlockSpec((WINDOW, q_w), lambda b, n: (cur(b, n), 0)),
        out_shape=jax.ShapeDtypeStruct((bsz * seq, q_w), BF16),
        compiler_params=_params(("parallel", "arbitrary"),
                                [((WINDOW, q_w), BF16)] * 2 + [((WINDOW, kv_w), BF16)] * 4,
                                [((2 * WINDOW, 2 * WINDOW), F32)] * 16),
        name="swa_attn",
    )(sinks, proj, proj, proj, proj, proj, cos_t, sin_t, cos_t, sin_t)


def _t5_thresholds():
    exact = REL_BUCKETS // 2
    d = np.arange(exact, 2 * REL_MAX_DIST, dtype=np.float64)
    large = exact + np.floor(np.log(d / exact) / math.log(REL_MAX_DIST / exact) * (REL_BUCKETS - exact)).astype(np.int64)
    large = np.minimum(large, REL_BUCKETS - 1)
    return [int(exact + np.argmax(large >= b)) for b in range(exact + 1, REL_BUCKETS)]


def _bias_tile_kernel(table_ref, o_ref):
    dd = pl.program_id(0)
    h = pl.program_id(1)
    exact = REL_BUCKETS // 2
    key = lax.broadcasted_iota(I32, (MOBA_BLOCK, MOBA_BLOCK), 0)
    qry = lax.broadcasted_iota(I32, (MOBA_BLOCK, MOBA_BLOCK), 1)
    dist = jnp.maximum(dd * MOBA_BLOCK + qry - key, 0)
    bucket = jnp.minimum(dist, exact)
    for thr in _t5_thresholds():
        bucket = bucket + (dist >= thr).astype(I32)
    val = jnp.full((MOBA_BLOCK, MOBA_BLOCK), table_ref[REL_BUCKETS - 1, h], F32)
    for b in range(REL_BUCKETS - 2, -1, -1):
        val = jnp.where(bucket == b, table_ref[b, h], val)
    o_ref[0, 0] = val


def _bias_tiles(rel_bias):
    n_heads = rel_bias.shape[1]
    return pl.pallas_call(
        _bias_tile_kernel,
        grid=(N_BIAS_TILES, n_heads),
        in_specs=[pl.BlockSpec(memory_space=pltpu.SMEM)],
        out_specs=pl.BlockSpec((1, 1, MOBA_BLOCK, MOBA_BLOCK), lambda dd, h: (h, dd, 0, 0)),
        out_shape=jax.ShapeDtypeStruct((n_heads, N_BIAS_TILES, MOBA_BLOCK, MOBA_BLOCK), F32),
        compiler_params=_params(("arbitrary", "arbitrary"), [((MOBA_BLOCK, MOBA_BLOCK), F32)],
                                [((MOBA_BLOCK, MOBA_BLOCK), F32)] * 4),
        name="moba_bias_tiles",
    )(rel_bias)


def _kmean_kernel(k_ref, o_ref, *, nblk):
    for j in range(nblk):
        rows = k_ref[j * MOBA_BLOCK:(j + 1) * MOBA_BLOCK, :].astype(F32)
        o_ref[j:j + 1, :] = jnp.mean(rows, axis=0, keepdims=True)


def _kmean(proj, bsz, seq, d):
    kv_w = d // 8
    nblk = seq // MOBA_BLOCK
    col = (9 * d // 8) // kv_w
    return pl.pallas_call(
        functools.partial(_kmean_kernel, nblk=nblk),
        grid=(bsz,),
        in_specs=[pl.BlockSpec((seq, kv_w), lambda b: (b, col))],
        out_specs=pl.BlockSpec((nblk, kv_w), lambda b: (b, 0)),
        out_shape=jax.ShapeDtypeStruct((bsz * nblk, kv_w), F32),
        compiler_params=_params(("parallel",), [((seq, kv_w), BF16)], [((MOBA_BLOCK, kv_w), F32)] * 2),
        name="moba_kmean",
    )(proj)


def _moba_kernel(q_ref, k_ref, v_ref, km_ref, bias_ref, o_ref, vt_ref, *, nblk):
    i = pl.program_id(2)
    mb, dh = MOBA_BLOCK, HEAD_DIM_MOBA
    scale = 1.0 / math.sqrt(dh)
    nt = (((1,), (1,)), ((), ()))
    blk = lax.broadcasted_iota(I32, (nblk, mb), 0)
    key = lax.broadcasted_iota(I32, (mb, mb), 0)
    qry = lax.broadcasted_iota(I32, (mb, mb), 1)
    causal = qry >= key

    @pl.when(i == 0)
    def _():
        for j in range(nblk):
            vt_ref[:, j * mb:(j + 1) * mb] = v_ref[j * mb:(j + 1) * mb, :].astype(F32).T.astype(BF16)

    kmean = km_ref[...]
    km_hi = kmean.astype(BF16)
    rest = kmean - km_hi.astype(F32)
    km_mid = rest.astype(BF16)
    km_lo = (rest - km_mid.astype(F32)).astype(BF16)
    own = pl.multiple_of(i * mb, mb)
    k_own, vt_own = k_ref[pl.ds(own, mb), :], vt_ref[:, pl.ds(own, mb)]

    qs = [q_ref[:, hh * dh:(hh + 1) * dh] for hh in range(MOBA_GROUP)]
    gates = [lax.dot_general(km_hi, q, nt, preferred_element_type=F32)
             + lax.dot_general(km_mid, q, nt, preferred_element_type=F32)
             + lax.dot_general(km_lo, q, nt, preferred_element_type=F32) for q in qs]
    own_scores = [lax.dot_general(k_own, q, nt, preferred_element_type=F32) for q in qs]
    picks, init = [], []
    for hh in range(MOBA_GROUP):
        gate = jnp.where(blk < i, gates[hh], NEG_INF)
        pick = []
        for t in range(MOBA_TOPK):
            gmax = jnp.max(gate, axis=0, keepdims=True)
            idx = jnp.min(jnp.where(gate == gmax, blk, nblk), axis=0, keepdims=True)
            pick.append(jnp.where(t < i, idx, -1))
            gate = jnp.where(blk == idx, NEG_INF, gate)

        s = jnp.where(causal, own_scores[hh] * scale + bias_ref[hh, 0], NEG_INF)
        m0 = jnp.max(s, axis=0, keepdims=True)
        e = jnp.exp(s - m0)
        l0 = jnp.sum(e, axis=0, keepdims=True)
        a0 = jnp.dot(vt_own, e.astype(BF16), preferred_element_type=F32)
        picks.append(pick)
        init.append((m0, l0, a0))

    def past(j, carry):
        start = pl.multiple_of(j * mb, mb)
        kj, vtj = k_ref[pl.ds(start, mb), :], vt_ref[:, pl.ds(start, mb)]
        tile = jnp.minimum(i - j, N_BIAS_TILES - 1)
        scores = [lax.dot_general(kj, qs[hh], nt, preferred_element_type=F32) for hh in range(MOBA_GROUP)]
        out = []
        for hh in range(MOBA_GROUP):
            m, l, acc = carry[hh]
            s = scores[hh] * scale + bias_ref[hh, tile]
            chosen = (picks[hh][0] == j) | (picks[hh][1] == j) | (picks[hh][2] == j)
            s = jnp.where(chosen, s, NEG_INF)
            m_new = jnp.maximum(m, jnp.max(s, axis=0, keepdims=True))
            alpha = jnp.exp(m - m_new)
            e = jnp.exp(s - m_new)
            l = alpha * l + jnp.sum(e, axis=0, keepdims=True)
            acc = alpha * acc + jnp.dot(vtj, e.astype(BF16), preferred_element_type=F32)
            out.append((m_new, l, acc))
        return tuple(out)

    final = lax.fori_loop(0, i, past, tuple(init))
    for hh in range(MOBA_GROUP):
        _, l, acc = final[hh]
        o_ref[:, hh * dh:(hh + 1) * dh] = (acc / l).T.astype(o_ref.dtype)


def _moba_attention(proj, kmean, bias_tiles, bsz, seq, d):
    gw = MOBA_GROUP * HEAD_DIM_MOBA
    n_kv = (d // 8) // HEAD_DIM_MOBA
    nblk = seq // MOBA_BLOCK
    q_off, k_off, v_off = 5 * d // 8, 9 * d // 8, 10 * d // 8
    assert q_off % gw == 0 and MOBA_TOPK == 3
    qc, kc, vc = q_off // gw, k_off // HEAD_DIM_MOBA, v_off // HEAD_DIM_MOBA
    tile_shape = (MOBA_GROUP, N_BIAS_TILES, MOBA_BLOCK, MOBA_BLOCK)
    return pl.pallas_call(
        functools.partial(_moba_kernel, nblk=nblk),
        grid=(n_kv, bsz, nblk),
        in_specs=[pl.BlockSpec((MOBA_BLOCK, gw), lambda g, b, i: (b * nblk + i, qc + g)),
                  pl.BlockSpec((seq, HEAD_DIM_MOBA), lambda g, b, i: (b, kc + g)),
                  pl.BlockSpec((seq, HEAD_DIM_MOBA), lambda g, b, i: (b, vc + g)),
                  pl.BlockSpec((nblk, HEAD_DIM_MOBA), lambda g, b, i: (b, g)),
                  pl.BlockSpec(tile_shape, lambda g, b, i: (g, 0, 0, 0))],
        out_specs=pl.BlockSpec((MOBA_BLOCK, gw), lambda g, b, i: (b * nblk + i, g)),
        out_shape=jax.ShapeDtypeStruct((bsz * seq, d // 2), BF16),
        scratch_shapes=[pltpu.VMEM((HEAD_DIM_MOBA, seq), BF16)],
        compiler_params=_params(("parallel", "parallel", "arbitrary"),
                                [(tile_shape, F32), ((seq, HEAD_DIM_MOBA), BF16), ((seq, HEAD_DIM_MOBA), BF16),
                                 ((MOBA_BLOCK, gw), BF16), ((MOBA_BLOCK, gw), BF16)],
                                [((MOBA_BLOCK, MOBA_BLOCK), F32)] * 8 + [((HEAD_DIM_MOBA, seq), BF16)]),
        name="moba_attn",
    )(proj, proj, proj, kmean, bias_tiles)


def _merge_kernel(oa_ref, ob_ref, wa_ref, wb_ref, ga_ref, gb_ref, o_ref):
    ya = jnp.dot(oa_ref[...], wa_ref[...], preferred_element_type=F32)
    yb = jnp.dot(ob_ref[...], wb_ref[...], preferred_element_type=F32)
    merged = jax.nn.sigmoid(ga_ref[...].astype(F32)) * ya + jax.nn.sigmoid(gb_ref[...].astype(F32)) * yb
    o_ref[...] = merged.astype(o_ref.dtype)


def _merge(o_a, o_b, w_a, w_b, proj, d):
    t, kdim = o_a.shape
    tm, tn = _tile(t, 1024), _tile(d, 512)
    ga_col, gb_col = (11 * d // 8) // tn, (19 * d // 8) // tn
    assert (11 * d // 8) % tn == 0 and (19 * d // 8) % tn == 0
    return pl.pallas_call(
        _merge_kernel,
        grid=(t // tm, d // tn),
        in_specs=[pl.BlockSpec((tm, kdim), lambda i, j: (i, 0)),
                  pl.BlockSpec((tm, kdim), lambda i, j: (i, 0)),
                  pl.BlockSpec((kdim, tn), lambda i, j: (0, j)),
                  pl.BlockSpec((kdim, tn), lambda i, j: (0, j)),
                  pl.BlockSpec((tm, tn), lambda i, j: (i, ga_col + j)),
                  pl.BlockSpec((tm, tn), lambda i, j: (i, gb_col + j))],
        out_specs=pl.BlockSpec((tm, tn), lambda i, j: (i, j)),
        out_shape=jax.ShapeDtypeStruct((t, d), BF16),
        compiler_params=_params(("parallel", "arbitrary"),
                                [((tm, kdim), BF16)] * 2 + [((kdim, tn), BF16)] * 2 + [((tm, tn), BF16)] * 3,
                                [((tm, tn), F32)] * 3),
        name="merge",
    )(o_a, o_b, w_a, w_b, proj, proj)


def _post_mix_kernel(y_ref, x_ref, gpost_ref, gpre_ref, mod_ref, wr_ref, br_ref,
                     x1_ref, h2_ref, idx_ref, wts_ref):
    m = mod_ref[0]
    x1 = x_ref[...] + m[2:3, :] * _rms(y_ref[...], gpost_ref[...])
    x1_ref[...] = x1
    h2 = _rms(x1, gpre_ref[...]) * (1.0 + m[4:5, :]) + m[3:4, :]
    half = h2.shape[1] // 2
    h2_ref[...] = _pack_bf16_pair(h2[:, :half], h2[:, half:])
    logits = jnp.dot(h2, wr_ref[...], preferred_element_type=F32,
                     precision=lax.Precision.HIGHEST) + br_ref[...]
    lane = lax.broadcasted_iota(I32, logits.shape, 1)
    logits = jnp.where(lane < N_EXPERTS, logits, NEG_INF)
    idx_out = jnp.zeros(logits.shape, I32)
    val_out = jnp.zeros(logits.shape, F32)
    top = None
    den = jnp.zeros((logits.shape[0], 1), F32)
    for k in range(TOP_K):
        vmax = jnp.max(logits, axis=-1, keepdims=True)
        idx = jnp.min(jnp.where(logits == vmax, lane, LANES), axis=-1, keepdims=True)
        top = vmax if top is None else top
        e = jnp.exp(vmax - top)
        den = den + e
        idx_out = jnp.where(lane == k, idx, idx_out)
        val_out = jnp.where(lane == k, e, val_out)
        logits = jnp.where(lane == idx, NEG_INF, logits)
    idx_ref[...] = idx_out
    wts_ref[...] = val_out / den


def _post_mix(y, x2, g_post, g_pre, mod, w_router, b_router, seq):
    t, d = x2.shape
    tm = _tile(seq, 256)
    wr = jnp.pad(w_router, ((0, 0), (0, LANES - N_EXPERTS)))
    br = jnp.pad(b_router, (0, LANES - N_EXPERTS)).reshape(1, LANES)
    row = lambda i: (i, 0)
    const = lambda i: (0, 0)
    return pl.pallas_call(
        _post_mix_kernel,
        grid=(t // tm,),
        in_specs=[pl.BlockSpec((tm, d), row), pl.BlockSpec((tm, d), row),
                  pl.BlockSpec((1, d), const), pl.BlockSpec((1, d), const),
                  pl.BlockSpec((1, 6, d), lambda i: (i * tm // seq, 0, 0)),
                  pl.BlockSpec((d, LANES), const), pl.BlockSpec((1, LANES), const)],
        out_specs=[pl.BlockSpec((tm, d), row), pl.BlockSpec((tm, d // 2), row),
                   pl.BlockSpec((tm, LANES), row), pl.BlockSpec((tm, LANES), row)],
        out_shape=[jax.ShapeDtypeStruct((t, d), F32), jax.ShapeDtypeStruct((t, d // 2), jnp.uint32),
                   jax.ShapeDtypeStruct((t, LANES), I32), jax.ShapeDtypeStruct((t, LANES), F32)],
        compiler_params=_params(("parallel",), [((tm, d), F32)] * 4 + [((d, LANES), F32)],
                                [((tm, d), F32)] * 3),
        name="post_mix_router",
    )(y, x2, g_post.reshape(1, d), g_pre.reshape(1, d), mod, wr, br)


def _routing(top_idx, n_blocks):
    flat_e = top_idx.reshape(-1)
    onehot = (flat_e[:, None] == jnp.arange(N_EXPERTS, dtype=I32)[None, :]).astype(I32)
    csum = jnp.cumsum(onehot, axis=0)
    counts = csum[-1]
    rank = jnp.sum(csum * onehot, axis=1) - 1
    padded = (counts + MOE_BLOCK - 1) // MOE_BLOCK * MOE_BLOCK
    pad_end = jnp.cumsum(padded)
    pad_start = pad_end - padded
    dest = jnp.sum(onehot * pad_start[None, :], axis=1) + rank
    n_real = (pad_end[-1] // MOE_BLOCK).astype(I32)
    blk_start = jnp.arange(n_blocks, dtype=I32) * MOE_BLOCK
    block_e = jnp.minimum(jnp.sum((pad_end[None, :] <= blk_start[:, None]).astype(I32), axis=1), N_EXPERTS - 1)
    last_blk = jnp.where(counts > 0, pad_end // MOE_BLOCK - 1, -1).astype(I32)
    return dest.astype(I32), block_e.astype(I32), n_real.reshape(1), last_blk


def _pack_bf16_pair(lo, hi):
    lo_bits = lax.bitcast_convert_type(lo.astype(BF16).astype(F32), jnp.uint32) >> 16
    hi_bits = lax.bitcast_convert_type(hi.astype(BF16).astype(F32), jnp.uint32) & jnp.uint32(0xFFFF0000)
    return hi_bits | lo_bits


def _unpack_bf16_pair(p):
    lo = lax.bitcast_convert_type(p << 16, F32).astype(BF16)
    hi = lax.bitcast_convert_type(p & jnp.uint32(0xFFFF0000), F32).astype(BF16)
    return lo, hi


def _dispatch_kernel(last_ref, nr_ref, dest_ref, h_ref, xg_hbm, zeros, zsem, sem, *, tokens, n_blocks):
    def zero_block(m):
        rows = pl.ds(pl.multiple_of(m * MOE_BLOCK, MOE_BLOCK), MOE_BLOCK)
        return pltpu.make_async_copy(zeros, xg_hbm.at[rows], zsem)

    @pl.when(pl.program_id(0) == 0)
    def _():
        zeros[...] = jnp.zeros(zeros.shape, zeros.dtype)
        for phase in ("start", "wait"):
            for e in range(N_EXPERTS):
                @pl.when(last_ref[e] >= 0)
                def _(e=e, phase=phase):
                    getattr(zero_block(last_ref[e]), phase)()

            def tail(m, carry, phase=phase):
                getattr(zero_block(m), phase)()
                return carry

            lax.fori_loop(nr_ref[0], n_blocks, tail, 0)

    def row_copy(t, row):
        return pltpu.make_async_copy(h_ref.at[pl.ds(t, 1)], xg_hbm.at[pl.ds(row, 1)], sem)

    def issue(t, carry):
        for k in range(TOP_K):
            row_copy(t, dest_ref[0, 0, t * TOP_K + k]).start()
        return carry

    lax.fori_loop(0, tokens, issue, 0)

    def drain(t, carry):
        for k in range(TOP_K):
            row_copy(0, 0).wait()
        return carry

    lax.fori_loop(0, tokens, drain, 0)


def _dispatch(h2p, dest, last_blk, n_real, n_blocks):
    t, dp = h2p.shape
    tokens = _tile(t, 256)
    steps = t // tokens
    return pl.pallas_call(
        functools.partial(_dispatch_kernel, tokens=tokens, n_blocks=n_blocks),
        grid_spec=pltpu.PrefetchScalarGridSpec(
            num_scalar_prefetch=2, grid=(steps,),
            in_specs=[pl.BlockSpec((1, 1, tokens * TOP_K), lambda i, last, nr: (i, 0, 0), memory_space=pltpu.SMEM),
                      pl.BlockSpec((tokens, dp), lambda i, last, nr: (i, 0))],
            out_specs=pl.BlockSpec(memory_space=pl.ANY),
            scratch_shapes=[pltpu.VMEM((MOE_BLOCK, dp), h2p.dtype),
                            pltpu.SemaphoreType.DMA(()), pltpu.SemaphoreType.DMA(())]),
        out_shape=jax.ShapeDtypeStruct((n_blocks * MOE_BLOCK, dp), h2p.dtype),
        compiler_params=_params(("arbitrary",), [((tokens, dp), h2p.dtype)], [((MOE_BLOCK, dp), h2p.dtype)]),
        name="moe_dispatch",
    )(last_blk, n_real, dest.reshape(steps, 1, tokens * TOP_K), h2p)


def _deinterleave_kernel(w_ref, p_ref, wg_ref, wl_ref):
    perm = p_ref[...]
    group = perm.shape[0]
    half = group // 2
    for c in range(w_ref.shape[2] // group):
        t = jnp.dot(w_ref[0, :, c * group:(c + 1) * group].astype(BF16), perm, preferred_element_type=F32)
        wg_ref[0, :, c * half:(c + 1) * half] = t[:, :half].astype(BF16)
        wl_ref[0, :, c * half:(c + 1) * half] = t[:, half:].astype(BF16)


def _deinterleave(w_gu):
    e, d, f2 = w_gu.shape
    group = 2 * LANES
    perm = np.zeros((group, group), np.float32)
    perm[2 * np.arange(LANES), np.arange(LANES)] = 1.0
    perm[2 * np.arange(LANES) + 1, LANES + np.arange(LANES)] = 1.0
    tk = _tile(d, 512)
    out = jax.ShapeDtypeStruct((e, d, f2 // 2), BF16)
    return pl.pallas_call(
        _deinterleave_kernel,
        grid=(e, d // tk),
        in_specs=[pl.BlockSpec((1, tk, f2), lambda i, j: (i, j, 0)),
                  pl.BlockSpec((group, group), lambda i, j: (0, 0))],
        out_specs=[pl.BlockSpec((1, tk, f2 // 2), lambda i, j: (i, j, 0))] * 2,
        out_shape=[out, out],
        compiler_params=_params(("parallel", "arbitrary"), [((tk, f2), F32), ((tk, f2), BF16)],
                                [((tk, group), F32)] * 4),
        name="moe_deinterleave",
    )(w_gu, jnp.asarray(perm, BF16))


def _gate_up_kernel(be_ref, nr_ref, x_ref, wg_ref, wl_ref, bg_ref, bl_ref, o_ref):
    real = pl.program_id(1) < nr_ref[0]

    @pl.when(real)
    def _():
        lo, hi = _unpack_bf16_pair(x_ref[...])
        half = lo.shape[1]

        def proj(w_ref, b_ref):
            return (jnp.dot(lo, w_ref[0, :half, :], preferred_element_type=F32)
                    + jnp.dot(hi, w_ref[0, half:, :], preferred_element_type=F32) + b_ref[0])

        glu = jnp.minimum(proj(wg_ref, bg_ref), SWIGLU_LIMIT)
        lin = jnp.clip(proj(wl_ref, bl_ref), -SWIGLU_LIMIT, SWIGLU_LIMIT)
        o_ref[...] = (glu * jax.nn.sigmoid(SWIGLU_ALPHA * glu) * (lin + 1.0)).astype(o_ref.dtype)

    @pl.when(jnp.logical_not(real))
    def _():
        o_ref[...] = jnp.zeros(o_ref.shape, o_ref.dtype)


def _gate_up(xg, w_g, w_l, b_g, b_l, block_e, n_real, n_blocks):
    rows, dp = xg.shape
    d, f = w_g.shape[1], w_g.shape[2]
    passes = 2
    tf = f // passes
    blk = lambda p, m, be, nr: jnp.minimum(m, nr[0] - 1)
    wmap = lambda p, m, be, nr: (be[blk(p, m, be, nr)], 0, p)
    return pl.pallas_call(
        _gate_up_kernel,
        grid_spec=pltpu.PrefetchScalarGridSpec(
            num_scalar_prefetch=2, grid=(passes, n_blocks),
            in_specs=[pl.BlockSpec((MOE_BLOCK, dp), lambda p, m, be, nr: (blk(p, m, be, nr), 0)),
                      pl.BlockSpec((1, d, tf), wmap), pl.BlockSpec((1, d, tf), wmap),
                      pl.BlockSpec((1, 1, tf), wmap), pl.BlockSpec((1, 1, tf), wmap)],
            out_specs=pl.BlockSpec((MOE_BLOCK, tf), lambda p, m, be, nr: (m, p))),
        out_shape=jax.ShapeDtypeStruct((rows, f), BF16),
        compiler_params=_params(("arbitrary", "arbitrary"),
                                [((MOE_BLOCK, dp), xg.dtype), ((d, tf), BF16), ((d, tf), BF16),
                                 ((MOE_BLOCK, tf), BF16)],
                                [((MOE_BLOCK, d), BF16)] + [((MOE_BLOCK, tf), F32)] * 4),
        name="moe_gate_up",
    )(block_e, n_real, xg, w_g, w_l, b_g, b_l)


def _down_kernel(be_ref, nr_ref, a_ref, w_ref, b_ref, o_ref, wb_ref):
    m = pl.program_id(1)
    real = m < nr_ref[0]
    fresh = jnp.logical_or(m == 0, be_ref[m] != be_ref[jnp.maximum(m - 1, 0)])

    @pl.when(jnp.logical_and(real, fresh))
    def _():
        wb_ref[...] = w_ref[0].astype(BF16)

    @pl.when(real)
    def _():
        y = jnp.dot(a_ref[...], wb_ref[...], preferred_element_type=F32) + b_ref[0]
        quarter = y.shape[1] // 2
        o_ref[...] = _pack_bf16_pair(y[:, :quarter], y[:, quarter:])

    @pl.when(jnp.logical_not(real))
    def _():
        o_ref[...] = jnp.zeros(o_ref.shape, o_ref.dtype)


def _down(act, w_d, b_d, block_e, n_real, n_blocks):
    rows, f = act.shape
    d = w_d.shape[2]
    passes = 2
    tn = d // passes
    blk = lambda p, m, be, nr: jnp.minimum(m, nr[0] - 1)
    wmap = lambda p, m, be, nr: (be[blk(p, m, be, nr)], 0, p)
    return pl.pallas_call(
        _down_kernel,
        grid_spec=pltpu.PrefetchScalarGridSpec(
            num_scalar_prefetch=2, grid=(passes, n_blocks),
            in_specs=[pl.BlockSpec((MOE_BLOCK, f), lambda p, m, be, nr: (m, 0)),
                      pl.BlockSpec((1, f, tn), wmap), pl.BlockSpec((1, 1, tn), wmap)],
            out_specs=pl.BlockSpec((MOE_BLOCK, tn // 2), lambda p, m, be, nr: (m, p)),
            scratch_shapes=[pltpu.VMEM((f, tn), BF16)]),
        out_shape=jax.ShapeDtypeStruct((rows, d // 2), jnp.uint32),
        compiler_params=_params(("arbitrary", "arbitrary"),
                                [((MOE_BLOCK, f), BF16), ((f, tn), F32), ((MOE_BLOCK, tn // 2), jnp.uint32)],
                                [((f, tn), BF16), ((MOE_BLOCK, tn), F32)]),
        name="moe_down",
    )(block_e, n_real, act, w_d, b_d)


def _combine_kernel(dcur_ref, dnext_ref, w_ref, x1_ref, g_ref, mod_ref, y_hbm, o_ref, buf, sem,
                    *, tokens, steps):
    step = pl.program_id(0)
    slot = step % 2

    def row_copy(slot_, k, t, row):
        return pltpu.make_async_copy(y_hbm.at[pl.ds(row, 1)], buf.at[slot_, k, pl.ds(t, 1)], sem.at[slot_])

    def start_tile(dest_ref, slot_):
        def body(t, carry):
            for k in range(TOP_K):
                row_copy(slot_, k, t, dest_ref[0, 0, t * TOP_K + k]).start()
            return carry
        lax.fori_loop(0, tokens, body, 0)

    @pl.when(step == 0)
    def _():
        start_tile(dcur_ref, 0)

    @pl.when(step + 1 < steps)
    def _():
        start_tile(dnext_ref, 1 - slot)

    def drain(t, carry):
        for k in range(TOP_K):
            row_copy(slot, k, 0, 0).wait()
        return carry

    lax.fori_loop(0, tokens, drain, 0)

    w = w_ref[...]
    acc_lo = acc_hi = None
    for k in range(TOP_K):
        word = buf[slot, k]
        lo = lax.bitcast_convert_type(word << 16, F32) * w[:, k:k + 1]
        hi = lax.bitcast_convert_type(word & jnp.uint32(0xFFFF0000), F32) * w[:, k:k + 1]
        acc_lo = lo if acc_lo is None else acc_lo + lo
        acc_hi = hi if acc_hi is None else acc_hi + hi
    quarter = acc_lo.shape[1] // 2
    acc = jnp.concatenate([acc_lo[:, :quarter], acc_hi[:, :quarter], acc_lo[:, quarter:], acc_hi[:, quarter:]],
                          axis=1)
    m = mod_ref[0]
    o_ref[...] = x1_ref[...] + m[5:6, :] * _rms(acc, g_ref[...])


def _combine(y_sorted, dest, top_w, x1, g_post, mod, seq):
    t, d = x1.shape
    dp = y_sorted.shape[1]
    tokens = _tile(seq, 128)
    steps = t // tokens
    row = lambda i: (i, 0)
    dest3 = dest.reshape(steps, 1, tokens * TOP_K)
    return pl.pallas_call(
        functools.partial(_combine_kernel, tokens=tokens, steps=steps),
        grid=(steps,),
        in_specs=[pl.BlockSpec((1, 1, tokens * TOP_K), lambda i: (i, 0, 0), memory_space=pltpu.SMEM),
                  pl.BlockSpec((1, 1, tokens * TOP_K), lambda i: (jnp.minimum(i + 1, steps - 1), 0, 0),
                               memory_space=pltpu.SMEM),
                  pl.BlockSpec((tokens, LANES), row),
                  pl.BlockSpec((tokens, d), row),
                  pl.BlockSpec((1, d), lambda i: (0, 0)),
                  pl.BlockSpec((1, 6, d), lambda i: (i * tokens // seq, 0, 0)),
                  pl.BlockSpec(memory_space=pl.ANY)],
        out_specs=pl.BlockSpec((tokens, d), row),
        out_shape=jax.ShapeDtypeStruct((t, d), F32),
        scratch_shapes=[pltpu.VMEM((2, TOP_K, tokens, dp), y_sorted.dtype), pltpu.SemaphoreType.DMA((2,))],
        compiler_params=_params(("arbitrary",), [((tokens, d), F32)] * 2,
                                [((2, TOP_K, tokens, dp), y_sorted.dtype)] + [((tokens, d), F32)] * 3),
        name="moe_combine",
    )(dest3, dest3, top_w, x1, g_post.reshape(1, d), mod, y_sorted)


def _moe(h2, top_idx, top_w, x1, g_post, mod, w_gu, b_gu, w_dn, b_dn, seq):
    t, d = x1.shape
    f = w_dn.shape[1]
    n_blocks = -(-(t * TOP_K) // MOE_BLOCK) + N_EXPERTS
    dest, block_e, n_real, last_blk = _routing(top_idx, n_blocks)
    xg = _dispatch(h2, dest, last_blk, n_real, n_blocks)
    w_g, w_l = _deinterleave(w_gu)
    b_g = b_gu[:, 0::2].reshape(N_EXPERTS, 1, f)
    b_l = b_gu[:, 1::2].reshape(N_EXPERTS, 1, f)
    act = _gate_up(xg, w_g, w_l, b_g, b_l, block_e, n_real, n_blocks)
    y_sorted = _down(act, w_dn, b_dn.reshape(N_EXPERTS, 1, d), block_e, n_real, n_blocks)
    return _combine(y_sorted, dest, top_w, x1, g_post, mod, seq)


def kernel(x, c, w_ada, b_ada, g_pre_mix, g_post_mix, g_pre_ffn, g_post_ffn, w_in, b_in, sinks, rel_bias,
           w_o_swa, w_o_moba, w_out, w_router, b_router, w_gate_up, b_gate_up, w_down, b_down):
    bsz, seq, d = x.shape
    depth = w_ada.shape[0]
    assert seq % MOBA_BLOCK == 0 and d % 1024 == 0
    x2 = x.reshape(bsz * seq, d)
    cos_t, sin_t = _rope_tables(seq)
    bias_tiles = _bias_tiles(rel_bias)
    for l in range(depth):
        mod = _ada_mod(c, w_ada[l], b_ada[l])
        h = _prenorm(x2, g_pre_mix[l], mod, seq, shift_row=0, scale_row=1)
        proj = _matmul(h, w_in[l].astype(BF16), b_in[l], BF16, 1024, 768, "in_proj")
        o_a = _swa_attention(proj, sinks[l], cos_t, sin_t, bsz, seq, d)
        o_b = _moba_attention(proj, _kmean(proj, bsz, seq, d), bias_tiles, bsz, seq, d)
        merged = _merge(o_a, o_b, w_o_swa[l].astype(BF16), w_o_moba[l].astype(BF16), proj, d)
        y = _matmul(merged, w_out[l].astype(BF16), jnp.zeros((d,), F32), F32, 1024, 1024, "out_proj")
        x1, h2, idx, wts = _post_mix(y, x2, g_post_mix[l], g_pre_ffn[l], mod, w_router[l], b_router[l], seq)
        x2 = _moe(h2, idx[:, :TOP_K], wts, x1, g_post_ffn[l], mod,
                  w_gate_up[l], b_gate_up[l], w_down[l], b_down[l], seq)
    return x2.reshape(bsz, seq, d)
```

```python
import functools
import math

import numpy as np
import jax
import jax.numpy as jnp
from jax import lax
from jax.experimental import pallas as pl
from jax.experimental.pallas import tpu as pltpu

F32 = jnp.float32
BF16 = jnp.bfloat16
I32 = jnp.int32

HEAD_DIM_SWA = 64
SWA_GROUP = 8
WINDOW = 128
ROPE_THETA = 150000.0
HEAD_DIM_MOBA = 128
MOBA_GROUP = 4
MOBA_BLOCK = 256
MOBA_TOPK = 3
REL_BUCKETS = 32
REL_MAX_DIST = 1024
N_EXPERTS = 32
TOP_K = 4
SWIGLU_LIMIT = 7.0
SWIGLU_ALPHA = 1.702
MOE_BLOCK = 256
RMS_EPS = 1e-6

LANES = 128
SUBLANES = 8
BF16_SUBLANES = 16
VMEM_BYTES_V7X = 64 * 1024 * 1024
VMEM_CAP = VMEM_BYTES_V7X - 8 * 1024 * 1024

NEG_INF = float("-inf")
LOG2E = 1.0 / math.log(2.0)
N_BIAS_TILES = (REL_MAX_DIST + MOBA_BLOCK - 1) // MOBA_BLOCK + 2


def _nbytes(shape, dtype):
    return int(np.prod(shape)) * jnp.dtype(dtype).itemsize


def _params(semantics, blocks, temps=()):
    need = 2 * sum(_nbytes(s, d) for s, d in blocks) + sum(_nbytes(s, d) for s, d in temps)
    need = max(need + need // 4, 16 * 1024 * 1024)
    return pltpu.CompilerParams(dimension_semantics=semantics, vmem_limit_bytes=min(need, VMEM_CAP))


def _tile(n, pref):
    t = min(n, pref)
    assert n % t == 0, (n, pref)
    return t


def _ada_kernel(c_ref, w_ref, b_ref, o_ref):
    c = c_ref[...]
    s = c * jax.nn.sigmoid(c)
    o_ref[...] = jnp.dot(s, w_ref[...], preferred_element_type=F32,
                         precision=lax.Precision.HIGHEST) + b_ref[...]


def _ada_mod(c, w, b):
    bsz, d = c.shape
    n = w.shape[1]
    rows = -(-bsz // SUBLANES) * SUBLANES
    cp = jnp.pad(c, ((0, rows - bsz), (0, 0)))
    tn = _tile(n, 512)
    out = pl.pallas_call(
        _ada_kernel,
        grid=(n // tn,),
        in_specs=[pl.BlockSpec((rows, d), lambda j: (0, 0)),
                  pl.BlockSpec((d, tn), lambda j: (0, j)),
                  pl.BlockSpec((1, tn), lambda j: (0, j))],
        out_specs=pl.BlockSpec((rows, tn), lambda j: (0, j)),
        out_shape=jax.ShapeDtypeStruct((rows, n), F32),
        compiler_params=_params(("arbitrary",), [((d, tn), F32), ((rows, d), F32)], [((d, tn), F32)]),
        name="ada_mod",
    )(cp, w, b.reshape(1, n))
    return out[:bsz].reshape(bsz, 6, d)


def _rms(x, g):
    return x * lax.rsqrt(jnp.mean(x * x, axis=-1, keepdims=True) + RMS_EPS) * g


def _prenorm_kernel(x_ref, g_ref, mod_ref, o_ref, *, shift_row, scale_row):
    m = mod_ref[0]
    y = _rms(x_ref[...], g_ref[...])
    o_ref[...] = (y * (1.0 + m[scale_row:scale_row + 1, :]) + m[shift_row:shift_row + 1, :]).astype(o_ref.dtype)


def _prenorm(x2, g, mod, seq, shift_row, scale_row):
    t, d = x2.shape
    tm = _tile(seq, 256)
    return pl.pallas_call(
        functools.partial(_prenorm_kernel, shift_row=shift_row, scale_row=scale_row),
        grid=(t // tm,),
        in_specs=[pl.BlockSpec((tm, d), lambda i: (i, 0)),
                  pl.BlockSpec((1, d), lambda i: (0, 0)),
                  pl.BlockSpec((1, 6, d), lambda i: (i * tm // seq, 0, 0))],
        out_specs=pl.BlockSpec((tm, d), lambda i: (i, 0)),
        out_shape=jax.ShapeDtypeStruct((t, d), BF16),
        compiler_params=_params(("parallel",), [((tm, d), F32), ((tm, d), BF16)], [((tm, d), F32)] * 2),
        name="prenorm",
    )(x2, g.reshape(1, d), mod)


def _mm_kernel(a_ref, w_ref, b_ref, o_ref):
    acc = jnp.dot(a_ref[...], w_ref[...], preferred_element_type=F32)
    o_ref[...] = (acc + b_ref[...]).astype(o_ref.dtype)


def _matmul(a, w, bias, out_dtype, tm, tn, name):
    m, k = a.shape
    n = w.shape[1]
    tm, tn = _tile(m, tm), _tile(n, tn)
    return pl.pallas_call(
        _mm_kernel,
        grid=(m // tm, n // tn),
        in_specs=[pl.BlockSpec((tm, k), lambda i, j: (i, 0)),
                  pl.BlockSpec((k, tn), lambda i, j: (0, j)),
                  pl.BlockSpec((1, tn), lambda i, j: (0, j))],
        out_specs=pl.BlockSpec((tm, tn), lambda i, j: (i, j)),
        out_shape=jax.ShapeDtypeStruct((m, n), out_dtype),
        compiler_params=_params(("parallel", "arbitrary"),
                                [((tm, k), a.dtype), ((k, tn), w.dtype), ((tm, tn), out_dtype)],
                                [((tm, tn), F32)]),
        name=name,
    )(a, w, bias.reshape(1, n).astype(F32))


def _rope_tables(seq):
    half = HEAD_DIM_SWA // 2
    inv = ROPE_THETA ** (-jnp.arange(half, dtype=F32) / half)
    ang = jnp.arange(seq, dtype=F32)[:, None] * inv[None, :]
    cos, sin = jnp.cos(ang), jnp.sin(ang)
    cos_t = jnp.concatenate([cos, cos, cos, cos], axis=-1)
    sin_t = jnp.concatenate([-sin, sin, -sin, sin], axis=-1)
    return cos_t, sin_t


def _swa_kernel(sinks_ref, q_ref, kp_ref, kc_ref, vp_ref, vc_ref, cq_ref, sq_ref, cp_ref, sp_ref, o_ref,
                *, n_kv):
    n = pl.program_id(1)
    w = WINDOW
    half = HEAD_DIM_SWA // 2
    lane = lax.broadcasted_iota(I32, (1, LANES), 1)
    first_half = (lane % HEAD_DIM_SWA) < half
    low_head = lane < HEAD_DIM_SWA

    def rope(xf, cos, sin):
        partner = jnp.where(first_half, pltpu.roll(xf, LANES - half, 1), pltpu.roll(xf, half, 1))
        return xf * cos + partner * sin

    cos_q, sin_q = cq_ref[...], sq_ref[...]
    cos_k = jnp.concatenate([cp_ref[...], cos_q], axis=0)
    sin_k = jnp.concatenate([sp_ref[...], sin_q], axis=0)
    ri = lax.broadcasted_iota(I32, (w, 2 * w), 0)
    cj = lax.broadcasted_iota(I32, (w, 2 * w), 1)
    diff = w + ri - cj
    mask = (diff >= 0) & (diff < w) & ((cj >= w) | (n > 0))
    pairs = SWA_GROUP // 2

    for g in range(n_kv):
        slab = g // 2
        cols = slice(slab * LANES, (slab + 1) * LANES)
        k2 = rope(jnp.concatenate([kp_ref[:, cols], kc_ref[:, cols]], axis=0).astype(F32), cos_k, sin_k)
        v2 = jnp.concatenate([vp_ref[:, cols], vc_ref[:, cols]], axis=0).astype(F32)
        k2s, v2s = pltpu.roll(k2, HEAD_DIM_SWA, 1), pltpu.roll(v2, HEAD_DIM_SWA, 1)
        if g % 2 == 0:
            k_lo, v_lo = jnp.where(low_head, k2, 0.0), jnp.where(low_head, v2, 0.0)
            k_hi, v_hi = jnp.where(low_head, 0.0, k2s), jnp.where(low_head, 0.0, v2s)
        else:
            k_lo, v_lo = jnp.where(low_head, k2s, 0.0), jnp.where(low_head, v2s, 0.0)
            k_hi, v_hi = jnp.where(low_head, 0.0, k2), jnp.where(low_head, 0.0, v2)
        kv = ((k_lo.astype(BF16), v_lo.astype(BF16)), (k_hi.astype(BF16), v_hi.astype(BF16)))
        for p in range(pairs):
            qs = g * pairs + p
            qcols = slice(qs * LANES, (qs + 1) * LANES)
            q = rope(q_ref[:, qcols].astype(F32), cos_q, sin_q) * (1.0 / math.sqrt(HEAD_DIM_SWA))
            q = q.astype(BF16)
            o = jnp.zeros((w, LANES), F32)
            for hh, (kk, vv) in enumerate(kv):
                s = lax.dot_general(q, kk, (((1,), (1,)), ((), ())), preferred_element_type=F32)
                s = jnp.where(mask, s, NEG_INF)
                sink = sinks_ref[2 * qs + hh]
                m = jnp.maximum(jnp.max(s, axis=-1, keepdims=True), sink)
                e = jnp.exp(s - m)
                den = jnp.sum(e, axis=-1, keepdims=True) + jnp.exp(sink - m)
                o = o + jnp.dot(e.astype(BF16), vv, preferred_element_type=F32) / den
            o_ref[:, qcols] = o.astype(o_ref.dtype)


def _swa_attention(proj, sinks, cos_t, sin_t, bsz, seq, d):
    q_w, kv_w = d // 2, d // 16
    n_kv = kv_w // HEAD_DIM_SWA
    assert n_kv % 2 == 0 and q_w % kv_w == 0
    nb = seq // WINDOW
    k_col, v_col = q_w // kv_w, q_w // kv_w + 1
    cur = lambda b, n: b * nb + n
    prev = lambda b, n: b * nb + jnp.maximum(n - 1, 0)
    return pl.pallas_call(
        functools.partial(_swa_kernel, n_kv=n_kv),
        grid=(bsz, nb),
        in_specs=[pl.BlockSpec(memory_space=pltpu.SMEM),
                  pl.BlockSpec((WINDOW, q_w), lambda b, n: (cur(b, n), 0)),
                  pl.BlockSpec((WINDOW, kv_w), lambda b, n: (prev(b, n), k_col)),
                  pl.BlockSpec((WINDOW, kv_w), lambda b, n: (cur(b, n), k_col)),
                  pl.BlockSpec((WINDOW, kv_w), lambda b, n: (prev(b, n), v_col)),
                  pl.BlockSpec((WINDOW, kv_w), lambda b, n: (cur(b, n), v_col)),
                  pl.BlockSpec((WINDOW, LANES), lambda b, n: (n, 0)),
                  pl.BlockSpec((WINDOW, LANES), lambda b, n: (n, 0)),
                  pl.BlockSpec((WINDOW, LANES), lambda b, n: (jnp.maximum(n - 1, 0), 0)),
                  pl.BlockSpec((WINDOW, LANES), lambda b, n: (jnp.maximum(n - 1, 0), 0))],
        out_specs=pl.BlockSpec((WINDOW, q_w), lambda b, n: (cur(b, n), 0)),
        out_shape=jax.ShapeDtypeStruct((bsz * seq, q_w), BF16),
        compiler_params=_params(("parallel", "arbitrary"),
                                [((WINDOW, q_w), BF16)] * 2 + [((WINDOW, kv_w), BF16)] * 4,
                                [((2 * WINDOW, 2 * WINDOW), F32)] * 16),
        name="swa_attn",
    )(sinks, proj, proj, proj, proj, proj, cos_t, sin_t, cos_t, sin_t)


def _t5_thresholds():
    exact = REL_BUCKETS // 2
    d = np.arange(exact, 2 * REL_MAX_DIST, dtype=np.float64)
    large = exact + np.floor(np.log(d / exact) / math.log(REL_MAX_DIST / exact) * (REL_BUCKETS - exact)).astype(np.int64)
    large = np.minimum(large, REL_BUCKETS - 1)
    return [int(exact + np.argmax(large >= b)) for b in range(exact + 1, REL_BUCKETS)]


def _bias_tile_kernel(table_ref, o_ref):
    dd = pl.program_id(0)
    h = pl.program_id(1)
    exact = REL_BUCKETS // 2
    key = lax.broadcasted_iota(I32, (MOBA_BLOCK, MOBA_BLOCK), 0)
    qry = lax.broadcasted_iota(I32, (MOBA_BLOCK, MOBA_BLOCK), 1)
    dist = jnp.maximum(dd * MOBA_BLOCK + qry - key, 0)
    bucket = jnp.minimum(dist, exact)
    for thr in _t5_thresholds():
        bucket = bucket + (dist >= thr).astype(I32)
    val = jnp.full((MOBA_BLOCK, MOBA_BLOCK), table_ref[REL_BUCKETS - 1, h], F32)
    for b in range(REL_BUCKETS - 2, -1, -1):
        val = jnp.where(bucket == b, table_ref[b, h], val)
    o_ref[0, 0] = val * LOG2E


def _bias_tiles(rel_bias):
    n_heads = rel_bias.shape[1]
    return pl.pallas_call(
        _bias_tile_kernel,
        grid=(N_BIAS_TILES, n_heads),
        in_specs=[pl.BlockSpec(memory_space=pltpu.SMEM)],
        out_specs=pl.BlockSpec((1, 1, MOBA_BLOCK, MOBA_BLOCK), lambda dd, h: (h, dd, 0, 0)),
        out_shape=jax.ShapeDtypeStruct((n_heads, N_BIAS_TILES, MOBA_BLOCK, MOBA_BLOCK), F32),
        compiler_params=_params(("arbitrary", "arbitrary"), [((MOBA_BLOCK, MOBA_BLOCK), F32)],
                                [((MOBA_BLOCK, MOBA_BLOCK), F32)] * 4),
        name="moba_bias_tiles",
    )(rel_bias)


def _kmean_kernel(k_ref, o_ref, *, nblk):
    for j in range(nblk):
        rows = k_ref[j * MOBA_BLOCK:(j + 1) * MOBA_BLOCK, :].astype(F32)
        o_ref[j:j + 1, :] = jnp.mean(rows, axis=0, keepdims=True)


def _kmean(proj, bsz, seq, d):
    kv_w = d // 8
    nblk = seq // MOBA_BLOCK
    col = (9 * d // 8) // kv_w
    return pl.pallas_call(
        functools.partial(_kmean_kernel, nblk=nblk),
        grid=(bsz,),
        in_specs=[pl.BlockSpec((seq, kv_w), lambda b: (b, col))],
        out_specs=pl.BlockSpec((nblk, kv_w), lambda b: (b, 0)),
        out_shape=jax.ShapeDtypeStruct((bsz * nblk, kv_w), F32),
        compiler_params=_params(("parallel",), [((seq, kv_w), BF16)], [((MOBA_BLOCK, kv_w), F32)] * 2),
        name="moba_kmean",
    )(proj)


def _moba_kernel(q_ref, k_ref, v_ref, km_ref, bias_ref, o_ref, vt_ref, *, nblk):
    i = pl.program_id(2)
    mb, dh = MOBA_BLOCK, HEAD_DIM_MOBA
    scale = 1.0 / math.sqrt(dh)
    nt = (((1,), (1,)), ((), ()))
    blk = lax.broadcasted_iota(I32, (nblk, mb), 0)
    key = lax.broadcasted_iota(I32, (mb, mb), 0)
    qry = lax.broadcasted_iota(I32, (mb, mb), 1)
    causal = qry >= key

    @pl.when(i == 0)
    def _():
        for j in range(nblk):
            vt_ref[:dh, j * mb:(j + 1) * mb] = v_ref[j * mb:(j + 1) * mb, :].astype(F32).T.astype(BF16)
        vt_ref[dh:, :] = jnp.ones((vt_ref.shape[0] - dh, vt_ref.shape[1]), BF16)

    kmean = km_ref[...]
    km_hi = kmean.astype(BF16)
    rest = kmean - km_hi.astype(F32)
    km_mid = rest.astype(BF16)
    km_lo = (rest - km_mid.astype(F32)).astype(BF16)
    own = pl.multiple_of(i * mb, mb)
    k_own, vt_own = k_ref[pl.ds(own, mb), :], vt_ref[:, pl.ds(own, mb)]

    qs = [q_ref[:, hh * dh:(hh + 1) * dh] for hh in range(MOBA_GROUP)]
    gates = [lax.dot_general(km_hi, q, nt, preferred_element_type=F32)
             + lax.dot_general(km_mid, q, nt, preferred_element_type=F32)
             + lax.dot_general(km_lo, q, nt, preferred_element_type=F32) for q in qs]
    own_scores = [lax.dot_general(k_own, q, nt, preferred_element_type=F32) for q in qs]
    picks, init = [], []
    for hh in range(MOBA_GROUP):
        gate = jnp.where(blk < i, gates[hh], NEG_INF)
        pick = []
        for t in range(MOBA_TOPK):
            gmax = jnp.max(gate, axis=0, keepdims=True)
            idx = jnp.min(jnp.where(gate == gmax, blk, nblk), axis=0, keepdims=True)
            pick.append(jnp.where(t < i, idx, -1))
            gate = jnp.where(blk == idx, NEG_INF, gate)

        s = jnp.where(causal, own_scores[hh] * (scale * LOG2E) + bias_ref[hh, 0], NEG_INF)
        m0 = jnp.max(s, axis=0, keepdims=True)
        a0 = jnp.dot(vt_own, jnp.exp2(s - m0).astype(BF16), preferred_element_type=F32)
        picks.append(pick)
        init.append((m0, a0))

    def past(jj, carry):
        ja = 2 * jj
        start = pl.multiple_of(ja * mb, mb)
        kj, vtj = k_ref[pl.ds(start, 2 * mb), :], vt_ref[:, pl.ds(start, 2 * mb)]
        tile_a = jnp.minimum(i - ja, N_BIAS_TILES - 1)
        tile_b = jnp.minimum(i - ja - 1, N_BIAS_TILES - 1)
        scores = [lax.dot_general(kj, qs[hh], nt, preferred_element_type=F32) for hh in range(MOBA_GROUP)]
        out = []
        for hh in range(MOBA_GROUP):
            m, acc = carry[hh]
            p0, p1, p2 = picks[hh]
            sa = scores[hh][:mb] * (scale * LOG2E) + bias_ref[hh, tile_a]
            sb = scores[hh][mb:] * (scale * LOG2E) + bias_ref[hh, tile_b]
            sa = jnp.where((p0 == ja) | (p1 == ja) | (p2 == ja), sa, NEG_INF)
            sb = jnp.where((p0 == ja + 1) | (p1 == ja + 1) | (p2 == ja + 1), sb, NEG_INF)
            m_new = jnp.maximum(m, jnp.maximum(jnp.max(sa, axis=0, keepdims=True),
                                               jnp.max(sb, axis=0, keepdims=True)))
            e = jnp.concatenate([jnp.exp2(sa - m_new), jnp.exp2(sb - m_new)], axis=0).astype(BF16)
            acc = jnp.exp2(m - m_new) * acc + jnp.dot(vtj, e, preferred_element_type=F32)
            out.append((m_new, acc))
        return tuple(out)

    final = lax.fori_loop(0, (i + 1) // 2, past, tuple(init))
    for hh in range(MOBA_GROUP):
        _, acc = final[hh]
        o_ref[:, hh * dh:(hh + 1) * dh] = (acc[:dh] / acc[dh:dh + 1]).T.astype(o_ref.dtype)


def _moba_attention(proj, kmean, bias_tiles, bsz, seq, d):
    gw = MOBA_GROUP * HEAD_DIM_MOBA
    n_kv = (d // 8) // HEAD_DIM_MOBA
    nblk = seq // MOBA_BLOCK
    q_off, k_off, v_off = 5 * d // 8, 9 * d // 8, 10 * d // 8
    assert q_off % gw == 0 and MOBA_TOPK == 3
    qc, kc, vc = q_off // gw, k_off // HEAD_DIM_MOBA, v_off // HEAD_DIM_MOBA
    tile_shape = (MOBA_GROUP, N_BIAS_TILES, MOBA_BLOCK, MOBA_BLOCK)
    return pl.pallas_call(
        functools.partial(_moba_kernel, nblk=nblk),
        grid=(n_kv, bsz, nblk),
        in_specs=[pl.BlockSpec((MOBA_BLOCK, gw), lambda g, b, i: (b * nblk + i, qc + g)),
                  pl.BlockSpec((seq, HEAD_DIM_MOBA), lambda g, b, i: (b, kc + g)),
                  pl.BlockSpec((seq, HEAD_DIM_MOBA), lambda g, b, i: (b, vc + g)),
                  pl.BlockSpec((nblk, HEAD_DIM_MOBA), lambda g, b, i: (b, g)),
                  pl.BlockSpec(tile_shape, lambda g, b, i: (g, 0, 0, 0))],
        out_specs=pl.BlockSpec((MOBA_BLOCK, gw), lambda g, b, i: (b * nblk + i, g)),
        out_shape=jax.ShapeDtypeStruct((bsz * seq, d // 2), BF16),
        scratch_shapes=[pltpu.VMEM((HEAD_DIM_MOBA + BF16_SUBLANES, seq), BF16)],
        compiler_params=_params(("parallel", "parallel", "arbitrary"),
                                [(tile_shape, F32), ((seq, HEAD_DIM_MOBA), BF16), ((seq, HEAD_DIM_MOBA), BF16),
                                 ((MOBA_BLOCK, gw), BF16), ((MOBA_BLOCK, gw), BF16)],
                                [((2 * MOBA_BLOCK, MOBA_BLOCK), F32)] * 8
                                + [((HEAD_DIM_MOBA + BF16_SUBLANES, seq), BF16)]),
        name="moba_attn",
    )(proj, proj, proj, kmean, bias_tiles)


def _merge_kernel(oa_ref, ob_ref, wa_ref, wb_ref, ga_ref, gb_ref, o_ref):
    ya = jnp.dot(oa_ref[...], wa_ref[...], preferred_element_type=F32)
    yb = jnp.dot(ob_ref[...], wb_ref[...], preferred_element_type=F32)
    merged = jax.nn.sigmoid(ga_ref[...].astype(F32)) * ya + jax.nn.sigmoid(gb_ref[...].astype(F32)) * yb
    o_ref[...] = merged.astype(o_ref.dtype)


def _merge(o_a, o_b, w_a, w_b, proj, d):
    t, kdim = o_a.shape
    tm, tn = _tile(t, 1024), _tile(d, 512)
    ga_col, gb_col = (11 * d // 8) // tn, (19 * d // 8) // tn
    assert (11 * d // 8) % tn == 0 and (19 * d // 8) % tn == 0
    return pl.pallas_call(
        _merge_kernel,
        grid=(t // tm, d // tn),
        in_specs=[pl.BlockSpec((tm, kdim), lambda i, j: (i, 0)),
                  pl.BlockSpec((tm, kdim), lambda i, j: (i, 0)),
                  pl.BlockSpec((kdim, tn), lambda i, j: (0, j)),
                  pl.BlockSpec((kdim, tn), lambda i, j: (0, j)),
                  pl.BlockSpec((tm, tn), lambda i, j: (i, ga_col + j)),
                  pl.BlockSpec((tm, tn), lambda i, j: (i, gb_col + j))],
        out_specs=pl.BlockSpec((tm, tn), lambda i, j: (i, j)),
        out_shape=jax.ShapeDtypeStruct((t, d), BF16),
        compiler_params=_params(("parallel", "arbitrary"),
                                [((tm, kdim), BF16)] * 2 + [((kdim, tn), BF16)] * 2 + [((tm, tn), BF16)] * 3,
                                [((tm, tn), F32)] * 3),
        name="merge",
    )(o_a, o_b, w_a, w_b, proj, proj)


def _post_mix_kernel(y_ref, x_ref, gpost_ref, gpre_ref, mod_ref, wr_ref, br_ref,
                     x1_ref, h2_ref, idx_ref, wts_ref):
    m = mod_ref[0]
    x1 = x_ref[...] + m[2:3, :] * _rms(y_ref[...], gpost_ref[...])
    x1_ref[...] = x1
    h2 = _rms(x1, gpre_ref[...]) * (1.0 + m[4:5, :]) + m[3:4, :]
    half = h2.shape[1] // 2
    h2_ref[...] = _pack_bf16_pair(h2[:, :half], h2[:, half:])
    logits = jnp.dot(h2, wr_ref[...], preferred_element_type=F32,
                     precision=lax.Precision.HIGHEST) + br_ref[...]
    lane = lax.broadcasted_iota(I32, logits.shape, 1)
    logits = jnp.where(lane < N_EXPERTS, logits, NEG_INF)
    idx_out = jnp.zeros(logits.shape, I32)
    val_out = jnp.zeros(logits.shape, F32)
    top = None
    den = jnp.zeros((logits.shape[0], 1), F32)
    for k in range(TOP_K):
        vmax = jnp.max(logits, axis=-1, keepdims=True)
        idx = jnp.min(jnp.where(logits == vmax, lane, LANES), axis=-1, keepdims=True)
        top = vmax if top is None else top
        e = jnp.exp(vmax - top)
        den = den + e
        idx_out = jnp.where(lane == k, idx, idx_out)
        val_out = jnp.where(lane == k, e, val_out)
        logits = jnp.where(lane == idx, NEG_INF, logits)
    idx_ref[...] = idx_out
    wts_ref[...] = val_out / den


def _post_mix(y, x2, g_post, g_pre, mod, w_router, b_router, seq):
    t, d = x2.shape
    tm = _tile(seq, 256)
    wr = jnp.pad(w_router, ((0, 0), (0, LANES - N_EXPERTS)))
    br = jnp.pad(b_router, (0, LANES - N_EXPERTS)).reshape(1, LANES)
    row = lambda i: (i, 0)
    const = lambda i: (0, 0)
    return pl.pallas_call(
        _post_mix_kernel,
        grid=(t // tm,),
        in_specs=[pl.BlockSpec((tm, d), row), pl.BlockSpec((tm, d), row),
                  pl.BlockSpec((1, d), const), pl.BlockSpec((1, d), const),
                  pl.BlockSpec((1, 6, d), lambda i: (i * tm // seq, 0, 0)),
                  pl.BlockSpec((d, LANES), const), pl.BlockSpec((1, LANES), const)],
        out_specs=[pl.BlockSpec((tm, d), row), pl.BlockSpec((tm, d // 2), row),
                   pl.BlockSpec((tm, LANES), row), pl.BlockSpec((tm, LANES), row)],
        out_shape=[jax.ShapeDtypeStruct((t, d), F32), jax.ShapeDtypeStruct((t, d // 2), jnp.uint32),
                   jax.ShapeDtypeStruct((t, LANES), I32), jax.ShapeDtypeStruct((t, LANES), F32)],
        compiler_params=_params(("parallel",), [((tm, d), F32)] * 4 + [((d, LANES), F32)],
                                [((tm, d), F32)] * 3),
        name="post_mix_router",
    )(y, x2, g_post.reshape(1, d), g_pre.reshape(1, d), mod, wr, br)


def _routing(top_idx, n_blocks):
    flat_e = top_idx.reshape(-1)
    onehot = (flat_e[:, None] == jnp.arange(N_EXPERTS, dtype=I32)[None, :]).astype(I32)
    csum = jnp.cumsum(onehot, axis=0)
    counts = csum[-1]
    rank = jnp.sum(csum * onehot, axis=1) - 1
    padded = (counts + MOE_BLOCK - 1) // MOE_BLOCK * MOE_BLOCK
    pad_end = jnp.cumsum(padded)
    pad_start = pad_end - padded
    dest = jnp.sum(onehot * pad_start[None, :], axis=1) + rank
    n_real = (pad_end[-1] // MOE_BLOCK).astype(I32)
    blk_start = jnp.arange(n_blocks, dtype=I32) * MOE_BLOCK
    block_e = jnp.minimum(jnp.sum((pad_end[None, :] <= blk_start[:, None]).astype(I32), axis=1), N_EXPERTS - 1)
    last_blk = jnp.where(counts > 0, pad_end // MOE_BLOCK - 1, -1).astype(I32)
    return dest.astype(I32), block_e.astype(I32), n_real.reshape(1), last_blk


def _pack_bf16_pair(lo, hi):
    lo_bits = lax.bitcast_convert_type(lo.astype(BF16).astype(F32), jnp.uint32) >> 16
    hi_bits = lax.bitcast_convert_type(hi.astype(BF16).astype(F32), jnp.uint32) & jnp.uint32(0xFFFF0000)
    return hi_bits | lo_bits


def _unpack_bf16_pair(p):
    lo = lax.bitcast_convert_type(p << 16, F32).astype(BF16)
    hi = lax.bitcast_convert_type(p & jnp.uint32(0xFFFF0000), F32).astype(BF16)
    return lo, hi


def _dispatch_kernel(last_ref, nr_ref, dest_ref, h_ref, xg_hbm, zeros, zsem, sem, *, tokens, n_blocks):
    def zero_block(m):
        rows = pl.ds(pl.multiple_of(m * MOE_BLOCK, MOE_BLOCK), MOE_BLOCK)
        return pltpu.make_async_copy(zeros, xg_hbm.at[rows], zsem)

    @pl.when(pl.program_id(0) == 0)
    def _():
        zeros[...] = jnp.zeros(zeros.shape, zeros.dtype)
        for phase in ("start", "wait"):
            for e in range(N_EXPERTS):
                @pl.when(last_ref[e] >= 0)
                def _(e=e, phase=phase):
                    getattr(zero_block(last_ref[e]), phase)()

            def tail(m, carry, phase=phase):
                getattr(zero_block(m), phase)()
                return carry

            lax.fori_loop(nr_ref[0], n_blocks, tail, 0)

    def row_copy(t, row):
        return pltpu.make_async_copy(h_ref.at[pl.ds(t, 1)], xg_hbm.at[pl.ds(row, 1)], sem)

    def issue(t, carry):
        for k in range(TOP_K):
            row_copy(t, dest_ref[0, 0, t * TOP_K + k]).start()
        return carry

    lax.fori_loop(0, tokens, issue, 0)

    def drain(t, carry):
        for k in range(TOP_K):
            row_copy(0, 0).wait()
        return carry

    lax.fori_loop(0, tokens, drain, 0)


def _dispatch(h2p, dest, last_blk, n_real, n_blocks):
    t, dp = h2p.shape
    tokens = _tile(t, 256)
    steps = t // tokens
    return pl.pallas_call(
        functools.partial(_dispatch_kernel, tokens=tokens, n_blocks=n_blocks),
        grid_spec=pltpu.PrefetchScalarGridSpec(
            num_scalar_prefetch=2, grid=(steps,),
            in_specs=[pl.BlockSpec((1, 1, tokens * TOP_K), lambda i, last, nr: (i, 0, 0), memory_space=pltpu.SMEM),
                      pl.BlockSpec((tokens, dp), lambda i, last, nr: (i, 0))],
            out_specs=pl.BlockSpec(memory_space=pl.ANY),
            scratch_shapes=[pltpu.VMEM((MOE_BLOCK, dp), h2p.dtype),
                            pltpu.SemaphoreType.DMA(()), pltpu.SemaphoreType.DMA(())]),
        out_shape=jax.ShapeDtypeStruct((n_blocks * MOE_BLOCK, dp), h2p.dtype),
        compiler_params=_params(("arbitrary",), [((tokens, dp), h2p.dtype)], [((MOE_BLOCK, dp), h2p.dtype)]),
        name="moe_dispatch",
    )(last_blk, n_real, dest.reshape(steps, 1, tokens * TOP_K), h2p)


def _deinterleave_kernel(w_ref, p_ref, wg_ref, wl_ref):
    perm = p_ref[...]
    group = perm.shape[0]
    half = group // 2
    for c in range(w_ref.shape[2] // group):
        t = jnp.dot(w_ref[0, :, c * group:(c + 1) * group].astype(BF16), perm, preferred_element_type=F32)
        wg_ref[0, :, c * half:(c + 1) * half] = t[:, :half].astype(BF16)
        wl_ref[0, :, c * half:(c + 1) * half] = t[:, half:].astype(BF16)


def _deinterleave(w_gu):
    e, d, f2 = w_gu.shape
    group = 2 * LANES
    perm = np.zeros((group, group), np.float32)
    perm[2 * np.arange(LANES), np.arange(LANES)] = 1.0
    perm[2 * np.arange(LANES) + 1, LANES + np.arange(LANES)] = 1.0
    tk = _tile(d, 512)
    out = jax.ShapeDtypeStruct((e, d, f2 // 2), BF16)
    return pl.pallas_call(
        _deinterleave_kernel,
        grid=(e, d // tk),
        in_specs=[pl.BlockSpec((1, tk, f2), lambda i, j: (i, j, 0)),
                  pl.BlockSpec((group, group), lambda i, j: (0, 0))],
        out_specs=[pl.BlockSpec((1, tk, f2 // 2), lambda i, j: (i, j, 0))] * 2,
        out_shape=[out, out],
        compiler_params=_params(("parallel", "arbitrary"), [((tk, f2), F32), ((tk, f2), BF16)],
                                [((tk, group), F32)] * 4),
        name="moe_deinterleave",
    )(w_gu, jnp.asarray(perm, BF16))


def _gate_up_kernel(be_ref, nr_ref, x_ref, wg_ref, wl_ref, bg_ref, bl_ref, o_ref):
    real = pl.program_id(1) < nr_ref[0]

    @pl.when(real)
    def _():
        lo, hi = _unpack_bf16_pair(x_ref[...])
        half = lo.shape[1]

        def proj(w_ref, b_ref):
            return (jnp.dot(lo, w_ref[0, :half, :], preferred_element_type=F32)
                    + jnp.dot(hi, w_ref[0, half:, :], preferred_element_type=F32) + b_ref[0])

        glu = jnp.minimum(proj(wg_ref, bg_ref), SWIGLU_LIMIT)
        lin = jnp.clip(proj(wl_ref, bl_ref), -SWIGLU_LIMIT, SWIGLU_LIMIT)
        o_ref[...] = (glu * jax.nn.sigmoid(SWIGLU_ALPHA * glu) * (lin + 1.0)).astype(o_ref.dtype)

    @pl.when(jnp.logical_not(real))
    def _():
        o_ref[...] = jnp.zeros(o_ref.shape, o_ref.dtype)


def _gate_up(xg, w_g, w_l, b_g, b_l, block_e, n_real, n_blocks):
    rows, dp = xg.shape
    d, f = w_g.shape[1], w_g.shape[2]
    passes = 2
    tf = f // passes
    blk = lambda p, m, be, nr: jnp.minimum(m, nr[0] - 1)
    wmap = lambda p, m, be, nr: (be[blk(p, m, be, nr)], 0, p)
    return pl.pallas_call(
        _gate_up_kernel,
        grid_spec=pltpu.PrefetchScalarGridSpec(
            num_scalar_prefetch=2, grid=(passes, n_blocks),
            in_specs=[pl.BlockSpec((MOE_BLOCK, dp), lambda p, m, be, nr: (blk(p, m, be, nr), 0)),
                      pl.BlockSpec((1, d, tf), wmap), pl.BlockSpec((1, d, tf), wmap),
                      pl.BlockSpec((1, 1, tf), wmap), pl.BlockSpec((1, 1, tf), wmap)],
            out_specs=pl.BlockSpec((MOE_BLOCK, tf), lambda p, m, be, nr: (m, p))),
        out_shape=jax.ShapeDtypeStruct((rows, f), BF16),
        compiler_params=_params(("arbitrary", "arbitrary"),
                                [((MOE_BLOCK, dp), xg.dtype), ((d, tf), BF16), ((d, tf), BF16),
                                 ((MOE_BLOCK, tf), BF16)],
                                [((MOE_BLOCK, d), BF16)] + [((MOE_BLOCK, tf), F32)] * 4),
        name="moe_gate_up",
    )(block_e, n_real, xg, w_g, w_l, b_g, b_l)


def _down_kernel(be_ref, nr_ref, a_ref, w_ref, b_ref, o_ref, wb_ref):
    m = pl.program_id(1)
    real = m < nr_ref[0]
    fresh = jnp.logical_or(m == 0, be_ref[m] != be_ref[jnp.maximum(m - 1, 0)])

    @pl.when(jnp.logical_and(real, fresh))
    def _():
        wb_ref[...] = w_ref[0].astype(BF16)

    @pl.when(real)
    def _():
        y = jnp.dot(a_ref[...], wb_ref[...], preferred_element_type=F32) + b_ref[0]
        quarter = y.shape[1] // 2
        o_ref[...] = _pack_bf16_pair(y[:, :quarter], y[:, quarter:])

    @pl.when(jnp.logical_not(real))
    def _():
        o_ref[...] = jnp.zeros(o_ref.shape, o_ref.dtype)


def _down(act, w_d, b_d, block_e, n_real, n_blocks):
    rows, f = act.shape
    d = w_d.shape[2]
    passes = 2
    tn = d // passes
    blk = lambda p, m, be, nr: jnp.minimum(m, nr[0] - 1)
    wmap = lambda p, m, be, nr: (be[blk(p, m, be, nr)], 0, p)
    return pl.pallas_call(
        _down_kernel,
        grid_spec=pltpu.PrefetchScalarGridSpec(
            num_scalar_prefetch=2, grid=(passes, n_blocks),
            in_specs=[pl.BlockSpec((MOE_BLOCK, f), lambda p, m, be, nr: (m, 0)),
                      pl.BlockSpec((1, f, tn), wmap), pl.BlockSpec((1, 1, tn), wmap)],
            out_specs=pl.BlockSpec((MOE_BLOCK, tn // 2), lambda p, m, be, nr: (m, p)),
            scratch_shapes=[pltpu.VMEM((f, tn), BF16)]),
        out_shape=jax.ShapeDtypeStruct((rows, d // 2), jnp.uint32),
        compiler_params=_params(("arbitrary", "arbitrary"),
                                [((MOE_BLOCK, f), BF16), ((f, tn), F32), ((MOE_BLOCK, tn // 2), jnp.uint32)],
                                [((f, tn), BF16), ((MOE_BLOCK, tn), F32)]),
        name="moe_down",
    )(block_e, n_real, act, w_d, b_d)


def _combine_kernel(dcur_ref, dnext_ref, w_ref, x1_ref, g_ref, mod_ref, y_hbm, o_ref, buf, sem,
                    *, tokens, steps):
    step = pl.program_id(0)
    slot = step % 2

    def row_copy(slot_, k, t, row):
        return pltpu.make_async_copy(y_hbm.at[pl.ds(row, 1)], buf.at[slot_, k, pl.ds(t, 1)], sem.at[slot_])

    def start_tile(dest_ref, slot_):
        def body(t, carry):
            for k in range(TOP_K):
                row_copy(slot_, k, t, dest_ref[0, 0, t * TOP_K + k]).start()
            return carry
        lax.fori_loop(0, tokens, body, 0)

    @pl.when(step == 0)
    def _():
        start_tile(dcur_ref, 0)

    @pl.when(step + 1 < steps)
    def _():
        start_tile(dnext_ref, 1 - slot)

    def drain(t, carry):
        for k in range(TOP_K):
            row_copy(slot, k, 0, 0).wait()
        return carry

    lax.fori_loop(0, tokens, drain, 0)

    w = w_ref[...]
    acc_lo = acc_hi = None
    for k in range(TOP_K):
        word = buf[slot, k]
        lo = lax.bitcast_convert_type(word << 16, F32) * w[:, k:k + 1]
        hi = lax.bitcast_convert_type(word & jnp.uint32(0xFFFF0000), F32) * w[:, k:k + 1]
        acc_lo = lo if acc_lo is None else acc_lo + lo
        acc_hi = hi if acc_hi is None else acc_hi + hi
    quarter = acc_lo.shape[1] // 2
    acc = jnp.concatenate([acc_lo[:, :quarter], acc_hi[:, :quarter], acc_lo[:, quarter:], acc_hi[:, quarter:]],
                          axis=1)
    m = mod_ref[0]
    o_ref[...] = x1_ref[...] + m[5:6, :] * _rms(acc, g_ref[...])


def _combine(y_sorted, dest, top_w, x1, g_post, mod, seq):
    t, d = x1.shape
    dp = y_sorted.shape[1]
    tokens = _tile(seq, 128)
    steps = t // tokens
    row = lambda i: (i, 0)
    dest3 = dest.reshape(steps, 1, tokens * TOP_K)
    return pl.pallas_call(
        functools.partial(_combine_kernel, tokens=tokens, steps=steps),
        grid=(steps,),
        in_specs=[pl.BlockSpec((1, 1, tokens * TOP_K), lambda i: (i, 0, 0), memory_space=pltpu.SMEM),
                  pl.BlockSpec((1, 1, tokens * TOP_K), lambda i: (jnp.minimum(i + 1, steps - 1), 0, 0),
                               memory_space=pltpu.SMEM),
                  pl.BlockSpec((tokens, LANES), row),
                  pl.BlockSpec((tokens, d), row),
                  pl.BlockSpec((1, d), lambda i: (0, 0)),
                  pl.BlockSpec((1, 6, d), lambda i: (i * tokens // seq, 0, 0)),
                  pl.BlockSpec(memory_space=pl.ANY)],
        out_specs=pl.BlockSpec((tokens, d), row),
        out_shape=jax.ShapeDtypeStruct((t, d), F32),
        scratch_shapes=[pltpu.VMEM((2, TOP_K, tokens, dp), y_sorted.dtype), pltpu.SemaphoreType.DMA((2,))],
        compiler_params=_params(("arbitrary",), [((tokens, d), F32)] * 2,
                                [((2, TOP_K, tokens, dp), y_sorted.dtype)] + [((tokens, d), F32)] * 3),
        name="moe_combine",
    )(dest3, dest3, top_w, x1, g_post.reshape(1, d), mod, y_sorted)


def _moe(h2, top_idx, top_w, x1, g_post, mod, w_gu, b_gu, w_dn, b_dn, seq):
    t, d = x1.shape
    f = w_dn.shape[1]
    n_blocks = -(-(t * TOP_K) // MOE_BLOCK) + N_EXPERTS
    dest, block_e, n_real, last_blk = _routing(top_idx, n_blocks)
    xg = _dispatch(h2, dest, last_blk, n_real, n_blocks)
    w_g, w_l = _deinterleave(w_gu)
    b_g = b_gu[:, 0::2].reshape(N_EXPERTS, 1, f)
    b_l = b_gu[:, 1::2].reshape(N_EXPERTS, 1, f)
    act = _gate_up(xg, w_g, w_l, b_g, b_l, block_e, n_real, n_blocks)
    y_sorted = _down(act, w_dn, b_dn.reshape(N_EXPERTS, 1, d), block_e, n_real, n_blocks)
    return _combine(y_sorted, dest, top_w, x1, g_post, mod, seq)


def kernel(x, c, w_ada, b_ada, g_pre_mix, g_post_mix, g_pre_ffn, g_post_ffn, w_in, b_in, sinks, rel_bias,
           w_o_swa, w_o_moba, w_out, w_router, b_router, w_gate_up, b_gate_up, w_down, b_down):
    bsz, seq, d = x.shape
    depth = w_ada.shape[0]
    assert seq % MOBA_BLOCK == 0 and d % 1024 == 0
    x2 = x.reshape(bsz * seq, d)
    cos_t, sin_t = _rope_tables(seq)
    bias_tiles = _bias_tiles(rel_bias)
    for l in range(depth):
        mod = _ada_mod(c, w_ada[l], b_ada[l])
        h = _prenorm(x2, g_pre_mix[l], mod, seq, shift_row=0, scale_row=1)
        proj = _matmul(h, w_in[l].astype(BF16), b_in[l], BF16, 1024, 768, "in_proj")
        o_a = _swa_attention(proj, sinks[l], cos_t, sin_t, bsz, seq, d)
        o_b = _moba_attention(proj, _kmean(proj, bsz, seq, d), bias_tiles, bsz, seq, d)
        merged = _merge(o_a, o_b, w_o_swa[l].astype(BF16), w_o_moba[l].astype(BF16), proj, d)
        y = _matmul(merged, w_out[l].astype(BF16), jnp.zeros((d,), F32), F32, 1024, 1024, "out_proj")
        x1, h2, idx, wts = _post_mix(y, x2, g_post_mix[l], g_pre_ffn[l], mod, w_router[l], b_router[l], seq)
        x2 = _moe(h2, idx[:, :TOP_K], wts, x1, g_post_ffn[l], mod,
                  w_gate_up[l], b_gate_up[l], w_down[l], b_down[l], seq)
    return x2.reshape(bsz, seq, d)
```

```python
import functools
import math

import numpy as np
import jax
import jax.numpy as jnp
from jax import lax
from jax.experimental import pallas as pl
from jax.experimental.pallas import tpu as pltpu

F32 = jnp.float32
BF16 = jnp.bfloat16
I32 = jnp.int32

HEAD_DIM_SWA = 64
SWA_GROUP = 8
WINDOW = 128
ROPE_THETA = 150000.0
HEAD_DIM_MOBA = 128
MOBA_GROUP = 4
MOBA_BLOCK = 256
MOBA_TOPK = 3
REL_BUCKETS = 32
REL_MAX_DIST = 1024
N_EXPERTS = 32
TOP_K = 4
SWIGLU_LIMIT = 7.0
SWIGLU_ALPHA = 1.702
MOE_BLOCK = 256
RMS_EPS = 1e-6

LANES = 128
SUBLANES = 8
BF16_SUBLANES = 16
VMEM_BYTES_V7X = 64 * 1024 * 1024
VMEM_CAP = VMEM_BYTES_V7X - 8 * 1024 * 1024

NEG_INF = float("-inf")
LOG2E = 1.0 / math.log(2.0)
N_BIAS_TILES = (REL_MAX_DIST + MOBA_BLOCK - 1) // MOBA_BLOCK + 2


def _nbytes(shape, dtype):
    return int(np.prod(shape)) * jnp.dtype(dtype).itemsize


def _params(semantics, blocks, temps=()):
    need = 2 * sum(_nbytes(s, d) for s, d in blocks) + sum(_nbytes(s, d) for s, d in temps)
    need = max(need + need // 4, 16 * 1024 * 1024)
    return pltpu.CompilerParams(dimension_semantics=semantics, vmem_limit_bytes=min(need, VMEM_CAP))


def _tile(n, pref):
    t = min(n, pref)
    assert n % t == 0, (n, pref)
    return t


def _ada_kernel(c_ref, w_ref, b_ref, o_ref):
    c = c_ref[...]
    s = c * jax.nn.sigmoid(c)
    o_ref[...] = jnp.dot(s, w_ref[...], preferred_element_type=F32,
                         precision=lax.Precision.HIGHEST) + b_ref[...]


def _ada_mod(c, w, b):
    bsz, d = c.shape
    n = w.shape[1]
    rows = -(-bsz // SUBLANES) * SUBLANES
    cp = jnp.pad(c, ((0, rows - bsz), (0, 0)))
    tn = _tile(n, 512)
    out = pl.pallas_call(
        _ada_kernel,
        grid=(n // tn,),
        in_specs=[pl.BlockSpec((rows, d), lambda j: (0, 0)),
                  pl.BlockSpec((d, tn), lambda j: (0, j)),
                  pl.BlockSpec((1, tn), lambda j: (0, j))],
        out_specs=pl.BlockSpec((rows, tn), lambda j: (0, j)),
        out_shape=jax.ShapeDtypeStruct((rows, n), F32),
        compiler_params=_params(("arbitrary",), [((d, tn), F32), ((rows, d), F32)], [((d, tn), F32)]),
        name="ada_mod",
    )(cp, w, b.reshape(1, n))
    return out[:bsz].reshape(bsz, 6, d)


def _rms(x, g):
    return x * lax.rsqrt(jnp.mean(x * x, axis=-1, keepdims=True) + RMS_EPS) * g


def _prenorm_kernel(x_ref, g_ref, mod_ref, o_ref, *, shift_row, scale_row):
    m = mod_ref[0]
    y = _rms(x_ref[...], g_ref[...])
    o_ref[...] = (y * (1.0 + m[scale_row:scale_row + 1, :]) + m[shift_row:shift_row + 1, :]).astype(o_ref.dtype)


def _prenorm(x2, g, mod, seq, shift_row, scale_row):
    t, d = x2.shape
    tm = _tile(seq, 256)
    return pl.pallas_call(
        functools.partial(_prenorm_kernel, shift_row=shift_row, scale_row=scale_row),
        grid=(t // tm,),
        in_specs=[pl.BlockSpec((tm, d), lambda i: (i, 0)),
                  pl.BlockSpec((1, d), lambda i: (0, 0)),
                  pl.BlockSpec((1, 6, d), lambda i: (i * tm // seq, 0, 0))],
        out_specs=pl.BlockSpec((tm, d), lambda i: (i, 0)),
        out_shape=jax.ShapeDtypeStruct((t, d), BF16),
        compiler_params=_params(("parallel",), [((tm, d), F32), ((tm, d), BF16)], [((tm, d), F32)] * 2),
        name="prenorm",
    )(x2, g.reshape(1, d), mod)


def _mm_kernel(a_ref, w_ref, b_ref, o_ref):
    acc = jnp.dot(a_ref[...], w_ref[...], preferred_element_type=F32)
    o_ref[...] = (acc + b_ref[...]).astype(o_ref.dtype)


def _matmul(a, w, bias, out_dtype, tm, tn, name):
    m, k = a.shape
    n = w.shape[1]
    tm, tn = _tile(m, tm), _tile(n, tn)
    return pl.pallas_call(
        _mm_kernel,
        grid=(m // tm, n // tn),
        in_specs=[pl.BlockSpec((tm, k), lambda i, j: (i, 0)),
                  pl.BlockSpec((k, tn), lambda i, j: (0, j)),
                  pl.BlockSpec((1, tn), lambda i, j: (0, j))],
        out_specs=pl.BlockSpec((tm, tn), lambda i, j: (i, j)),
        out_shape=jax.ShapeDtypeStruct((m, n), out_dtype),
        compiler_params=_params(("parallel", "arbitrary"),
                                [((tm, k), a.dtype), ((k, tn), w.dtype), ((tm, tn), out_dtype)],
                                [((tm, tn), F32)]),
        name=name,
    )(a, w, bias.reshape(1, n).astype(F32))


def _rope_tables(seq):
    half = HEAD_DIM_SWA // 2
    inv = ROPE_THETA ** (-jnp.arange(half, dtype=F32) / half)
    ang = jnp.arange(seq, dtype=F32)[:, None] * inv[None, :]
    cos, sin = jnp.cos(ang), jnp.sin(ang)
    cos_t = jnp.concatenate([cos, cos, cos, cos], axis=-1)
    sin_t = jnp.concatenate([-sin, sin, -sin, sin], axis=-1)
    return cos_t, sin_t


def _swa_kernel(sinks_ref, q_ref, kp_ref, kc_ref, vp_ref, vc_ref, cq_ref, sq_ref, cp_ref, sp_ref, o_ref,
                *, n_kv):
    n = pl.program_id(1)
    w = WINDOW
    half = HEAD_DIM_SWA // 2
    lane = lax.broadcasted_iota(I32, (1, LANES), 1)
    first_half = (lane % HEAD_DIM_SWA) < half
    low_head = lane < HEAD_DIM_SWA

    def rope(xf, cos, sin):
        partner = jnp.where(first_half, pltpu.roll(xf, LANES - half, 1), pltpu.roll(xf, half, 1))
        return xf * cos + partner * sin

    cos_q, sin_q = cq_ref[...], sq_ref[...]
    cos_k = jnp.concatenate([cp_ref[...], cos_q], axis=0)
    sin_k = jnp.concatenate([sp_ref[...], sin_q], axis=0)
    ri = lax.broadcasted_iota(I32, (w, 2 * w), 0)
    cj = lax.broadcasted_iota(I32, (w, 2 * w), 1)
    diff = w + ri - cj
    mask = (diff >= 0) & (diff < w) & ((cj >= w) | (n > 0))
    pairs = SWA_GROUP // 2

    for g in range(n_kv):
        slab = g // 2
        cols = slice(slab * LANES, (slab + 1) * LANES)
        k2 = rope(jnp.concatenate([kp_ref[:, cols], kc_ref[:, cols]], axis=0).astype(F32), cos_k, sin_k)
        v2 = jnp.concatenate([vp_ref[:, cols], vc_ref[:, cols]], axis=0).astype(F32)
        k2s, v2s = pltpu.roll(k2, HEAD_DIM_SWA, 1), pltpu.roll(v2, HEAD_DIM_SWA, 1)
        if g % 2 == 0:
            k_lo, v_lo = jnp.where(low_head, k2, 0.0), jnp.where(low_head, v2, 0.0)
            k_hi, v_hi = jnp.where(low_head, 0.0, k2s), jnp.where(low_head, 0.0, v2s)
        else:
            k_lo, v_lo = jnp.where(low_head, k2s, 0.0), jnp.where(low_head, v2s, 0.0)
            k_hi, v_hi = jnp.where(low_head, 0.0, k2), jnp.where(low_head, 0.0, v2)
        kv = ((k_lo.astype(BF16), v_lo.astype(BF16)), (k_hi.astype(BF16), v_hi.astype(BF16)))
        for p in range(pairs):
            qs = g * pairs + p
            qcols = slice(qs * LANES, (qs + 1) * LANES)
            q = rope(q_ref[:, qcols].astype(F32), cos_q, sin_q) * (1.0 / math.sqrt(HEAD_DIM_SWA))
            q = q.astype(BF16)
            o = jnp.zeros((w, LANES), F32)
            for hh, (kk, vv) in enumerate(kv):
                s = lax.dot_general(q, kk, (((1,), (1,)), ((), ())), preferred_element_type=F32)
                s = jnp.where(mask, s, NEG_INF)
                sink = sinks_ref[2 * qs + hh]
                m = jnp.maximum(jnp.max(s, axis=-1, keepdims=True), sink)
                e = jnp.exp(s - m)
                den = jnp.sum(e, axis=-1, keepdims=True) + jnp.exp(sink - m)
                o = o + jnp.dot(e.astype(BF16), vv, preferred_element_type=F32) / den
            o_ref[:, qcols] = o.astype(o_ref.dtype)


def _swa_attention(proj, sinks, cos_t, sin_t, bsz, seq, d):
    q_w, kv_w = d // 2, d // 16
    n_kv = kv_w // HEAD_DIM_SWA
    assert n_kv % 2 == 0 and q_w % kv_w == 0
    nb = seq // WINDOW
    k_col, v_col = q_w // kv_w, q_w // kv_w + 1
    cur = lambda b, n: b * nb + n
    prev = lambda b, n: b * nb + jnp.maximum(n - 1, 0)
    return pl.pallas_call(
        functools.partial(_swa_kernel, n_kv=n_kv),
        grid=(bsz, nb),
        in_specs=[pl.BlockSpec(memory_space=pltpu.SMEM),
                  pl.BlockSpec((WINDOW, q_w), lambda b, n: (cur(b, n), 0)),
                  pl.BlockSpec((WINDOW, kv_w), lambda b, n: (prev(b, n), k_col)),
                  pl.BlockSpec((WINDOW, kv_w), lambda b, n: (cur(b, n), k_col)),
                  pl.BlockSpec((WINDOW, kv_w), lambda b, n: (prev(b, n), v_col)),
                  pl.BlockSpec((WINDOW, kv_w), lambda b, n: (cur(b, n), v_col)),
                  pl.BlockSpec((WINDOW, LANES), lambda b, n: (n, 0)),
                  pl.BlockSpec((WINDOW, LANES), lambda b, n: (n, 0)),
                  pl.BlockSpec((WINDOW, LANES), lambda b, n: (jnp.maximum(n - 1, 0), 0)),
                  pl.BlockSpec((WINDOW, LANES), lambda b, n: (jnp.maximum(n - 1, 0), 0))],
        out_specs=pl.BlockSpec((WINDOW, q_w), lambda b, n: (cur(b, n), 0)),
        out_shape=jax.ShapeDtypeStruct((bsz * seq, q_w), BF16),
        compiler_params=_params(("parallel", "arbitrary"),
                                [((WINDOW, q_w), BF16)] * 2 + [((WINDOW, kv_w), BF16)] * 4,
                                [((2 * WINDOW, 2 * WINDOW), F32)] * 16),
        name="swa_attn",
    )(sinks, proj, proj, proj, proj, proj, cos_t, sin_t, cos_t, sin_t)


def _t5_thresholds():
    exact = REL_BUCKETS // 2
    d = np.arange(exact, 2 * REL_MAX_DIST, dtype=np.float64)
    large = exact + np.floor(np.log(d / exact) / math.log(REL_MAX_DIST / exact) * (REL_BUCKETS - exact)).astype(np.int64)
    large = np.minimum(large, REL_BUCKETS - 1)
    return [int(exact + np.argmax(large >= b)) for b in range(exact + 1, REL_BUCKETS)]


def _bias_tile_kernel(table_ref, o_ref):
    dd = pl.program_id(0)
    h = pl.program_id(1)
    exact = REL_BUCKETS // 2
    key = lax.broadcasted_iota(I32, (MOBA_BLOCK, MOBA_BLOCK), 0)
    qry = lax.broadcasted_iota(I32, (MOBA_BLOCK, MOBA_BLOCK), 1)
    dist = jnp.maximum(dd * MOBA_BLOCK + qry - key, 0)
    bucket = jnp.minimum(dist, exact)
    for thr in _t5_thresholds():
        bucket = bucket + (dist >= thr).astype(I32)
    val = jnp.full((MOBA_BLOCK, MOBA_BLOCK), table_ref[REL_BUCKETS - 1, h], F32)
    for b in range(REL_BUCKETS - 2, -1, -1):
        val = jnp.where(bucket == b, table_ref[b, h], val)
    o_ref[0, 0] = val * LOG2E


def _bias_tiles(rel_bias):
    n_heads = rel_bias.shape[1]
    return pl.pallas_call(
        _bias_tile_kernel,
        grid=(N_BIAS_TILES, n_heads),
        in_specs=[pl.BlockSpec(memory_space=pltpu.SMEM)],
        out_specs=pl.BlockSpec((1, 1, MOBA_BLOCK, MOBA_BLOCK), lambda dd, h: (h, dd, 0, 0)),
        out_shape=jax.ShapeDtypeStruct((n_heads, N_BIAS_TILES, MOBA_BLOCK, MOBA_BLOCK), F32),
        compiler_params=_params(("arbitrary", "arbitrary"), [((MOBA_BLOCK, MOBA_BLOCK), F32)],
                                [((MOBA_BLOCK, MOBA_BLOCK), F32)] * 4),
        name="moba_bias_tiles",
    )(rel_bias)


def _kmean_kernel(k_ref, o_ref, *, nblk):
    for j in range(nblk):
        rows = k_ref[j * MOBA_BLOCK:(j + 1) * MOBA_BLOCK, :].astype(F32)
        o_ref[j:j + 1, :] = jnp.mean(rows, axis=0, keepdims=True)


def _kmean(proj, bsz, seq, d):
    kv_w = d // 8
    nblk = seq // MOBA_BLOCK
    col = (9 * d // 8) // kv_w
    return pl.pallas_call(
        functools.partial(_kmean_kernel, nblk=nblk),
        grid=(bsz,),
        in_specs=[pl.BlockSpec((seq, kv_w), lambda b: (b, col))],
        out_specs=pl.BlockSpec((nblk, kv_w), lambda b: (b, 0)),
        out_shape=jax.ShapeDtypeStruct((bsz * nblk, kv_w), F32),
        compiler_params=_params(("parallel",), [((seq, kv_w), BF16)], [((MOBA_BLOCK, kv_w), F32)] * 2),
        name="moba_kmean",
    )(proj)


def _deinterleave_perm():
    perm = np.zeros((2 * LANES, 2 * LANES), np.float32)
    perm[2 * np.arange(LANES), np.arange(LANES)] = 1.0
    perm[2 * np.arange(LANES) + 1, LANES + np.arange(LANES)] = 1.0
    return jnp.asarray(perm, BF16)


def _deinterleave_block(w_ref, p_ref, wg_ref, wl_ref):
    perm = p_ref[...]
    group = perm.shape[0]
    half = group // 2
    for c in range(w_ref.shape[2] // group):
        t = jnp.dot(w_ref[0, :, c * group:(c + 1) * group].astype(BF16), perm, preferred_element_type=F32)
        wg_ref[0, :, c * half:(c + 1) * half] = t[:, :half].astype(BF16)
        wl_ref[0, :, c * half:(c + 1) * half] = t[:, half:].astype(BF16)


def _moba_kernel(q_ref, k_ref, v_ref, km_ref, bias_ref, w_ref, p_ref, o_ref, wg_ref, wl_ref, vt_ref, *, nblk):
    i = pl.program_id(2)
    mb, dh = MOBA_BLOCK, HEAD_DIM_MOBA
    scale = 1.0 / math.sqrt(dh)
    nt = (((1,), (1,)), ((), ()))
    blk = lax.broadcasted_iota(I32, (nblk, mb), 0)
    key = lax.broadcasted_iota(I32, (mb, mb), 0)
    qry = lax.broadcasted_iota(I32, (mb, mb), 1)
    causal = qry >= key

    @pl.when(i == 0)
    def _():
        for j in range(nblk):
            vt_ref[:dh, j * mb:(j + 1) * mb] = v_ref[j * mb:(j + 1) * mb, :].astype(F32).T.astype(BF16)
        vt_ref[dh:, :] = jnp.ones((vt_ref.shape[0] - dh, vt_ref.shape[1]), BF16)

    kmean = km_ref[...]
    km_hi = kmean.astype(BF16)
    rest = kmean - km_hi.astype(F32)
    km_mid = rest.astype(BF16)
    km_lo = (rest - km_mid.astype(F32)).astype(BF16)
    own = pl.multiple_of(i * mb, mb)
    k_own, vt_own = k_ref[pl.ds(own, mb), :], vt_ref[:, pl.ds(own, mb)]

    qs = [q_ref[:, hh * dh:(hh + 1) * dh] for hh in range(MOBA_GROUP)]
    gates = [lax.dot_general(km_hi, q, nt, preferred_element_type=F32)
             + lax.dot_general(km_mid, q, nt, preferred_element_type=F32)
             + lax.dot_general(km_lo, q, nt, preferred_element_type=F32) for q in qs]
    own_scores = [lax.dot_general(k_own, q, nt, preferred_element_type=F32) for q in qs]
    picks, init = [], []
    for hh in range(MOBA_GROUP):
        gate = jnp.where(blk < i, gates[hh], NEG_INF)
        pick = []
        for t in range(MOBA_TOPK):
            gmax = jnp.max(gate, axis=0, keepdims=True)
            idx = jnp.min(jnp.where(gate == gmax, blk, nblk), axis=0, keepdims=True)
            pick.append(jnp.where(t < i, idx, -1))
            gate = jnp.where(blk == idx, NEG_INF, gate)

        s = jnp.where(causal, own_scores[hh] * (scale * LOG2E) + bias_ref[hh, 0], NEG_INF)
        m0 = jnp.max(s, axis=0, keepdims=True)
        a0 = jnp.dot(vt_own, jnp.exp2(s - m0).astype(BF16), preferred_element_type=F32)
        picks.append(pick)
        init.append((m0, a0))

    def past(jj, carry):
        ja = 2 * jj
        start = pl.multiple_of(ja * mb, mb)
        kj, vtj = k_ref[pl.ds(start, 2 * mb), :], vt_ref[:, pl.ds(start, 2 * mb)]
        tile_a = jnp.minimum(i - ja, N_BIAS_TILES - 1)
        tile_b = jnp.minimum(i - ja - 1, N_BIAS_TILES - 1)
        scores = [lax.dot_general(kj, qs[hh], nt, preferred_element_type=F32) for hh in range(MOBA_GROUP)]
        out = []
        for hh in range(MOBA_GROUP):
            m, acc = carry[hh]
            p0, p1, p2 = picks[hh]
            sa = scores[hh][:mb] * (scale * LOG2E) + bias_ref[hh, tile_a]
            sb = scores[hh][mb:] * (scale * LOG2E) + bias_ref[hh, tile_b]
            sa = jnp.where((p0 == ja) | (p1 == ja) | (p2 == ja), sa, NEG_INF)
            sb = jnp.where((p0 == ja + 1) | (p1 == ja + 1) | (p2 == ja + 1), sb, NEG_INF)
            m_new = jnp.maximum(m, jnp.maximum(jnp.max(sa, axis=0, keepdims=True),
                                               jnp.max(sb, axis=0, keepdims=True)))
            e = jnp.concatenate([jnp.exp2(sa - m_new), jnp.exp2(sb - m_new)], axis=0).astype(BF16)
            acc = jnp.exp2(m - m_new) * acc + jnp.dot(vtj, e, preferred_element_type=F32)
            out.append((m_new, acc))
        return tuple(out)

    final = lax.fori_loop(0, (i + 1) // 2, past, tuple(init))
    for hh in range(MOBA_GROUP):
        _, acc = final[hh]
        o_ref[:, hh * dh:(hh + 1) * dh] = (acc[:dh] / acc[dh:dh + 1]).T.astype(o_ref.dtype)
    _deinterleave_block(w_ref, p_ref, wg_ref, wl_ref)


def _moba_attention(proj, kmean, bias_tiles, w_gu, bsz, seq, d):
    gw = MOBA_GROUP * HEAD_DIM_MOBA
    n_kv = (d // 8) // HEAD_DIM_MOBA
    nblk = seq // MOBA_BLOCK
    q_off, k_off, v_off = 5 * d // 8, 9 * d // 8, 10 * d // 8
    assert q_off % gw == 0 and MOBA_TOPK == 3
    qc, kc, vc = q_off // gw, k_off // HEAD_DIM_MOBA, v_off // HEAD_DIM_MOBA
    tile_shape = (MOBA_GROUP, N_BIAS_TILES, MOBA_BLOCK, MOBA_BLOCK)
    n_exp, wd, f2 = w_gu.shape
    steps = n_kv * bsz * nblk
    tk = n_exp * wd // steps
    assert n_exp * wd % steps == 0 and wd % tk == 0 and tk % BF16_SUBLANES == 0
    per_exp = wd // tk
    wmap = lambda g, b, i: (((g * bsz + b) * nblk + i) // per_exp, ((g * bsz + b) * nblk + i) % per_exp, 0)
    w_out = jax.ShapeDtypeStruct((n_exp, wd, f2 // 2), BF16)
    return pl.pallas_call(
        functools.partial(_moba_kernel, nblk=nblk),
        grid=(n_kv, bsz, nblk),
        in_specs=[pl.BlockSpec((MOBA_BLOCK, gw), lambda g, b, i: (b * nblk + i, qc + g)),
                  pl.BlockSpec((seq, HEAD_DIM_MOBA), lambda g, b, i: (b, kc + g)),
                  pl.BlockSpec((seq, HEAD_DIM_MOBA), lambda g, b, i: (b, vc + g)),
                  pl.BlockSpec((nblk, HEAD_DIM_MOBA), lambda g, b, i: (b, g)),
                  pl.BlockSpec(tile_shape, lambda g, b, i: (g, 0, 0, 0)),
                  pl.BlockSpec((1, tk, f2), wmap),
                  pl.BlockSpec((2 * LANES, 2 * LANES), lambda g, b, i: (0, 0))],
        out_specs=[pl.BlockSpec((MOBA_BLOCK, gw), lambda g, b, i: (b * nblk + i, g)),
                   pl.BlockSpec((1, tk, f2 // 2), wmap), pl.BlockSpec((1, tk, f2 // 2), wmap)],
        out_shape=[jax.ShapeDtypeStruct((bsz * seq, d // 2), BF16), w_out, w_out],
        scratch_shapes=[pltpu.VMEM((HEAD_DIM_MOBA + BF16_SUBLANES, seq), BF16)],
        compiler_params=_params(("parallel", "parallel", "arbitrary"),
                                [(tile_shape, F32), ((seq, HEAD_DIM_MOBA), BF16), ((seq, HEAD_DIM_MOBA), BF16),
                                 ((MOBA_BLOCK, gw), BF16), ((MOBA_BLOCK, gw), BF16),
                                 ((tk, f2), F32), ((tk, f2), BF16)],
                                [((2 * MOBA_BLOCK, MOBA_BLOCK), F32)] * 8
                                + [((HEAD_DIM_MOBA + BF16_SUBLANES, seq), BF16)] + [((tk, 2 * LANES), F32)] * 4),
        name="moba_attn",
    )(proj, proj, proj, kmean, bias_tiles, w_gu, _deinterleave_perm())


def _merge_kernel(oa_ref, ob_ref, wa_ref, wb_ref, ga_ref, gb_ref, o_ref):
    ya = jnp.dot(oa_ref[...], wa_ref[...], preferred_element_type=F32)
    yb = jnp.dot(ob_ref[...], wb_ref[...], preferred_element_type=F32)
    merged = jax.nn.sigmoid(ga_ref[...].astype(F32)) * ya + jax.nn.sigmoid(gb_ref[...].astype(F32)) * yb
    o_ref[...] = merged.astype(o_ref.dtype)


def _merge(o_a, o_b, w_a, w_b, proj, d):
    t, kdim = o_a.shape
    tm, tn = _tile(t, 1024), _tile(d, 512)
    ga_col, gb_col = (11 * d // 8) // tn, (19 * d // 8) // tn
    assert (11 * d // 8) % tn == 0 and (19 * d // 8) % tn == 0
    return pl.pallas_call(
        _merge_kernel,
        grid=(t // tm, d // tn),
        in_specs=[pl.BlockSpec((tm, kdim), lambda i, j: (i, 0)),
                  pl.BlockSpec((tm, kdim), lambda i, j: (i, 0)),
                  pl.BlockSpec((kdim, tn), lambda i, j: (0, j)),
                  pl.BlockSpec((kdim, tn), lambda i, j: (0, j)),
                  pl.BlockSpec((tm, tn), lambda i, j: (i, ga_col + j)),
                  pl.BlockSpec((tm, tn), lambda i, j: (i, gb_col + j))],
        out_specs=pl.BlockSpec((tm, tn), lambda i, j: (i, j)),
        out_shape=jax.ShapeDtypeStruct((t, d), BF16),
        compiler_params=_params(("parallel", "arbitrary"),
                                [((tm, kdim), BF16)] * 2 + [((kdim, tn), BF16)] * 2 + [((tm, tn), BF16)] * 3,
                                [((tm, tn), F32)] * 3),
        name="merge",
    )(o_a, o_b, w_a, w_b, proj, proj)


def _post_mix_kernel(y_ref, x_ref, gpost_ref, gpre_ref, mod_ref, wr_ref, br_ref,
                     x1_ref, h2_ref, idx_ref, wts_ref):
    m = mod_ref[0]
    x1 = x_ref[...] + m[2:3, :] * _rms(y_ref[...], gpost_ref[...])
    x1_ref[...] = x1
    h2 = _rms(x1, gpre_ref[...]) * (1.0 + m[4:5, :]) + m[3:4, :]
    half = h2.shape[1] // 2
    h2_ref[...] = _pack_bf16_pair(h2[:, :half], h2[:, half:])
    logits = jnp.dot(h2, wr_ref[...], preferred_element_type=F32,
                     precision=lax.Precision.HIGHEST) + br_ref[...]
    lane = lax.broadcasted_iota(I32, logits.shape, 1)
    logits = jnp.where(lane < N_EXPERTS, logits, NEG_INF)
    idx_out = jnp.zeros(logits.shape, I32)
    val_out = jnp.zeros(logits.shape, F32)
    top = None
    den = jnp.zeros((logits.shape[0], 1), F32)
    for k in range(TOP_K):
        vmax = jnp.max(logits, axis=-1, keepdims=True)
        idx = jnp.min(jnp.where(logits == vmax, lane, LANES), axis=-1, keepdims=True)
        top = vmax if top is None else top
        e = jnp.exp(vmax - top)
        den = den + e
        idx_out = jnp.where(lane == k, idx, idx_out)
        val_out = jnp.where(lane == k, e, val_out)
        logits = jnp.where(lane == idx, NEG_INF, logits)
    idx_ref[...] = idx_out
    wts_ref[...] = val_out / den


def _post_mix(y, x2, g_post, g_pre, mod, w_router, b_router, seq):
    t, d = x2.shape
    tm = _tile(seq, 256)
    wr = jnp.pad(w_router, ((0, 0), (0, LANES - N_EXPERTS)))
    br = jnp.pad(b_router, (0, LANES - N_EXPERTS)).reshape(1, LANES)
    row = lambda i: (i, 0)
    const = lambda i: (0, 0)
    return pl.pallas_call(
        _post_mix_kernel,
        grid=(t // tm,),
        in_specs=[pl.BlockSpec((tm, d), row), pl.BlockSpec((tm, d), row),
                  pl.BlockSpec((1, d), const), pl.BlockSpec((1, d), const),
                  pl.BlockSpec((1, 6, d), lambda i: (i * tm // seq, 0, 0)),
                  pl.BlockSpec((d, LANES), const), pl.BlockSpec((1, LANES), const)],
        out_specs=[pl.BlockSpec((tm, d), row), pl.BlockSpec((tm, d // 2), row),
                   pl.BlockSpec((tm, LANES), row), pl.BlockSpec((tm, LANES), row)],
        out_shape=[jax.ShapeDtypeStruct((t, d), F32), jax.ShapeDtypeStruct((t, d // 2), jnp.uint32),
                   jax.ShapeDtypeStruct((t, LANES), I32), jax.ShapeDtypeStruct((t, LANES), F32)],
        compiler_params=_params(("parallel",), [((tm, d), F32)] * 4 + [((d, LANES), F32)],
                                [((tm, d), F32)] * 3),
        name="post_mix_router",
    )(y, x2, g_post.reshape(1, d), g_pre.reshape(1, d), mod, wr, br)


def _routing(top_idx, n_blocks):
    flat_e = top_idx.reshape(-1)
    onehot = (flat_e[:, None] == jnp.arange(N_EXPERTS, dtype=I32)[None, :]).astype(I32)
    csum = jnp.cumsum(onehot, axis=0)
    counts = csum[-1]
    rank = jnp.sum(csum * onehot, axis=1) - 1
    padded = (counts + MOE_BLOCK - 1) // MOE_BLOCK * MOE_BLOCK
    pad_end = jnp.cumsum(padded)
    pad_start = pad_end - padded
    dest = jnp.sum(onehot * pad_start[None, :], axis=1) + rank
    n_real = (pad_end[-1] // MOE_BLOCK).astype(I32)
    blk_start = jnp.arange(n_blocks, dtype=I32) * MOE_BLOCK
    block_e = jnp.minimum(jnp.sum((pad_end[None, :] <= blk_start[:, None]).astype(I32), axis=1), N_EXPERTS - 1)
    last_blk = jnp.where(counts > 0, pad_end // MOE_BLOCK - 1, -1).astype(I32)
    return dest.astype(I32), block_e.astype(I32), n_real.reshape(1), last_blk


def _pack_bf16_pair(lo, hi):
    lo_bits = lax.bitcast_convert_type(lo.astype(BF16).astype(F32), jnp.uint32) >> 16
    hi_bits = lax.bitcast_convert_type(hi.astype(BF16).astype(F32), jnp.uint32) & jnp.uint32(0xFFFF0000)
    return hi_bits | lo_bits


def _unpack_bf16_pair(p):
    lo = lax.bitcast_convert_type(p << 16, F32).astype(BF16)
    hi = lax.bitcast_convert_type(p & jnp.uint32(0xFFFF0000), F32).astype(BF16)
    return lo, hi


def _dispatch_kernel(last_ref, nr_ref, dest_ref, h_ref, xg_hbm, zeros, zsem, sem, *, tokens, n_blocks):
    def zero_block(m):
        rows = pl.ds(pl.multiple_of(m * MOE_BLOCK, MOE_BLOCK), MOE_BLOCK)
        return pltpu.make_async_copy(zeros, xg_hbm.at[rows], zsem)

    @pl.when(pl.program_id(0) == 0)
    def _():
        zeros[...] = jnp.zeros(zeros.shape, zeros.dtype)
        for phase in ("start", "wait"):
            for e in range(N_EXPERTS):
                @pl.when(last_ref[e] >= 0)
                def _(e=e, phase=phase):
                    getattr(zero_block(last_ref[e]), phase)()

            def tail(m, carry, phase=phase):
                getattr(zero_block(m), phase)()
                return carry

            lax.fori_loop(nr_ref[0], n_blocks, tail, 0)

    def row_copy(t, row):
        return pltpu.make_async_copy(h_ref.at[pl.ds(t, 1)], xg_hbm.at[pl.ds(row, 1)], sem)

    def issue(t, carry):
        for k in range(TOP_K):
            row_copy(t, dest_ref[0, 0, t * TOP_K + k]).start()
        return carry

    lax.fori_loop(0, tokens, issue, 0)

    def drain(t, carry):
        for k in range(TOP_K):
            row_copy(0, 0).wait()
        return carry

    lax.fori_loop(0, tokens, drain, 0)


def _dispatch(h2p, dest, last_blk, n_real, n_blocks):
    t, dp = h2p.shape
    tokens = _tile(t, 256)
    steps = t // tokens
    return pl.pallas_call(
        functools.partial(_dispatch_kernel, tokens=tokens, n_blocks=n_blocks),
        grid_spec=pltpu.PrefetchScalarGridSpec(
            num_scalar_prefetch=2, grid=(steps,),
            in_specs=[pl.BlockSpec((1, 1, tokens * TOP_K), lambda i, last, nr: (i, 0, 0), memory_space=pltpu.SMEM),
                      pl.BlockSpec((tokens, dp), lambda i, last, nr: (i, 0))],
            out_specs=pl.BlockSpec(memory_space=pl.ANY),
            scratch_shapes=[pltpu.VMEM((MOE_BLOCK, dp), h2p.dtype),
                            pltpu.SemaphoreType.DMA(()), pltpu.SemaphoreType.DMA(())]),
        out_shape=jax.ShapeDtypeStruct((n_blocks * MOE_BLOCK, dp), h2p.dtype),
        compiler_params=_params(("arbitrary",), [((tokens, dp), h2p.dtype)], [((MOE_BLOCK, dp), h2p.dtype)]),
        name="moe_dispatch",
    )(last_blk, n_real, dest.reshape(steps, 1, tokens * TOP_K), h2p)


def _gate_up_kernel(be_ref, nr_ref, x_ref, wg_ref, wl_ref, bg_ref, bl_ref, o_ref):
    real = pl.program_id(1) < nr_ref[0]

    @pl.when(real)
    def _():
        lo, hi = _unpack_bf16_pair(x_ref[...])
        half = lo.shape[1]

        def proj(w_ref, b_ref):
            return (jnp.dot(lo, w_ref[0, :half, :], preferred_element_type=F32)
                    + jnp.dot(hi, w_ref[0, half:, :], preferred_element_type=F32) + b_ref[0])

        glu = jnp.minimum(proj(wg_ref, bg_ref), SWIGLU_LIMIT)
        lin = jnp.clip(proj(wl_ref, bl_ref), -SWIGLU_LIMIT, SWIGLU_LIMIT)
        o_ref[...] = (glu * jax.nn.sigmoid(SWIGLU_ALPHA * glu) * (lin + 1.0)).astype(o_ref.dtype)

    @pl.when(jnp.logical_not(real))
    def _():
        o_ref[...] = jnp.zeros(o_ref.shape, o_ref.dtype)


def _gate_up(xg, w_g, w_l, b_g, b_l, block_e, n_real, n_blocks):
    rows, dp = xg.shape
    d, f = w_g.shape[1], w_g.shape[2]
    passes = 2
    tf = f // passes
    blk = lambda p, m, be, nr: jnp.minimum(m, nr[0] - 1)
    wmap = lambda p, m, be, nr: (be[blk(p, m, be, nr)], 0, p)
    return pl.pallas_call(
        _gate_up_kernel,
        grid_spec=pltpu.PrefetchScalarGridSpec(
            num_scalar_prefetch=2, grid=(passes, n_blocks),
            in_specs=[pl.BlockSpec((MOE_BLOCK, dp), lambda p, m, be, nr: (blk(p, m, be, nr), 0)),
                      pl.BlockSpec((1, d, tf), wmap), pl.BlockSpec((1, d, tf), wmap),
                      pl.BlockSpec((1, 1, tf), wmap), pl.BlockSpec((1, 1, tf), wmap)],
            out_specs=pl.BlockSpec((MOE_BLOCK, tf), lambda p, m, be, nr: (m, p))),
        out_shape=jax.ShapeDtypeStruct((rows, f), BF16),
        compiler_params=_params(("arbitrary", "arbitrary"),
                                [((MOE_BLOCK, dp), xg.dtype), ((d, tf), BF16), ((d, tf), BF16),
                                 ((MOE_BLOCK, tf), BF16)],
                                [((MOE_BLOCK, d), BF16)] + [((MOE_BLOCK, tf), F32)] * 4),
        name="moe_gate_up",
    )(block_e, n_real, xg, w_g, w_l, b_g, b_l)


def _down_kernel(be_ref, nr_ref, a_ref, w_ref, b_ref, o_ref, wb_ref):
    m = pl.program_id(1)
    real = m < nr_ref[0]
    fresh = jnp.logical_or(m == 0, be_ref[m] != be_ref[jnp.maximum(m - 1, 0)])

    @pl.when(jnp.logical_and(real, fresh))
    def _():
        wb_ref[...] = w_ref[0].astype(BF16)

    @pl.when(real)
    def _():
        y = jnp.dot(a_ref[...], wb_ref[...], preferred_element_type=F32) + b_ref[0]
        quarter = y.shape[1] // 2
        o_ref[...] = _pack_bf16_pair(y[:, :quarter], y[:, quarter:])

    @pl.when(jnp.logical_not(real))
    def _():
        o_ref[...] = jnp.zeros(o_ref.shape, o_ref.dtype)


def _down(act, w_d, b_d, block_e, n_real, n_blocks):
    rows, f = act.shape
    d = w_d.shape[2]
    passes = 2
    tn = d // passes
    blk = lambda p, m, be, nr: jnp.minimum(m, nr[0] - 1)
    wmap = lambda p, m, be, nr: (be[blk(p, m, be, nr)], 0, p)
    return pl.pallas_call(
        _down_kernel,
        grid_spec=pltpu.PrefetchScalarGridSpec(
            num_scalar_prefetch=2, grid=(passes, n_blocks),
            in_specs=[pl.BlockSpec((MOE_BLOCK, f), lambda p, m, be, nr: (m, 0)),
                      pl.BlockSpec((1, f, tn), wmap), pl.BlockSpec((1, 1, tn), wmap)],
            out_specs=pl.BlockSpec((MOE_BLOCK, tn // 2), lambda p, m, be, nr: (m, p)),
            scratch_shapes=[pltpu.VMEM((f, tn), BF16)]),
        out_shape=jax.ShapeDtypeStruct((rows, d // 2), jnp.uint32),
        compiler_params=_params(("arbitrary", "arbitrary"),
                                [((MOE_BLOCK, f), BF16), ((f, tn), F32), ((MOE_BLOCK, tn // 2), jnp.uint32)],
                                [((f, tn), BF16), ((MOE_BLOCK, tn), F32)]),
        name="moe_down",
    )(block_e, n_real, act, w_d, b_d)


def _combine_kernel(dcur_ref, dnext_ref, w_ref, x1_ref, g_ref, mod_ref, y_hbm, o_ref, buf, sem,
                    *, tokens, steps):
    step = pl.program_id(0)
    slot = step % 2

    def row_copy(slot_, k, t, row):
        return pltpu.make_async_copy(y_hbm.at[pl.ds(row, 1)], buf.at[slot_, k, pl.ds(t, 1)], sem.at[slot_])

    def start_tile(dest_ref, slot_):
        def body(t, carry):
            for k in range(TOP_K):
                row_copy(slot_, k, t, dest_ref[0, 0, t * TOP_K + k]).start()
            return carry
        lax.fori_loop(0, tokens, body, 0)

    @pl.when(step == 0)
    def _():
        start_tile(dcur_ref, 0)

    @pl.when(step + 1 < steps)
    def _():
        start_tile(dnext_ref, 1 - slot)

    def drain(t, carry):
        for k in range(TOP_K):
            row_copy(slot, k, 0, 0).wait()
        return carry

    lax.fori_loop(0, tokens, drain, 0)

    w = w_ref[...]
    acc_lo = acc_hi = None
    for k in range(TOP_K):
        word = buf[slot, k]
        lo = lax.bitcast_convert_type(word << 16, F32) * w[:, k:k + 1]
        hi = lax.bitcast_convert_type(word & jnp.uint32(0xFFFF0000), F32) * w[:, k:k + 1]
        acc_lo = lo if acc_lo is None else acc_lo + lo
        acc_hi = hi if acc_hi is None else acc_hi + hi
    quarter = acc_lo.shape[1] // 2
    acc = jnp.concatenate([acc_lo[:, :quarter], acc_hi[:, :quarter], acc_lo[:, quarter:], acc_hi[:, quarter:]],
                          axis=1)
    m = mod_ref[0]
    o_ref[...] = x1_ref[...] + m[5:6, :] * _rms(acc, g_ref[...])


def _combine(y_sorted, dest, top_w, x1, g_post, mod, seq):
    t, d = x1.shape
    dp = y_sorted.shape[1]
    tokens = _tile(seq, 128)
    steps = t // tokens
    row = lambda i: (i, 0)
    dest3 = dest.reshape(steps, 1, tokens * TOP_K)
    return pl.pallas_call(
        functools.partial(_combine_kernel, tokens=tokens, steps=steps),
        grid=(steps,),
        in_specs=[pl.BlockSpec((1, 1, tokens * TOP_K), lambda i: (i, 0, 0), memory_space=pltpu.SMEM),
                  pl.BlockSpec((1, 1, tokens * TOP_K), lambda i: (jnp.minimum(i + 1, steps - 1), 0, 0),
                               memory_space=pltpu.SMEM),
                  pl.BlockSpec((tokens, LANES), row),
                  pl.BlockSpec((tokens, d), row),
                  pl.BlockSpec((1, d), lambda i: (0, 0)),
                  pl.BlockSpec((1, 6, d), lambda i: (i * tokens // seq, 0, 0)),
                  pl.BlockSpec(memory_space=pl.ANY)],
        out_specs=pl.BlockSpec((tokens, d), row),
        out_shape=jax.ShapeDtypeStruct((t, d), F32),
        scratch_shapes=[pltpu.VMEM((2, TOP_K, tokens, dp), y_sorted.dtype), pltpu.SemaphoreType.DMA((2,))],
        compiler_params=_params(("arbitrary",), [((tokens, d), F32)] * 2,
                                [((2, TOP_K, tokens, dp), y_sorted.dtype)] + [((tokens, d), F32)] * 3),
        name="moe_combine",
    )(dest3, dest3, top_w, x1, g_post.reshape(1, d), mod, y_sorted)


def _moe(h2, top_idx, top_w, x1, g_post, mod, w_g, w_l, b_gu, w_dn, b_dn, seq):
    t, d = x1.shape
    f = w_dn.shape[1]
    n_blocks = -(-(t * TOP_K) // MOE_BLOCK) + N_EXPERTS
    dest, block_e, n_real, last_blk = _routing(top_idx, n_blocks)
    xg = _dispatch(h2, dest, last_blk, n_real, n_blocks)
    b_g = b_gu[:, 0::2].reshape(N_EXPERTS, 1, f)
    b_l = b_gu[:, 1::2].reshape(N_EXPERTS, 1, f)
    act = _gate_up(xg, w_g, w_l, b_g, b_l, block_e, n_real, n_blocks)
    y_sorted = _down(act, w_dn, b_dn.reshape(N_EXPERTS, 1, d), block_e, n_real, n_blocks)
    return _combine(y_sorted, dest, top_w, x1, g_post, mod, seq)


def kernel(x, c, w_ada, b_ada, g_pre_mix, g_post_mix, g_pre_ffn, g_post_ffn, w_in, b_in, sinks, rel_bias,
           w_o_swa, w_o_moba, w_out, w_router, b_router, w_gate_up, b_gate_up, w_down, b_down):
    bsz, seq, d = x.shape
    depth = w_ada.shape[0]
    assert seq % MOBA_BLOCK == 0 and d % 1024 == 0
    x2 = x.reshape(bsz * seq, d)
    cos_t, sin_t = _rope_tables(seq)
    bias_tiles = _bias_tiles(rel_bias)
    for l in range(depth):
        mod = _ada_mod(c, w_ada[l], b_ada[l])
        h = _prenorm(x2, g_pre_mix[l], mod, seq, shift_row=0, scale_row=1)
        proj = _matmul(h, w_in[l].astype(BF16), b_in[l], BF16, 1024, 768, "in_proj")
        o_a = _swa_attention(proj, sinks[l], cos_t, sin_t, bsz, seq, d)
        o_b, w_g, w_l = _moba_attention(proj, _kmean(proj, bsz, seq, d), bias_tiles, w_gate_up[l], bsz, seq, d)
        merged = _merge(o_a, o_b, w_o_swa[l].astype(BF16), w_o_moba[l].astype(BF16), proj, d)
        y = _matmul(merged, w_out[l].astype(BF16), jnp.zeros((d,), F32), F32, 1024, 1024, "out_proj")
        x1, h2, idx, wts = _post_mix(y, x2, g_post_mix[l], g_pre_ffn[l], mod, w_router[l], b_router[l], seq)
        x2 = _moe(h2, idx[:, :TOP_K], wts, x1, g_post_ffn[l], mod,
                  w_g, w_l, b_gate_up[l], w_down[l], b_down[l], seq)
    return x2.reshape(bsz, seq, d)
```

```python
import functools
import math

import numpy as np
import jax
import jax.numpy as jnp
from jax import lax
from jax.experimental import pallas as pl
from jax.experimental.pallas import tpu as pltpu

F32 = jnp.float32
BF16 = jnp.bfloat16
I32 = jnp.int32

HEAD_DIM_SWA = 64
SWA_GROUP = 8
WINDOW = 128
ROPE_THETA = 150000.0
HEAD_DIM_MOBA = 128
MOBA_GROUP = 4
MOBA_BLOCK = 256
MOBA_TOPK = 3
REL_BUCKETS = 32
REL_MAX_DIST = 1024
N_EXPERTS = 32
TOP_K = 4
SWIGLU_LIMIT = 7.0
SWIGLU_ALPHA = 1.702
MOE_BLOCK = 256
RMS_EPS = 1e-6

LANES = 128
SUBLANES = 8
BF16_SUBLANES = 16
VMEM_BYTES_V7X = 64 * 1024 * 1024
VMEM_CAP = VMEM_BYTES_V7X - 8 * 1024 * 1024

NEG_INF = float("-inf")
LOG2E = 1.0 / math.log(2.0)
N_BIAS_TILES = (REL_MAX_DIST + MOBA_BLOCK - 1) // MOBA_BLOCK + 2


def _nbytes(shape, dtype):
    return int(np.prod(shape)) * jnp.dtype(dtype).itemsize


def _params(semantics, blocks, temps=()):
    need = 2 * sum(_nbytes(s, d) for s, d in blocks) + sum(_nbytes(s, d) for s, d in temps)
    need = max(need + need // 4, 16 * 1024 * 1024)
    return pltpu.CompilerParams(dimension_semantics=semantics, vmem_limit_bytes=min(need, VMEM_CAP))


def _tile(n, pref):
    t = min(n, pref)
    assert n % t == 0, (n, pref)
    return t


def _ada_kernel(c_ref, w_ref, b_ref, o_ref):
    c = c_ref[...]
    s = c * jax.nn.sigmoid(c)
    o_ref[...] = jnp.dot(s, w_ref[...], preferred_element_type=F32,
                         precision=lax.Precision.HIGHEST) + b_ref[...]


def _ada_mod(c, w, b):
    bsz, d = c.shape
    n = w.shape[1]
    rows = -(-bsz // SUBLANES) * SUBLANES
    cp = jnp.pad(c, ((0, rows - bsz), (0, 0)))
    tn = _tile(n, 512)
    out = pl.pallas_call(
        _ada_kernel,
        grid=(n // tn,),
        in_specs=[pl.BlockSpec((rows, d), lambda j: (0, 0)),
                  pl.BlockSpec((d, tn), lambda j: (0, j)),
                  pl.BlockSpec((1, tn), lambda j: (0, j))],
        out_specs=pl.BlockSpec((rows, tn), lambda j: (0, j)),
        out_shape=jax.ShapeDtypeStruct((rows, n), F32),
        compiler_params=_params(("arbitrary",), [((d, tn), F32), ((rows, d), F32)], [((d, tn), F32)]),
        name="ada_mod",
    )(cp, w, b.reshape(1, n))
    return out[:bsz].reshape(bsz, 6, d)


def _rms(x, g):
    return x * lax.rsqrt(jnp.mean(x * x, axis=-1, keepdims=True) + RMS_EPS) * g


def _prenorm_kernel(x_ref, g_ref, mod_ref, o_ref, *, shift_row, scale_row):
    m = mod_ref[0]
    y = _rms(x_ref[...], g_ref[...])
    o_ref[...] = (y * (1.0 + m[scale_row:scale_row + 1, :]) + m[shift_row:shift_row + 1, :]).astype(o_ref.dtype)


def _prenorm(x2, g, mod, seq, shift_row, scale_row):
    t, d = x2.shape
    tm = _tile(seq, 256)
    return pl.pallas_call(
        functools.partial(_prenorm_kernel, shift_row=shift_row, scale_row=scale_row),
        grid=(t // tm,),
        in_specs=[pl.BlockSpec((tm, d), lambda i: (i, 0)),
                  pl.BlockSpec((1, d), lambda i: (0, 0)),
                  pl.BlockSpec((1, 6, d), lambda i: (i * tm // seq, 0, 0))],
        out_specs=pl.BlockSpec((tm, d), lambda i: (i, 0)),
        out_shape=jax.ShapeDtypeStruct((t, d), BF16),
        compiler_params=_params(("parallel",), [((tm, d), F32), ((tm, d), BF16)], [((tm, d), F32)] * 2),
        name="prenorm",
    )(x2, g.reshape(1, d), mod)


def _mm_kernel(a_ref, w_ref, b_ref, o_ref):
    acc = jnp.dot(a_ref[...], w_ref[...], preferred_element_type=F32)
    o_ref[...] = (acc + b_ref[...]).astype(o_ref.dtype)


def _matmul(a, w, bias, out_dtype, tm, tn, name):
    m, k = a.shape
    n = w.shape[1]
    tm, tn = _tile(m, tm), _tile(n, tn)
    return pl.pallas_call(
        _mm_kernel,
        grid=(m // tm, n // tn),
        in_specs=[pl.BlockSpec((tm, k), lambda i, j: (i, 0)),
                  pl.BlockSpec((k, tn), lambda i, j: (0, j)),
                  pl.BlockSpec((1, tn), lambda i, j: (0, j))],
        out_specs=pl.BlockSpec((tm, tn), lambda i, j: (i, j)),
        out_shape=jax.ShapeDtypeStruct((m, n), out_dtype),
        compiler_params=_params(("parallel", "arbitrary"),
                                [((tm, k), a.dtype), ((k, tn), w.dtype), ((tm, tn), out_dtype)],
                                [((tm, tn), F32)]),
        name=name,
    )(a, w, bias.reshape(1, n).astype(F32))


def _rope_tables(seq):
    half = HEAD_DIM_SWA // 2
    inv = ROPE_THETA ** (-jnp.arange(half, dtype=F32) / half)
    ang = jnp.arange(seq, dtype=F32)[:, None] * inv[None, :]
    cos, sin = jnp.cos(ang), jnp.sin(ang)
    cos_t = jnp.concatenate([cos, cos, cos, cos], axis=-1)
    sin_t = jnp.concatenate([-sin, sin, -sin, sin], axis=-1)
    return cos_t, sin_t


def _swa_kernel(sinks_ref, q_ref, kp_ref, kc_ref, vp_ref, vc_ref, cq_ref, sq_ref, cp_ref, sp_ref, o_ref,
                *, n_kv):
    n = pl.program_id(1)
    w = WINDOW
    half = HEAD_DIM_SWA // 2
    lane = lax.broadcasted_iota(I32, (1, LANES), 1)
    first_half = (lane % HEAD_DIM_SWA) < half
    low_head = lane < HEAD_DIM_SWA

    def rope(xf, cos, sin):
        partner = jnp.where(first_half, pltpu.roll(xf, LANES - half, 1), pltpu.roll(xf, half, 1))
        return xf * cos + partner * sin

    cos_q, sin_q = cq_ref[...], sq_ref[...]
    cos_k = jnp.concatenate([cp_ref[...], cos_q], axis=0)
    sin_k = jnp.concatenate([sp_ref[...], sin_q], axis=0)
    ri = lax.broadcasted_iota(I32, (w, 2 * w), 0)
    cj = lax.broadcasted_iota(I32, (w, 2 * w), 1)
    diff = w + ri - cj
    mask = (diff >= 0) & (diff < w) & ((cj >= w) | (n > 0))
    pairs = SWA_GROUP // 2

    for g in range(n_kv):
        slab = g // 2
        cols = slice(slab * LANES, (slab + 1) * LANES)
        k2 = rope(jnp.concatenate([kp_ref[:, cols], kc_ref[:, cols]], axis=0).astype(F32), cos_k, sin_k)
        v2 = jnp.concatenate([vp_ref[:, cols], vc_ref[:, cols]], axis=0).astype(F32)
        k2s, v2s = pltpu.roll(k2, HEAD_DIM_SWA, 1), pltpu.roll(v2, HEAD_DIM_SWA, 1)
        if g % 2 == 0:
            k_lo, v_lo = jnp.where(low_head, k2, 0.0), jnp.where(low_head, v2, 0.0)
            k_hi, v_hi = jnp.where(low_head, 0.0, k2s), jnp.where(low_head, 0.0, v2s)
        else:
            k_lo, v_lo = jnp.where(low_head, k2s, 0.0), jnp.where(low_head, v2s, 0.0)
            k_hi, v_hi = jnp.where(low_head, 0.0, k2), jnp.where(low_head, 0.0, v2)
        kv = ((k_lo.astype(BF16), v_lo.astype(BF16)), (k_hi.astype(BF16), v_hi.astype(BF16)))
        for p in range(pairs):
            qs = g * pairs + p
            qcols = slice(qs * LANES, (qs + 1) * LANES)
            q = rope(q_ref[:, qcols].astype(F32), cos_q, sin_q) * (1.0 / math.sqrt(HEAD_DIM_SWA))
            q = q.astype(BF16)
            o = jnp.zeros((w, LANES), F32)
            for hh, (kk, vv) in enumerate(kv):
                s = lax.dot_general(q, kk, (((1,), (1,)), ((), ())), preferred_element_type=F32)
                s = jnp.where(mask, s, NEG_INF)
                sink = sinks_ref[2 * qs + hh]
                m = jnp.maximum(jnp.max(s, axis=-1, keepdims=True), sink)
                e = jnp.exp(s - m)
                den = jnp.sum(e, axis=-1, keepdims=True) + jnp.exp(sink - m)
                o = o + jnp.dot(e.astype(BF16), vv, preferred_element_type=F32) / den
            o_ref[:, qcols] = o.astype(o_ref.dtype)


def _swa_attention(proj, sinks, cos_t, sin_t, bsz, seq, d):
    q_w, kv_w = d // 2, d // 16
    n_kv = kv_w // HEAD_DIM_SWA
    assert n_kv % 2 == 0 and q_w % kv_w == 0
    nb = seq // WINDOW
    k_col, v_col = q_w // kv_w, q_w // kv_w + 1
    cur = lambda b, n: b * nb + n
    prev = lambda b, n: b * nb + jnp.maximum(n - 1, 0)
    return pl.pallas_call(
        functools.partial(_swa_kernel, n_kv=n_kv),
        grid=(bsz, nb),
        in_specs=[pl.BlockSpec(memory_space=pltpu.SMEM),
                  pl.BlockSpec((WINDOW, q_w), lambda b, n: (cur(b, n), 0)),
                  pl.BlockSpec((WINDOW, kv_w), lambda b, n: (prev(b, n), k_col)),
                  pl.BlockSpec((WINDOW, kv_w), lambda b, n: (cur(b, n), k_col)),
                  pl.BlockSpec((WINDOW, kv_w), lambda b, n: (prev(b, n), v_col)),
                  pl.BlockSpec((WINDOW, kv_w), lambda b, n: (cur(b, n), v_col)),
                  pl.BlockSpec((WINDOW, LANES), lambda b, n: (n, 0)),
                  pl.BlockSpec((WINDOW, LANES), lambda b, n: (n, 0)),
                  pl.BlockSpec((WINDOW, LANES), lambda b, n: (jnp.maximum(n - 1, 0), 0)),
                  pl.BlockSpec((WINDOW, LANES), lambda b, n: (jnp.maximum(n - 1, 0), 0))],
        out_specs=pl.BlockSpec((WINDOW, q_w), lambda b, n: (cur(b, n), 0)),
        out_shape=jax.ShapeDtypeStruct((bsz * seq, q_w), BF16),
        compiler_params=_params(("parallel", "arbitrary"),
                                [((WINDOW, q_w), BF16)] * 2 + [((WINDOW, kv_w), BF16)] * 4,
                                [((2 * WINDOW, 2 * WINDOW), F32)] * 16),
        name="swa_attn",
    )(sinks, proj, proj, proj, proj, proj, cos_t, sin_t, cos_t, sin_t)


def _t5_thresholds():
    exact = REL_BUCKETS // 2
    d = np.arange(exact, 2 * REL_MAX_DIST, dtype=np.float64)
    large = exact + np.floor(np.log(d / exact) / math.log(REL_MAX_DIST / exact) * (REL_BUCKETS - exact)).astype(np.int64)
    large = np.minimum(large, REL_BUCKETS - 1)
    return [int(exact + np.argmax(large >= b)) for b in range(exact + 1, REL_BUCKETS)]


def _bias_tile_kernel(table_ref, o_ref):
    dd = pl.program_id(0)
    h = pl.program_id(1)
    exact = REL_BUCKETS // 2
    key = lax.broadcasted_iota(I32, (MOBA_BLOCK, MOBA_BLOCK), 0)
    qry = lax.broadcasted_iota(I32, (MOBA_BLOCK, MOBA_BLOCK), 1)
    dist = jnp.maximum(dd * MOBA_BLOCK + qry - key, 0)
    bucket = jnp.minimum(dist, exact)
    for thr in _t5_thresholds():
        bucket = bucket + (dist >= thr).astype(I32)
    val = jnp.full((MOBA_BLOCK, MOBA_BLOCK), table_ref[REL_BUCKETS - 1, h], F32)
    for b in range(REL_BUCKETS - 2, -1, -1):
        val = jnp.where(bucket == b, table_ref[b, h], val)
    o_ref[0, 0] = val * LOG2E


def _bias_tiles(rel_bias):
    n_heads = rel_bias.shape[1]
    return pl.pallas_call(
        _bias_tile_kernel,
        grid=(N_BIAS_TILES, n_heads),
        in_specs=[pl.BlockSpec(memory_space=pltpu.SMEM)],
        out_specs=pl.BlockSpec((1, 1, MOBA_BLOCK, MOBA_BLOCK), lambda dd, h: (h, dd, 0, 0)),
        out_shape=jax.ShapeDtypeStruct((n_heads, N_BIAS_TILES, MOBA_BLOCK, MOBA_BLOCK), F32),
        compiler_params=_params(("arbitrary", "arbitrary"), [((MOBA_BLOCK, MOBA_BLOCK), F32)],
                                [((MOBA_BLOCK, MOBA_BLOCK), F32)] * 4),
        name="moba_bias_tiles",
    )(rel_bias)


def _kmean_kernel(k_ref, o_ref, *, nblk):
    for j in range(nblk):
        rows = k_ref[j * MOBA_BLOCK:(j + 1) * MOBA_BLOCK, :].astype(F32)
        o_ref[j:j + 1, :] = jnp.mean(rows, axis=0, keepdims=True)


def _kmean(proj, bsz, seq, d):
    kv_w = d // 8
    nblk = seq // MOBA_BLOCK
    col = (9 * d // 8) // kv_w
    return pl.pallas_call(
        functools.partial(_kmean_kernel, nblk=nblk),
        grid=(bsz,),
        in_specs=[pl.BlockSpec((seq, kv_w), lambda b: (b, col))],
        out_specs=pl.BlockSpec((nblk, kv_w), lambda b: (b, 0)),
        out_shape=jax.ShapeDtypeStruct((bsz * nblk, kv_w), F32),
        compiler_params=_params(("parallel",), [((seq, kv_w), BF16)], [((MOBA_BLOCK, kv_w), F32)] * 2),
        name="moba_kmean",
    )(proj)


def _deinterleave_perm():
    perm = np.zeros((2 * LANES, 2 * LANES), np.float32)
    perm[2 * np.arange(LANES), np.arange(LANES)] = 1.0
    perm[2 * np.arange(LANES) + 1, LANES + np.arange(LANES)] = 1.0
    return jnp.asarray(perm, BF16)


def _deinterleave_block(w_ref, p_ref, wg_ref, wl_ref):
    perm = p_ref[...]
    group = perm.shape[0]
    half = group // 2
    for c in range(w_ref.shape[2] // group):
        t = jnp.dot(w_ref[0, :, c * group:(c + 1) * group].astype(BF16), perm, preferred_element_type=F32)
        wg_ref[0, :, c * half:(c + 1) * half] = t[:, :half].astype(BF16)
        wl_ref[0, :, c * half:(c + 1) * half] = t[:, half:].astype(BF16)


def _moba_kernel(q_ref, k_ref, v_ref, km_ref, bias_ref, w_ref, p_ref, o_ref, wg_ref, wl_ref, vt_ref, *, nblk):
    i = pl.program_id(2)
    mb, dh = MOBA_BLOCK, HEAD_DIM_MOBA
    scale = 1.0 / math.sqrt(dh)
    nt = (((1,), (1,)), ((), ()))
    blk = lax.broadcasted_iota(I32, (nblk, mb), 0)
    key = lax.broadcasted_iota(I32, (mb, mb), 0)
    qry = lax.broadcasted_iota(I32, (mb, mb), 1)
    causal = qry >= key

    @pl.when(i == 0)
    def _():
        for j in range(nblk):
            vt_ref[:dh, j * mb:(j + 1) * mb] = v_ref[j * mb:(j + 1) * mb, :].astype(F32).T.astype(BF16)
        vt_ref[dh:, :] = jnp.ones((vt_ref.shape[0] - dh, vt_ref.shape[1]), BF16)

    kmean = km_ref[...]
    km_hi = kmean.astype(BF16)
    rest = kmean - km_hi.astype(F32)
    km_mid = rest.astype(BF16)
    km_lo = (rest - km_mid.astype(F32)).astype(BF16)
    own = pl.multiple_of(i * mb, mb)
    k_own, vt_own = k_ref[pl.ds(own, mb), :], vt_ref[:, pl.ds(own, mb)]

    qs = [q_ref[:, hh * dh:(hh + 1) * dh] for hh in range(MOBA_GROUP)]
    gates = [lax.dot_general(km_hi, q, nt, preferred_element_type=F32)
             + lax.dot_general(km_mid, q, nt, preferred_element_type=F32)
             + lax.dot_general(km_lo, q, nt, preferred_element_type=F32) for q in qs]
    own_scores = [lax.dot_general(k_own, q, nt, preferred_element_type=F32) for q in qs]
    picks, init = [], []
    for hh in range(MOBA_GROUP):
        gate = jnp.where(blk < i, gates[hh], NEG_INF)
        pick = []
        for t in range(MOBA_TOPK):
            gmax = jnp.max(gate, axis=0, keepdims=True)
            idx = jnp.min(jnp.where(gate == gmax, blk, nblk), axis=0, keepdims=True)
            pick.append(jnp.where(t < i, idx, -1))
            gate = jnp.where(blk == idx, NEG_INF, gate)

        s = jnp.where(causal, own_scores[hh] * (scale * LOG2E) + bias_ref[hh, 0], NEG_INF)
        m0 = jnp.max(s, axis=0, keepdims=True)
        a0 = jnp.dot(vt_own, jnp.exp2(s - m0).astype(BF16), preferred_element_type=F32)
        picks.append(pick)
        init.append((m0, a0))

    def past(jj, carry):
        ja = 2 * jj
        start = pl.multiple_of(ja * mb, mb)
        kj, vtj = k_ref[pl.ds(start, 2 * mb), :], vt_ref[:, pl.ds(start, 2 * mb)]
        tile_a = jnp.minimum(i - ja, N_BIAS_TILES - 1)
        tile_b = jnp.minimum(i - ja - 1, N_BIAS_TILES - 1)
        scores = [lax.dot_general(kj, qs[hh], nt, preferred_element_type=F32) for hh in range(MOBA_GROUP)]
        out = []
        for hh in range(MOBA_GROUP):
            m, acc = carry[hh]
            p0, p1, p2 = picks[hh]
            sa = scores[hh][:mb] * (scale * LOG2E) + bias_ref[hh, tile_a]
            sb = scores[hh][mb:] * (scale * LOG2E) + bias_ref[hh, tile_b]
            sa = jnp.where((p0 == ja) | (p1 == ja) | (p2 == ja), sa, NEG_INF)
            sb = jnp.where((p0 == ja + 1) | (p1 == ja + 1) | (p2 == ja + 1), sb, NEG_INF)
            m_new = jnp.maximum(m, jnp.maximum(jnp.max(sa, axis=0, keepdims=True),
                                               jnp.max(sb, axis=0, keepdims=True)))
            e = jnp.concatenate([jnp.exp2(sa - m_new), jnp.exp2(sb - m_new)], axis=0).astype(BF16)
            acc = jnp.exp2(m - m_new) * acc + jnp.dot(vtj, e, preferred_element_type=F32)
            out.append((m_new, acc))
        return tuple(out)

    final = lax.fori_loop(0, (i + 1) // 2, past, tuple(init))
    for hh in range(MOBA_GROUP):
        _, acc = final[hh]
        o_ref[:, hh * dh:(hh + 1) * dh] = (acc[:dh] / acc[dh:dh + 1]).T.astype(o_ref.dtype)
    _deinterleave_block(w_ref, p_ref, wg_ref, wl_ref)


def _moba_attention(proj, kmean, bias_tiles, w_gu, bsz, seq, d):
    gw = MOBA_GROUP * HEAD_DIM_MOBA
    n_kv = (d // 8) // HEAD_DIM_MOBA
    nblk = seq // MOBA_BLOCK
    q_off, k_off, v_off = 5 * d // 8, 9 * d // 8, 10 * d // 8
    assert q_off % gw == 0 and MOBA_TOPK == 3
    qc, kc, vc = q_off // gw, k_off // HEAD_DIM_MOBA, v_off // HEAD_DIM_MOBA
    tile_shape = (MOBA_GROUP, N_BIAS_TILES, MOBA_BLOCK, MOBA_BLOCK)
    n_exp, wd, f2 = w_gu.shape
    steps = n_kv * bsz * nblk
    tk = n_exp * wd // steps
    assert n_exp * wd % steps == 0 and wd % tk == 0 and tk % BF16_SUBLANES == 0
    per_exp = wd // tk
    wmap = lambda g, b, i: (((g * bsz + b) * nblk + i) // per_exp, ((g * bsz + b) * nblk + i) % per_exp, 0)
    w_out = jax.ShapeDtypeStruct((n_exp, wd, f2 // 2), BF16)
    return pl.pallas_call(
        functools.partial(_moba_kernel, nblk=nblk),
        grid=(n_kv, bsz, nblk),
        in_specs=[pl.BlockSpec((MOBA_BLOCK, gw), lambda g, b, i: (b * nblk + i, qc + g)),
                  pl.BlockSpec((seq, HEAD_DIM_MOBA), lambda g, b, i: (b, kc + g)),
                  pl.BlockSpec((seq, HEAD_DIM_MOBA), lambda g, b, i: (b, vc + g)),
                  pl.BlockSpec((nblk, HEAD_DIM_MOBA), lambda g, b, i: (b, g)),
                  pl.BlockSpec(tile_shape, lambda g, b, i: (g, 0, 0, 0)),
                  pl.BlockSpec((1, tk, f2), wmap),
                  pl.BlockSpec((2 * LANES, 2 * LANES), lambda g, b, i: (0, 0))],
        out_specs=[pl.BlockSpec((MOBA_BLOCK, gw), lambda g, b, i: (b * nblk + i, g)),
                   pl.BlockSpec((1, tk, f2 // 2), wmap), pl.BlockSpec((1, tk, f2 // 2), wmap)],
        out_shape=[jax.ShapeDtypeStruct((bsz * seq, d // 2), BF16), w_out, w_out],
        scratch_shapes=[pltpu.VMEM((HEAD_DIM_MOBA + BF16_SUBLANES, seq), BF16)],
        compiler_params=_params(("parallel", "parallel", "arbitrary"),
                                [(tile_shape, F32), ((seq, HEAD_DIM_MOBA), BF16), ((seq, HEAD_DIM_MOBA), BF16),
                                 ((MOBA_BLOCK, gw), BF16), ((MOBA_BLOCK, gw), BF16),
                                 ((tk, f2), F32), ((tk, f2), BF16)],
                                [((2 * MOBA_BLOCK, MOBA_BLOCK), F32)] * 8
                                + [((HEAD_DIM_MOBA + BF16_SUBLANES, seq), BF16)] + [((tk, 2 * LANES), F32)] * 4),
        name="moba_attn",
    )(proj, proj, proj, kmean, bias_tiles, w_gu, _deinterleave_perm())


def _merge_kernel(oa_ref, ob_ref, wa_ref, wb_ref, ga_ref, gb_ref, o_ref):
    ya = jnp.dot(oa_ref[...], wa_ref[...], preferred_element_type=F32)
    yb = jnp.dot(ob_ref[...], wb_ref[...], preferred_element_type=F32)
    merged = jax.nn.sigmoid(ga_ref[...].astype(F32)) * ya + jax.nn.sigmoid(gb_ref[...].astype(F32)) * yb
    o_ref[...] = merged.astype(o_ref.dtype)


def _merge(o_a, o_b, w_a, w_b, proj, d):
    t, kdim = o_a.shape
    tm, tn = _tile(t, 1024), _tile(d, 512)
    ga_col, gb_col = (11 * d // 8) // tn, (19 * d // 8) // tn
    assert (11 * d // 8) % tn == 0 and (19 * d // 8) % tn == 0
    return pl.pallas_call(
        _merge_kernel,
        grid=(t // tm, d // tn),
        in_specs=[pl.BlockSpec((tm, kdim), lambda i, j: (i, 0)),
                  pl.BlockSpec((tm, kdim), lambda i, j: (i, 0)),
                  pl.BlockSpec((kdim, tn), lambda i, j: (0, j)),
                  pl.BlockSpec((kdim, tn), lambda i, j: (0, j)),
                  pl.BlockSpec((tm, tn), lambda i, j: (i, ga_col + j)),
                  pl.BlockSpec((tm, tn), lambda i, j: (i, gb_col + j))],
        out_specs=pl.BlockSpec((tm, tn), lambda i, j: (i, j)),
        out_shape=jax.ShapeDtypeStruct((t, d), BF16),
        compiler_params=_params(("parallel", "arbitrary"),
                                [((tm, kdim), BF16)] * 2 + [((kdim, tn), BF16)] * 2 + [((tm, tn), BF16)] * 3,
                                [((tm, tn), F32)] * 3),
        name="merge",
    )(o_a, o_b, w_a, w_b, proj, proj)


def _post_mix_kernel(y_ref, x_ref, gpost_ref, gpre_ref, mod_ref, wr_ref, br_ref,
                     x1_ref, h2_ref, idx_ref, wts_ref):
    m = mod_ref[0]
    x1 = x_ref[...] + m[2:3, :] * _rms(y_ref[...], gpost_ref[...])
    x1_ref[...] = x1
    h2 = _rms(x1, gpre_ref[...]) * (1.0 + m[4:5, :]) + m[3:4, :]
    half = h2.shape[1] // 2
    h2_ref[...] = _pack_bf16_pair(h2[:, :half], h2[:, half:])
    logits = jnp.dot(h2, wr_ref[...], preferred_element_type=F32,
                     precision=lax.Precision.HIGHEST) + br_ref[...]
    lane = lax.broadcasted_iota(I32, logits.shape, 1)
    logits = jnp.where(lane < N_EXPERTS, logits, NEG_INF)
    idx_out = jnp.zeros(logits.shape, I32)
    val_out = jnp.zeros(logits.shape, F32)
    top = None
    den = jnp.zeros((logits.shape[0], 1), F32)
    for k in range(TOP_K):
        vmax = jnp.max(logits, axis=-1, keepdims=True)
        idx = jnp.min(jnp.where(logits == vmax, lane, LANES), axis=-1, keepdims=True)
        top = vmax if top is None else top
        e = jnp.exp(vmax - top)
        den = den + e
        idx_out = jnp.where(lane == k, idx, idx_out)
        val_out = jnp.where(lane == k, e, val_out)
        logits = jnp.where(lane == idx, NEG_INF, logits)
    idx_ref[...] = idx_out
    wts_ref[...] = val_out / den


def _post_mix(y, x2, g_post, g_pre, mod, w_router, b_router, seq):
    t, d = x2.shape
    tm = _tile(seq, 256)
    wr = jnp.pad(w_router, ((0, 0), (0, LANES - N_EXPERTS)))
    br = jnp.pad(b_router, (0, LANES - N_EXPERTS)).reshape(1, LANES)
    row = lambda i: (i, 0)
    const = lambda i: (0, 0)
    return pl.pallas_call(
        _post_mix_kernel,
        grid=(t // tm,),
        in_specs=[pl.BlockSpec((tm, d), row), pl.BlockSpec((tm, d), row),
                  pl.BlockSpec((1, d), const), pl.BlockSpec((1, d), const),
                  pl.BlockSpec((1, 6, d), lambda i: (i * tm // seq, 0, 0)),
                  pl.BlockSpec((d, LANES), const), pl.BlockSpec((1, LANES), const)],
        out_specs=[pl.BlockSpec((tm, d), row), pl.BlockSpec((tm, d // 2), row),
                   pl.BlockSpec((tm, LANES), row), pl.BlockSpec((tm, LANES), row)],
        out_shape=[jax.ShapeDtypeStruct((t, d), F32), jax.ShapeDtypeStruct((t, d // 2), jnp.uint32),
                   jax.ShapeDtypeStruct((t, LANES), I32), jax.ShapeDtypeStruct((t, LANES), F32)],
        compiler_params=_params(("parallel",), [((tm, d), F32)] * 4 + [((d, LANES), F32)],
                                [((tm, d), F32)] * 3),
        name="post_mix_router",
    )(y, x2, g_post.reshape(1, d), g_pre.reshape(1, d), mod, wr, br)


def _routing(top_idx, n_blocks):
    flat_e = top_idx.reshape(-1)
    onehot = (flat_e[:, None] == jnp.arange(N_EXPERTS, dtype=I32)[None, :]).astype(I32)
    csum = jnp.cumsum(onehot, axis=0)
    counts = csum[-1]
    rank = jnp.sum(csum * onehot, axis=1) - 1
    padded = (counts + MOE_BLOCK - 1) // MOE_BLOCK * MOE_BLOCK
    pad_end = jnp.cumsum(padded)
    pad_start = pad_end - padded
    dest = jnp.sum(onehot * pad_start[None, :], axis=1) + rank
    n_real = (pad_end[-1] // MOE_BLOCK).astype(I32)
    blk_start = jnp.arange(n_blocks, dtype=I32) * MOE_BLOCK
    block_e = jnp.minimum(jnp.sum((pad_end[None, :] <= blk_start[:, None]).astype(I32), axis=1), N_EXPERTS - 1)
    last_blk = jnp.where(counts > 0, pad_end // MOE_BLOCK - 1, -1).astype(I32)
    return dest.astype(I32), block_e.astype(I32), n_real.reshape(1), last_blk


def _pack_bf16_pair(lo, hi):
    lo_bits = lax.bitcast_convert_type(lo.astype(BF16).astype(F32), jnp.uint32) >> 16
    hi_bits = lax.bitcast_convert_type(hi.astype(BF16).astype(F32), jnp.uint32) & jnp.uint32(0xFFFF0000)
    return hi_bits | lo_bits


def _unpack_bf16_pair(p):
    lo = lax.bitcast_convert_type(p << 16, F32).astype(BF16)
    hi = lax.bitcast_convert_type(p & jnp.uint32(0xFFFF0000), F32).astype(BF16)
    return lo, hi


def _dispatch_kernel(last_ref, nr_ref, dest_ref, h_ref, xg_hbm, zeros, zsem, sem, *, tokens, n_blocks):
    def zero_block(m):
        rows = pl.ds(pl.multiple_of(m * MOE_BLOCK, MOE_BLOCK), MOE_BLOCK)
        return pltpu.make_async_copy(zeros, xg_hbm.at[rows], zsem)

    @pl.when(pl.program_id(0) == 0)
    def _():
        zeros[...] = jnp.zeros(zeros.shape, zeros.dtype)
        for phase in ("start", "wait"):
            for e in range(N_EXPERTS):
                @pl.when(last_ref[e] >= 0)
                def _(e=e, phase=phase):
                    getattr(zero_block(last_ref[e]), phase)()

            def tail(m, carry, phase=phase):
                getattr(zero_block(m), phase)()
                return carry

            lax.fori_loop(nr_ref[0], n_blocks, tail, 0)

    def row_copy(t, row):
        return pltpu.make_async_copy(h_ref.at[pl.ds(t, 1)], xg_hbm.at[pl.ds(row, 1)], sem)

    def issue(t, carry):
        for k in range(TOP_K):
            row_copy(t, dest_ref[0, 0, t * TOP_K + k]).start()
        return carry

    lax.fori_loop(0, tokens, issue, 0)

    def drain(t, carry):
        for k in range(TOP_K):
            row_copy(0, 0).wait()
        return carry

    lax.fori_loop(0, tokens, drain, 0)


def _dispatch(h2p, dest, last_blk, n_real, n_blocks):
    t, dp = h2p.shape
    tokens = _tile(t, 256)
    steps = t // tokens
    return pl.pallas_call(
        functools.partial(_dispatch_kernel, tokens=tokens, n_blocks=n_blocks),
        grid_spec=pltpu.PrefetchScalarGridSpec(
            num_scalar_prefetch=2, grid=(steps,),
            in_specs=[pl.BlockSpec((1, 1, tokens * TOP_K), lambda i, last, nr: (i, 0, 0), memory_space=pltpu.SMEM),
                      pl.BlockSpec((tokens, dp), lambda i, last, nr: (i, 0))],
            out_specs=pl.BlockSpec(memory_space=pl.ANY),
            scratch_shapes=[pltpu.VMEM((MOE_BLOCK, dp), h2p.dtype),
                            pltpu.SemaphoreType.DMA(()), pltpu.SemaphoreType.DMA(())]),
        out_shape=jax.ShapeDtypeStruct((n_blocks * MOE_BLOCK, dp), h2p.dtype),
        compiler_params=_params(("arbitrary",), [((tokens, dp), h2p.dtype)], [((MOE_BLOCK, dp), h2p.dtype)]),
        name="moe_dispatch",
    )(last_blk, n_real, dest.reshape(steps, 1, tokens * TOP_K), h2p)


def _gate_up_kernel(be_ref, nr_ref, x_ref, wg_ref, wl_ref, bg_ref, bl_ref, wd_ref, o_ref, wdb_ref, *, n_cast):
    real = pl.program_id(1) < nr_ref[0]

    @pl.when(jnp.logical_and(pl.program_id(0) == 0, pl.program_id(1) < n_cast))
    def _():
        wdb_ref[...] = wd_ref[...].astype(BF16)

    @pl.when(real)
    def _():
        lo, hi = _unpack_bf16_pair(x_ref[...])
        half = lo.shape[1]

        def proj(w_ref, b_ref):
            return (jnp.dot(lo, w_ref[0, :half, :], preferred_element_type=F32)
                    + jnp.dot(hi, w_ref[0, half:, :], preferred_element_type=F32) + b_ref[0])

        glu = jnp.minimum(proj(wg_ref, bg_ref), SWIGLU_LIMIT)
        lin = jnp.clip(proj(wl_ref, bl_ref), -SWIGLU_LIMIT, SWIGLU_LIMIT)
        o_ref[...] = (glu * jax.nn.sigmoid(SWIGLU_ALPHA * glu) * (lin + 1.0)).astype(o_ref.dtype)

    @pl.when(jnp.logical_not(real))
    def _():
        o_ref[...] = jnp.zeros(o_ref.shape, o_ref.dtype)


def _gate_up(xg, w_g, w_l, b_g, b_l, w_d, block_e, n_real, n_blocks):
    rows, dp = xg.shape
    d, f = w_g.shape[1], w_g.shape[2]
    passes = 2
    tf = f // passes
    blk = lambda p, m, be, nr: jnp.minimum(m, nr[0] - 1)
    wmap = lambda p, m, be, nr: (be[blk(p, m, be, nr)], 0, p)
    n_exp, wf, wn = w_d.shape
    tr = min(r for r in range(BF16_SUBLANES, wf + 1, BF16_SUBLANES)
             if wf % r == 0 and n_exp * wf // r <= n_blocks)
    per_exp = wf // tr
    n_cast = n_exp * per_exp

    def cmap(p, m, be, nr):
        s = jnp.where(p == 0, jnp.minimum(m, n_cast - 1), n_cast - 1)
        return (s // per_exp, s % per_exp, 0)

    return pl.pallas_call(
        functools.partial(_gate_up_kernel, n_cast=n_cast),
        grid_spec=pltpu.PrefetchScalarGridSpec(
            num_scalar_prefetch=2, grid=(passes, n_blocks),
            in_specs=[pl.BlockSpec((MOE_BLOCK, dp), lambda p, m, be, nr: (blk(p, m, be, nr), 0)),
                      pl.BlockSpec((1, d, tf), wmap), pl.BlockSpec((1, d, tf), wmap),
                      pl.BlockSpec((1, 1, tf), wmap), pl.BlockSpec((1, 1, tf), wmap),
                      pl.BlockSpec((1, tr, wn), cmap)],
            out_specs=[pl.BlockSpec((MOE_BLOCK, tf), lambda p, m, be, nr: (m, p)),
                       pl.BlockSpec((1, tr, wn), cmap)]),
        out_shape=[jax.ShapeDtypeStruct((rows, f), BF16), jax.ShapeDtypeStruct(w_d.shape, BF16)],
        compiler_params=_params(("arbitrary", "arbitrary"),
                                [((MOE_BLOCK, dp), xg.dtype), ((d, tf), BF16), ((d, tf), BF16),
                                 ((MOE_BLOCK, tf), BF16), ((tr, wn), F32), ((tr, wn), BF16)],
                                [((MOE_BLOCK, d), BF16)] + [((MOE_BLOCK, tf), F32)] * 4),
        name="moe_gate_up",
    )(block_e, n_real, xg, w_g, w_l, b_g, b_l, w_d)


def _down_kernel(be_ref, nr_ref, a_ref, w_ref, b_ref, o_ref):
    real = pl.program_id(0) < nr_ref[0]

    @pl.when(real)
    def _():
        y = jnp.dot(a_ref[...], w_ref[0], preferred_element_type=F32) + b_ref[0]
        half = y.shape[1] // 2
        o_ref[...] = _pack_bf16_pair(y[:, :half], y[:, half:])

    @pl.when(jnp.logical_not(real))
    def _():
        o_ref[...] = jnp.zeros(o_ref.shape, o_ref.dtype)


def _down(act, w_d, b_d, block_e, n_real, n_blocks):
    rows, f = act.shape
    d = w_d.shape[2]
    blk = lambda m, be, nr: jnp.minimum(m, nr[0] - 1)
    wmap = lambda m, be, nr: (be[blk(m, be, nr)], 0, 0)
    return pl.pallas_call(
        _down_kernel,
        grid_spec=pltpu.PrefetchScalarGridSpec(
            num_scalar_prefetch=2, grid=(n_blocks,),
            in_specs=[pl.BlockSpec((MOE_BLOCK, f), lambda m, be, nr: (m, 0)),
                      pl.BlockSpec((1, f, d), wmap), pl.BlockSpec((1, 1, d), wmap)],
            out_specs=pl.BlockSpec((MOE_BLOCK, d // 2), lambda m, be, nr: (m, 0))),
        out_shape=jax.ShapeDtypeStruct((rows, d // 2), jnp.uint32),
        compiler_params=_params(("arbitrary",),
                                [((MOE_BLOCK, f), BF16), ((f, d), BF16), ((MOE_BLOCK, d // 2), jnp.uint32)],
                                [((MOE_BLOCK, d), F32)] * 2),
        name="moe_down",
    )(block_e, n_real, act, w_d, b_d)


def _combine_kernel(dcur_ref, dnext_ref, w_ref, x1_ref, g_ref, mod_ref, y_hbm, o_ref, buf, sem,
                    *, tokens, steps):
    step = pl.program_id(0)
    slot = step % 2

    def row_copy(slot_, k, t, row):
        return pltpu.make_async_copy(y_hbm.at[pl.ds(row, 1)], buf.at[slot_, k, pl.ds(t, 1)], sem.at[slot_])

    def start_tile(dest_ref, slot_):
        def body(t, carry):
            for k in range(TOP_K):
                row_copy(slot_, k, t, dest_ref[0, 0, t * TOP_K + k]).start()
            return carry
        lax.fori_loop(0, tokens, body, 0)

    @pl.when(step == 0)
    def _():
        start_tile(dcur_ref, 0)

    @pl.when(step + 1 < steps)
    def _():
        start_tile(dnext_ref, 1 - slot)

    def drain(t, carry):
        for k in range(TOP_K):
            row_copy(slot, k, 0, 0).wait()
        return carry

    lax.fori_loop(0, tokens, drain, 0)

    w = w_ref[...]
    acc_lo = acc_hi = None
    for k in range(TOP_K):
        word = buf[slot, k]
        lo = lax.bitcast_convert_type(word << 16, F32) * w[:, k:k + 1]
        hi = lax.bitcast_convert_type(word & jnp.uint32(0xFFFF0000), F32) * w[:, k:k + 1]
        acc_lo = lo if acc_lo is None else acc_lo + lo
        acc_hi = hi if acc_hi is None else acc_hi + hi
    acc = jnp.concatenate([acc_lo, acc_hi], axis=1)
    m = mod_ref[0]
    o_ref[...] = x1_ref[...] + m[5:6, :] * _rms(acc, g_ref[...])


def _combine(y_sorted, dest, top_w, x1, g_post, mod, seq):
    t, d = x1.shape
    dp = y_sorted.shape[1]
    tokens = _tile(seq, 128)
    steps = t // tokens
    row = lambda i: (i, 0)
    dest3 = dest.reshape(steps, 1, tokens * TOP_K)
    return pl.pallas_call(
        functools.partial(_combine_kernel, tokens=tokens, steps=steps),
        grid=(steps,),
        in_specs=[pl.BlockSpec((1, 1, tokens * TOP_K), lambda i: (i, 0, 0), memory_space=pltpu.SMEM),
                  pl.BlockSpec((1, 1, tokens * TOP_K), lambda i: (jnp.minimum(i + 1, steps - 1), 0, 0),
                               memory_space=pltpu.SMEM),
                  pl.BlockSpec((tokens, LANES), row),
                  pl.BlockSpec((tokens, d), row),
                  pl.BlockSpec((1, d), lambda i: (0, 0)),
                  pl.BlockSpec((1, 6, d), lambda i: (i * tokens // seq, 0, 0)),
                  pl.BlockSpec(memory_space=pl.ANY)],
        out_specs=pl.BlockSpec((tokens, d), row),
        out_shape=jax.ShapeDtypeStruct((t, d), F32),
        scratch_shapes=[pltpu.VMEM((2, TOP_K, tokens, dp), y_sorted.dtype), pltpu.SemaphoreType.DMA((2,))],
        compiler_params=_params(("arbitrary",), [((tokens, d), F32)] * 2,
                                [((2, TOP_K, tokens, dp), y_sorted.dtype)] + [((tokens, d), F32)] * 3),
        name="moe_combine",
    )(dest3, dest3, top_w, x1, g_post.reshape(1, d), mod, y_sorted)


def _moe(h2, top_idx, top_w, x1, g_post, mod, w_g, w_l, b_gu, w_dn, b_dn, seq):
    t, d = x1.shape
    f = w_dn.shape[1]
    n_blocks = -(-(t * TOP_K) // MOE_BLOCK) + N_EXPERTS
    dest, block_e, n_real, last_blk = _routing(top_idx, n_blocks)
    xg = _dispatch(h2, dest, last_blk, n_real, n_blocks)
    b_g = b_gu[:, 0::2].reshape(N_EXPERTS, 1, f)
    b_l = b_gu[:, 1::2].reshape(N_EXPERTS, 1, f)
    act, w_dn_bf16 = _gate_up(xg, w_g, w_l, b_g, b_l, w_dn, block_e, n_real, n_blocks)
    y_sorted = _down(act, w_dn_bf16, b_dn.reshape(N_EXPERTS, 1, d), block_e, n_real, n_blocks)
    return _combine(y_sorted, dest, top_w, x1, g_post, mod, seq)


def kernel(x, c, w_ada, b_ada, g_pre_mix, g_post_mix, g_pre_ffn, g_post_ffn, w_in, b_in, sinks, rel_bias,
           w_o_swa, w_o_moba, w_out, w_router, b_router, w_gate_up, b_gate_up, w_down, b_down):
    bsz, seq, d = x.shape
    depth = w_ada.shape[0]
    assert seq % MOBA_BLOCK == 0 and d % 1024 == 0
    x2 = x.reshape(bsz * seq, d)
    cos_t, sin_t = _rope_tables(seq)
    bias_tiles = _bias_tiles(rel_bias)
    for l in range(depth):
        mod = _ada_mod(c, w_ada[l], b_ada[l])
        h = _prenorm(x2, g_pre_mix[l], mod, seq, shift_row=0, scale_row=1)
        proj = _matmul(h, w_in[l].astype(BF16), b_in[l], BF16, 1024, 768, "in_proj")
        o_a = _swa_attention(proj, sinks[l], cos_t, sin_t, bsz, seq, d)
        o_b, w_g, w_l = _moba_attention(proj, _kmean(proj, bsz, seq, d), bias_tiles, w_gate_up[l], bsz, seq, d)
        merged = _merge(o_a, o_b, w_o_swa[l].astype(BF16), w_o_moba[l].astype(BF16), proj, d)
        y = _matmul(merged, w_out[l].astype(BF16), jnp.zeros((d,), F32), F32, 1024, 1024, "out_proj")
        x1, h2, idx, wts = _post_mix(y, x2, g_post_mix[l], g_pre_ffn[l], mod, w_router[l], b_router[l], seq)
        x2 = _moe(h2, idx[:, :TOP_K], wts, x1, g_post_ffn[l], mod,
                  w_g, w_l, b_gate_up[l], w_down[l], b_down[l], seq)
    return x2.reshape(bsz, seq, d)
```

```python
import functools
import math

import numpy as np
import jax
import jax.numpy as jnp
from jax import lax
from jax.experimental import pallas as pl
from jax.experimental.pallas import tpu as pltpu

F32 = jnp.float32
BF16 = jnp.bfloat16
I32 = jnp.int32

HEAD_DIM_SWA = 64
SWA_GROUP = 8
WINDOW = 128
ROPE_THETA = 150000.0
HEAD_DIM_MOBA = 128
MOBA_GROUP = 4
MOBA_BLOCK = 256
MOBA_TOPK = 3
REL_BUCKETS = 32
REL_MAX_DIST = 1024
N_EXPERTS = 32
TOP_K = 4
SWIGLU_LIMIT = 7.0
SWIGLU_ALPHA = 1.702
MOE_BLOCK = 256
RMS_EPS = 1e-6

LANES = 128
SUBLANES = 8
BF16_SUBLANES = 16
VMEM_BYTES_V7X = 64 * 1024 * 1024
VMEM_CAP = VMEM_BYTES_V7X - 8 * 1024 * 1024

NEG_INF = float("-inf")
LOG2E = 1.0 / math.log(2.0)
N_BIAS_TILES = (REL_MAX_DIST + MOBA_BLOCK - 1) // MOBA_BLOCK + 2


def _nbytes(shape, dtype):
    return int(np.prod(shape)) * jnp.dtype(dtype).itemsize


def _params(semantics, blocks, temps=()):
    need = 2 * sum(_nbytes(s, d) for s, d in blocks) + sum(_nbytes(s, d) for s, d in temps)
    need = max(need + need // 4, 16 * 1024 * 1024)
    return pltpu.CompilerParams(dimension_semantics=semantics, vmem_limit_bytes=min(need, VMEM_CAP))


def _tile(n, pref):
    t = min(n, pref)
    assert n % t == 0, (n, pref)
    return t


def _ada_kernel(c_ref, w_ref, b_ref, o_ref):
    c = c_ref[...]
    s = c * jax.nn.sigmoid(c)
    o_ref[...] = jnp.dot(s, w_ref[...], preferred_element_type=F32,
                         precision=lax.Precision.HIGHEST) + b_ref[...]


def _ada_mod(c, w, b):
    bsz, d = c.shape
    n = w.shape[1]
    rows = -(-bsz // SUBLANES) * SUBLANES
    cp = jnp.pad(c, ((0, rows - bsz), (0, 0)))
    tn = _tile(n, 512)
    out = pl.pallas_call(
        _ada_kernel,
        grid=(n // tn,),
        in_specs=[pl.BlockSpec((rows, d), lambda j: (0, 0)),
                  pl.BlockSpec((d, tn), lambda j: (0, j)),
                  pl.BlockSpec((1, tn), lambda j: (0, j))],
        out_specs=pl.BlockSpec((rows, tn), lambda j: (0, j)),
        out_shape=jax.ShapeDtypeStruct((rows, n), F32),
        compiler_params=_params(("arbitrary",), [((d, tn), F32), ((rows, d), F32)], [((d, tn), F32)]),
        name="ada_mod",
    )(cp, w, b.reshape(1, n))
    return out[:bsz].reshape(bsz, 6, d)


def _rms(x, g):
    return x * lax.rsqrt(jnp.mean(x * x, axis=-1, keepdims=True) + RMS_EPS) * g


def _prenorm_kernel(x_ref, g_ref, mod_ref, o_ref, *, shift_row, scale_row):
    m = mod_ref[0]
    y = _rms(x_ref[...], g_ref[...])
    o_ref[...] = (y * (1.0 + m[scale_row:scale_row + 1, :]) + m[shift_row:shift_row + 1, :]).astype(o_ref.dtype)


def _prenorm(x2, g, mod, seq, shift_row, scale_row):
    t, d = x2.shape
    tm = _tile(seq, 256)
    return pl.pallas_call(
        functools.partial(_prenorm_kernel, shift_row=shift_row, scale_row=scale_row),
        grid=(t // tm,),
        in_specs=[pl.BlockSpec((tm, d), lambda i: (i, 0)),
                  pl.BlockSpec((1, d), lambda i: (0, 0)),
                  pl.BlockSpec((1, 6, d), lambda i: (i * tm // seq, 0, 0))],
        out_specs=pl.BlockSpec((tm, d), lambda i: (i, 0)),
        out_shape=jax.ShapeDtypeStruct((t, d), BF16),
        compiler_params=_params(("parallel",), [((tm, d), F32), ((tm, d), BF16)], [((tm, d), F32)] * 2),
        name="prenorm",
    )(x2, g.reshape(1, d), mod)


def _mm_kernel(a_ref, w_ref, b_ref, o_ref):
    acc = jnp.dot(a_ref[...], w_ref[...], preferred_element_type=F32)
    o_ref[...] = (acc + b_ref[...]).astype(o_ref.dtype)


def _matmul(a, w, bias, out_dtype, tm, tn, name):
    m, k = a.shape
    n = w.shape[1]
    tm, tn = _tile(m, tm), _tile(n, tn)
    return pl.pallas_call(
        _mm_kernel,
        grid=(m // tm, n // tn),
        in_specs=[pl.BlockSpec((tm, k), lambda i, j: (i, 0)),
                  pl.BlockSpec((k, tn), lambda i, j: (0, j)),
                  pl.BlockSpec((1, tn), lambda i, j: (0, j))],
        out_specs=pl.BlockSpec((tm, tn), lambda i, j: (i, j)),
        out_shape=jax.ShapeDtypeStruct((m, n), out_dtype),
        compiler_params=_params(("parallel", "arbitrary"),
                                [((tm, k), a.dtype), ((k, tn), w.dtype), ((tm, tn), out_dtype)],
                                [((tm, tn), F32)]),
        name=name,
    )(a, w, bias.reshape(1, n).astype(F32))


def _rope_tables(seq):
    half = HEAD_DIM_SWA // 2
    inv = ROPE_THETA ** (-jnp.arange(half, dtype=F32) / half)
    ang = jnp.arange(seq, dtype=F32)[:, None] * inv[None, :]
    cos, sin = jnp.cos(ang), jnp.sin(ang)
    cos_t = jnp.concatenate([cos, cos, cos, cos], axis=-1)
    sin_t = jnp.concatenate([-sin, sin, -sin, sin], axis=-1)
    return cos_t, sin_t


def _swa_kernel(sinks_ref, q_ref, kp_ref, kc_ref, vp_ref, vc_ref, cq_ref, sq_ref, cp_ref, sp_ref, o_ref,
                *, n_kv):
    n = pl.program_id(1)
    w = WINDOW
    half = HEAD_DIM_SWA // 2
    lane = lax.broadcasted_iota(I32, (1, LANES), 1)
    first_half = (lane % HEAD_DIM_SWA) < half
    low_head = lane < HEAD_DIM_SWA

    def rope(xf, cos, sin):
        partner = jnp.where(first_half, pltpu.roll(xf, LANES - half, 1), pltpu.roll(xf, half, 1))
        return xf * cos + partner * sin

    cos_q, sin_q = cq_ref[...], sq_ref[...]
    cos_k = jnp.concatenate([cp_ref[...], cos_q], axis=0)
    sin_k = jnp.concatenate([sp_ref[...], sin_q], axis=0)
    ri = lax.broadcasted_iota(I32, (w, 2 * w), 0)
    cj = lax.broadcasted_iota(I32, (w, 2 * w), 1)
    diff = w + ri - cj
    mask = (diff >= 0) & (diff < w) & ((cj >= w) | (n > 0))
    pairs = SWA_GROUP // 2

    for g in range(n_kv):
        slab = g // 2
        cols = slice(slab * LANES, (slab + 1) * LANES)
        k2 = rope(jnp.concatenate([kp_ref[:, cols], kc_ref[:, cols]], axis=0).astype(F32), cos_k, sin_k)
        v2 = jnp.concatenate([vp_ref[:, cols], vc_ref[:, cols]], axis=0).astype(F32)
        k2s, v2s = pltpu.roll(k2, HEAD_DIM_SWA, 1), pltpu.roll(v2, HEAD_DIM_SWA, 1)
        if g % 2 == 0:
            k_lo, v_lo = jnp.where(low_head, k2, 0.0), jnp.where(low_head, v2, 0.0)
            k_hi, v_hi = jnp.where(low_head, 0.0, k2s), jnp.where(low_head, 0.0, v2s)
        else:
            k_lo, v_lo = jnp.where(low_head, k2s, 0.0), jnp.where(low_head, v2s, 0.0)
            k_hi, v_hi = jnp.where(low_head, 0.0, k2), jnp.where(low_head, 0.0, v2)
        kv = ((k_lo.astype(BF16), v_lo.astype(BF16)), (k_hi.astype(BF16), v_hi.astype(BF16)))
        scores = []
        for p in range(pairs):
            qcols = slice((g * pairs + p) * LANES, (g * pairs + p + 1) * LANES)
            q = rope(q_ref[:, qcols].astype(F32), cos_q, sin_q) * (1.0 / math.sqrt(HEAD_DIM_SWA))
            q = q.astype(BF16)
            scores.append([lax.dot_general(q, kk, (((1,), (1,)), ((), ())), preferred_element_type=F32)
                           for kk, _ in kv])
        for p in range(pairs):
            qs = g * pairs + p
            qcols = slice(qs * LANES, (qs + 1) * LANES)
            o = jnp.zeros((w, LANES), F32)
            for hh, (kk, vv) in enumerate(kv):
                s = jnp.where(mask, scores[p][hh], NEG_INF)
                sink = sinks_ref[2 * qs + hh]
                m = jnp.maximum(jnp.max(s, axis=-1, keepdims=True), sink)
                e = jnp.exp(s - m)
                den = jnp.sum(e, axis=-1, keepdims=True) + jnp.exp(sink - m)
                o = o + jnp.dot(e.astype(BF16), vv, preferred_element_type=F32) / den
            o_ref[:, qcols] = o.astype(o_ref.dtype)


def _swa_attention(proj, sinks, cos_t, sin_t, bsz, seq, d):
    q_w, kv_w = d // 2, d // 16
    n_kv = kv_w // HEAD_DIM_SWA
    assert n_kv % 2 == 0 and q_w % kv_w == 0
    nb = seq // WINDOW
    k_col, v_col = q_w // kv_w, q_w // kv_w + 1
    cur = lambda b, n: b * nb + n
    prev = lambda b, n: b * nb + jnp.maximum(n - 1, 0)
    return pl.pallas_call(
        functools.partial(_swa_kernel, n_kv=n_kv),
        grid=(bsz, nb),
        in_specs=[pl.BlockSpec(memory_space=pltpu.SMEM),
                  pl.BlockSpec((WINDOW, q_w), lambda b, n: (cur(b, n), 0)),
                  pl.BlockSpec((WINDOW, kv_w), lambda b, n: (prev(b, n), k_col)),
                  pl.BlockSpec((WINDOW, kv_w), lambda b, n: (cur(b, n), k_col)),
                  pl.BlockSpec((WINDOW, kv_w), lambda b, n: (prev(b, n), v_col)),
                  pl.BlockSpec((WINDOW, kv_w), lambda b, n: (cur(b, n), v_col)),
                  pl.BlockSpec((WINDOW, LANES), lambda b, n: (n, 0)),
                  pl.BlockSpec((WINDOW, LANES), lambda b, n: (n, 0)),
                  pl.BlockSpec((WINDOW, LANES), lambda b, n: (jnp.maximum(n - 1, 0), 0)),
                  pl.BlockSpec((WINDOW, LANES), lambda b, n: (jnp.maximum(n - 1, 0), 0))],
        out_specs=pl.BlockSpec((WINDOW, q_w), lambda b, n: (cur(b, n), 0)),
        out_shape=jax.ShapeDtypeStruct((bsz * seq, q_w), BF16),
        compiler_params=_params(("parallel", "arbitrary"),
                                [((WINDOW, q_w), BF16)] * 2 + [((WINDOW, kv_w), BF16)] * 4,
                                [((2 * WINDOW, 2 * WINDOW), F32)] * 16),
        name="swa_attn",
    )(sinks, proj, proj, proj, proj, proj, cos_t, sin_t, cos_t, sin_t)


def _t5_thresholds():
    exact = REL_BUCKETS // 2
    d = np.arange(exact, 2 * REL_MAX_DIST, dtype=np.float64)
    large = exact + np.floor(np.log(d / exact) / math.log(REL_MAX_DIST / exact) * (REL_BUCKETS - exact)).astype(np.int64)
    large = np.minimum(large, REL_BUCKETS - 1)
    return [int(exact + np.argmax(large >= b)) for b in range(exact + 1, REL_BUCKETS)]


def _bias_tile_kernel(table_ref, o_ref):
    dd = pl.program_id(0)
    h = pl.program_id(1)
    exact = REL_BUCKETS // 2
    key = lax.broadcasted_iota(I32, (MOBA_BLOCK, MOBA_BLOCK), 0)
    qry = lax.broadcasted_iota(I32, (MOBA_BLOCK, MOBA_BLOCK), 1)
    dist = jnp.maximum(dd * MOBA_BLOCK + qry - key, 0)
    bucket = jnp.minimum(dist, exact)
    for thr in _t5_thresholds():
        bucket = bucket + (dist >= thr).astype(I32)
    val = jnp.full((MOBA_BLOCK, MOBA_BLOCK), table_ref[REL_BUCKETS - 1, h], F32)
    for b in range(REL_BUCKETS - 2, -1, -1):
        val = jnp.where(bucket == b, table_ref[b, h], val)
    o_ref[0, 0] = val * LOG2E


def _bias_tiles(rel_bias):
    n_heads = rel_bias.shape[1]
    return pl.pallas_call(
        _bias_tile_kernel,
        grid=(N_BIAS_TILES, n_heads),
        in_specs=[pl.BlockSpec(memory_space=pltpu.SMEM)],
        out_specs=pl.BlockSpec((1, 1, MOBA_BLOCK, MOBA_BLOCK), lambda dd, h: (h, dd, 0, 0)),
        out_shape=jax.ShapeDtypeStruct((n_heads, N_BIAS_TILES, MOBA_BLOCK, MOBA_BLOCK), F32),
        compiler_params=_params(("arbitrary", "arbitrary"), [((MOBA_BLOCK, MOBA_BLOCK), F32)],
                                [((MOBA_BLOCK, MOBA_BLOCK), F32)] * 4),
        name="moba_bias_tiles",
    )(rel_bias)


def _kmean_kernel(k_ref, o_ref, *, nblk):
    for j in range(nblk):
        rows = k_ref[j * MOBA_BLOCK:(j + 1) * MOBA_BLOCK, :].astype(F32)
        o_ref[j:j + 1, :] = jnp.mean(rows, axis=0, keepdims=True)


def _kmean(proj, bsz, seq, d):
    kv_w = d // 8
    nblk = seq // MOBA_BLOCK
    col = (9 * d // 8) // kv_w
    return pl.pallas_call(
        functools.partial(_kmean_kernel, nblk=nblk),
        grid=(bsz,),
        in_specs=[pl.BlockSpec((seq, kv_w), lambda b: (b, col))],
        out_specs=pl.BlockSpec((nblk, kv_w), lambda b: (b, 0)),
        out_shape=jax.ShapeDtypeStruct((bsz * nblk, kv_w), F32),
        compiler_params=_params(("parallel",), [((seq, kv_w), BF16)], [((MOBA_BLOCK, kv_w), F32)] * 2),
        name="moba_kmean",
    )(proj)


def _deinterleave_perm():
    perm = np.zeros((2 * LANES, 2 * LANES), np.float32)
    perm[2 * np.arange(LANES), np.arange(LANES)] = 1.0
    perm[2 * np.arange(LANES) + 1, LANES + np.arange(LANES)] = 1.0
    return jnp.asarray(perm, BF16)


def _deinterleave_block(w_ref, p_ref, wg_ref, wl_ref):
    perm = p_ref[...]
    group = perm.shape[0]
    half = group // 2
    for c in range(w_ref.shape[2] // group):
        t = jnp.dot(w_ref[0, :, c * group:(c + 1) * group].astype(BF16), perm, preferred_element_type=F32)
        wg_ref[0, :, c * half:(c + 1) * half] = t[:, :half].astype(BF16)
        wl_ref[0, :, c * half:(c + 1) * half] = t[:, half:].astype(BF16)


def _moba_kernel(q_ref, k_ref, v_ref, km_ref, bias_ref, w_ref, p_ref, o_ref, wg_ref, wl_ref, vt_ref, *, nblk):
    i = pl.program_id(2)
    mb, dh = MOBA_BLOCK, HEAD_DIM_MOBA
    scale = 1.0 / math.sqrt(dh)
    nt = (((1,), (1,)), ((), ()))
    blk = lax.broadcasted_iota(I32, (nblk, mb), 0)
    key = lax.broadcasted_iota(I32, (mb, mb), 0)
    qry = lax.broadcasted_iota(I32, (mb, mb), 1)
    causal = qry >= key

    @pl.when(i == 0)
    def _():
        for j in range(nblk):
            vt_ref[:dh, j * mb:(j + 1) * mb] = v_ref[j * mb:(j + 1) * mb, :].astype(F32).T.astype(BF16)
        vt_ref[dh:, :] = jnp.ones((vt_ref.shape[0] - dh, vt_ref.shape[1]), BF16)

    kmean = km_ref[...]
    km_hi = kmean.astype(BF16)
    rest = kmean - km_hi.astype(F32)
    km_mid = rest.astype(BF16)
    km_lo = (rest - km_mid.astype(F32)).astype(BF16)
    own = pl.multiple_of(i * mb, mb)
    k_own, vt_own = k_ref[pl.ds(own, mb), :], vt_ref[:, pl.ds(own, mb)]

    qs = [q_ref[:, hh * dh:(hh + 1) * dh] for hh in range(MOBA_GROUP)]
    gates = [lax.dot_general(km_hi, q, nt, preferred_element_type=F32)
             + lax.dot_general(km_mid, q, nt, preferred_element_type=F32)
             + lax.dot_general(km_lo, q, nt, preferred_element_type=F32) for q in qs]
    own_scores = [lax.dot_general(k_own, q, nt, preferred_element_type=F32) for q in qs]
    _deinterleave_block(w_ref, p_ref, wg_ref, wl_ref)
    picks, init = [], []
    for hh in range(MOBA_GROUP):
        gate = jnp.where(blk < i, gates[hh], NEG_INF)
        pick = []
        for t in range(MOBA_TOPK):
            gmax = jnp.max(gate, axis=0, keepdims=True)
            idx = jnp.min(jnp.where(gate == gmax, blk, nblk), axis=0, keepdims=True)
            pick.append(jnp.where(t < i, idx, -1))
            gate = jnp.where(blk == idx, NEG_INF, gate)

        s = jnp.where(causal, own_scores[hh] * (scale * LOG2E) + bias_ref[hh, 0], NEG_INF)
        m0 = jnp.max(s, axis=0, keepdims=True)
        a0 = jnp.dot(vt_own, jnp.exp2(s - m0).astype(BF16), preferred_element_type=F32)
        picks.append(pick)
        init.append((m0, a0))

    def past(jj, carry):
        ja = 2 * jj
        start = pl.multiple_of(ja * mb, mb)
        kj, vtj = k_ref[pl.ds(start, 2 * mb), :], vt_ref[:, pl.ds(start, 2 * mb)]
        tile_a = jnp.minimum(i - ja, N_BIAS_TILES - 1)
        tile_b = jnp.minimum(i - ja - 1, N_BIAS_TILES - 1)
        scores = [lax.dot_general(kj, qs[hh], nt, preferred_element_type=F32) for hh in range(MOBA_GROUP)]
        out = []
        for hh in range(MOBA_GROUP):
            m, acc = carry[hh]
            p0, p1, p2 = picks[hh]
            sa = scores[hh][:mb] * (scale * LOG2E) + bias_ref[hh, tile_a]
            sb = scores[hh][mb:] * (scale * LOG2E) + bias_ref[hh, tile_b]
            sa = jnp.where((p0 == ja) | (p1 == ja) | (p2 == ja), sa, NEG_INF)
            sb = jnp.where((p0 == ja + 1) | (p1 == ja + 1) | (p2 == ja + 1), sb, NEG_INF)
            m_new = jnp.maximum(m, jnp.maximum(jnp.max(sa, axis=0, keepdims=True),
                                               jnp.max(sb, axis=0, keepdims=True)))
            e = jnp.concatenate([jnp.exp2(sa - m_new), jnp.exp2(sb - m_new)], axis=0).astype(BF16)
            acc = jnp.exp2(m - m_new) * acc + jnp.dot(vtj, e, preferred_element_type=F32)
            out.append((m_new, acc))
        return tuple(out)

    final = lax.fori_loop(0, (i + 1) // 2, past, tuple(init))
    for hh in range(MOBA_GROUP):
        _, acc = final[hh]
        o_ref[:, hh * dh:(hh + 1) * dh] = (acc[:dh] / acc[dh:dh + 1]).T.astype(o_ref.dtype)


def _moba_attention(proj, kmean, bias_tiles, w_gu, bsz, seq, d):
    gw = MOBA_GROUP * HEAD_DIM_MOBA
    n_kv = (d // 8) // HEAD_DIM_MOBA
    nblk = seq // MOBA_BLOCK
    q_off, k_off, v_off = 5 * d // 8, 9 * d // 8, 10 * d // 8
    assert q_off % gw == 0 and MOBA_TOPK == 3 and nblk >= 2
    qc, kc, vc = q_off // gw, k_off // HEAD_DIM_MOBA, v_off // HEAD_DIM_MOBA
    tile_shape = (MOBA_GROUP, N_BIAS_TILES, MOBA_BLOCK, MOBA_BLOCK)
    n_exp, wd, f2 = w_gu.shape
    steps = n_kv * bsz * nblk
    tk = n_exp * wd // steps
    assert n_exp * wd % steps == 0 and wd % tk == 0 and tk % BF16_SUBLANES == 0
    per_exp = wd // tk
    wmap = lambda g, b, i: (((g * bsz + b) * nblk + i) // per_exp, ((g * bsz + b) * nblk + i) % per_exp, 0)
    w_out = jax.ShapeDtypeStruct((n_exp, wd, f2 // 2), BF16)
    return pl.pallas_call(
        functools.partial(_moba_kernel, nblk=nblk),
        grid=(n_kv, bsz, nblk),
        in_specs=[pl.BlockSpec((MOBA_BLOCK, gw), lambda g, b, i: (b * nblk + i, qc + g)),
                  pl.BlockSpec((seq, HEAD_DIM_MOBA), lambda g, b, i: (b, kc + g)),
                  pl.BlockSpec((seq, HEAD_DIM_MOBA), lambda g, b, i: (b, vc + g)),
                  pl.BlockSpec((nblk, HEAD_DIM_MOBA), lambda g, b, i: (b, g)),
                  pl.BlockSpec(tile_shape, lambda g, b, i: (g, 0, 0, 0)),
                  pl.BlockSpec((1, tk, f2), wmap),
                  pl.BlockSpec((2 * LANES, 2 * LANES), lambda g, b, i: (0, 0))],
        out_specs=[pl.BlockSpec((MOBA_BLOCK, gw), lambda g, b, i: (b * nblk + i, g)),
                   pl.BlockSpec((1, tk, f2 // 2), wmap), pl.BlockSpec((1, tk, f2 // 2), wmap)],
        out_shape=[jax.ShapeDtypeStruct((bsz * seq, d // 2), BF16), w_out, w_out],
        scratch_shapes=[pltpu.VMEM((HEAD_DIM_MOBA + BF16_SUBLANES, seq), BF16)],
        compiler_params=_params(("parallel", "parallel", "arbitrary"),
                                [(tile_shape, F32), ((seq, HEAD_DIM_MOBA), BF16), ((seq, HEAD_DIM_MOBA), BF16),
                                 ((MOBA_BLOCK, gw), BF16), ((MOBA_BLOCK, gw), BF16),
                                 ((tk, f2), F32), ((tk, f2), BF16)],
                                [((2 * MOBA_BLOCK, MOBA_BLOCK), F32)] * 8
                                + [((HEAD_DIM_MOBA + BF16_SUBLANES, seq), BF16)] + [((tk, 2 * LANES), F32)] * 4),
        name="moba_attn",
    )(proj, proj, proj, kmean, bias_tiles, w_gu, _deinterleave_perm())


def _merge_kernel(oa_ref, ob_ref, wa_ref, wb_ref, ga_ref, gb_ref, o_ref):
    ya = jnp.dot(oa_ref[...], wa_ref[...], preferred_element_type=F32)
    yb = jnp.dot(ob_ref[...], wb_ref[...], preferred_element_type=F32)
    merged = jax.nn.sigmoid(ga_ref[...].astype(F32)) * ya + jax.nn.sigmoid(gb_ref[...].astype(F32)) * yb
    o_ref[...] = merged.astype(o_ref.dtype)


def _merge(o_a, o_b, w_a, w_b, proj, d):
    t, kdim = o_a.shape
    tm, tn = _tile(t, 1024), _tile(d, 512)
    ga_col, gb_col = (11 * d // 8) // tn, (19 * d // 8) // tn
    assert (11 * d // 8) % tn == 0 and (19 * d // 8) % tn == 0
    return pl.pallas_call(
        _merge_kernel,
        grid=(t // tm, d // tn),
        in_specs=[pl.BlockSpec((tm, kdim), lambda i, j: (i, 0)),
                  pl.BlockSpec((tm, kdim), lambda i, j: (i, 0)),
                  pl.BlockSpec((kdim, tn), lambda i, j: (0, j)),
                  pl.BlockSpec((kdim, tn), lambda i, j: (0, j)),
                  pl.BlockSpec((tm, tn), lambda i, j: (i, ga_col + j)),
                  pl.BlockSpec((tm, tn), lambda i, j: (i, gb_col + j))],
        out_specs=pl.BlockSpec((tm, tn), lambda i, j: (i, j)),
        out_shape=jax.ShapeDtypeStruct((t, d), BF16),
        compiler_params=_params(("parallel", "arbitrary"),
                                [((tm, kdim), BF16)] * 2 + [((kdim, tn), BF16)] * 2 + [((tm, tn), BF16)] * 3,
                                [((tm, tn), F32)] * 3),
        name="merge",
    )(o_a, o_b, w_a, w_b, proj, proj)


def _post_mix_kernel(y_ref, x_ref, gpost_ref, gpre_ref, mod_ref, wr_ref, br_ref,
                     x1_ref, h2_ref, idx_ref, wts_ref):
    m = mod_ref[0]
    x1 = x_ref[...] + m[2:3, :] * _rms(y_ref[...], gpost_ref[...])
    x1_ref[...] = x1
    h2 = _rms(x1, gpre_ref[...]) * (1.0 + m[4:5, :]) + m[3:4, :]
    half = h2.shape[1] // 2
    h2_ref[...] = _pack_bf16_pair(h2[:, :half], h2[:, half:])
    logits = jnp.dot(h2, wr_ref[...], preferred_element_type=F32,
                     precision=lax.Precision.HIGHEST) + br_ref[...]
    lane = lax.broadcasted_iota(I32, logits.shape, 1)
    logits = jnp.where(lane < N_EXPERTS, logits, NEG_INF)
    idx_out = jnp.zeros(logits.shape, I32)
    val_out = jnp.zeros(logits.shape, F32)
    top = None
    den = jnp.zeros((logits.shape[0], 1), F32)
    for k in range(TOP_K):
        vmax = jnp.max(logits, axis=-1, keepdims=True)
        idx = jnp.min(jnp.where(logits == vmax, lane, LANES), axis=-1, keepdims=True)
        top = vmax if top is None else top
        e = jnp.exp(vmax - top)
        den = den + e
        idx_out = jnp.where(lane == k, idx, idx_out)
        val_out = jnp.where(lane == k, e, val_out)
        logits = jnp.where(lane == idx, NEG_INF, logits)
    idx_ref[...] = idx_out
    wts_ref[...] = val_out / den


def _post_mix(y, x2, g_post, g_pre, mod, w_router, b_router, seq):
    t, d = x2.shape
    tm = _tile(seq, 256)
    wr = jnp.pad(w_router, ((0, 0), (0, LANES - N_EXPERTS)))
    br = jnp.pad(b_router, (0, LANES - N_EXPERTS)).reshape(1, LANES)
    row = lambda i: (i, 0)
    const = lambda i: (0, 0)
    return pl.pallas_call(
        _post_mix_kernel,
        grid=(t // tm,),
        in_specs=[pl.BlockSpec((tm, d), row), pl.BlockSpec((tm, d), row),
                  pl.BlockSpec((1, d), const), pl.BlockSpec((1, d), const),
                  pl.BlockSpec((1, 6, d), lambda i: (i * tm // seq, 0, 0)),
                  pl.BlockSpec((d, LANES), const), pl.BlockSpec((1, LANES), const)],
        out_specs=[pl.BlockSpec((tm, d), row), pl.BlockSpec((tm, d // 2), row),
                   pl.BlockSpec((tm, LANES), row), pl.BlockSpec((tm, LANES), row)],
        out_shape=[jax.ShapeDtypeStruct((t, d), F32), jax.ShapeDtypeStruct((t, d // 2), jnp.uint32),
                   jax.ShapeDtypeStruct((t, LANES), I32), jax.ShapeDtypeStruct((t, LANES), F32)],
        compiler_params=_params(("parallel",), [((tm, d), F32)] * 4 + [((d, LANES), F32)],
                                [((tm, d), F32)] * 3),
        name="post_mix_router",
    )(y, x2, g_post.reshape(1, d), g_pre.reshape(1, d), mod, wr, br)


def _routing(top_idx, n_blocks):
    flat_e = top_idx.reshape(-1)
    onehot = (flat_e[:, None] == jnp.arange(N_EXPERTS, dtype=I32)[None, :]).astype(I32)
    csum = jnp.cumsum(onehot, axis=0)
    counts = csum[-1]
    rank = jnp.sum(csum * onehot, axis=1) - 1
    padded = (counts + MOE_BLOCK - 1) // MOE_BLOCK * MOE_BLOCK
    pad_end = jnp.cumsum(padded)
    pad_start = pad_end - padded
    dest = jnp.sum(onehot * pad_start[None, :], axis=1) + rank
    n_real = (pad_end[-1] // MOE_BLOCK).astype(I32)
    blk_start = jnp.arange(n_blocks, dtype=I32) * MOE_BLOCK
    block_e = jnp.minimum(jnp.sum((pad_end[None, :] <= blk_start[:, None]).astype(I32), axis=1), N_EXPERTS - 1)
    last_blk = jnp.where(counts > 0, pad_end // MOE_BLOCK - 1, -1).astype(I32)
    return dest.astype(I32), block_e.astype(I32), n_real.reshape(1), last_blk


def _pack_bf16_pair(lo, hi):
    lo_bits = lax.bitcast_convert_type(lo.astype(BF16).astype(F32), jnp.uint32) >> 16
    hi_bits = lax.bitcast_convert_type(hi.astype(BF16).astype(F32), jnp.uint32) & jnp.uint32(0xFFFF0000)
    return hi_bits | lo_bits


def _unpack_bf16_pair(p):
    lo = lax.bitcast_convert_type(p << 16, F32).astype(BF16)
    hi = lax.bitcast_convert_type(p & jnp.uint32(0xFFFF0000), F32).astype(BF16)
    return lo, hi


def _dispatch_kernel(last_ref, nr_ref, dest_ref, h_ref, xg_hbm, zeros, zsem, sem, *, tokens, n_blocks):
    def zero_block(m):
        rows = pl.ds(pl.multiple_of(m * MOE_BLOCK, MOE_BLOCK), MOE_BLOCK)
        return pltpu.make_async_copy(zeros, xg_hbm.at[rows], zsem)

    @pl.when(pl.program_id(0) == 0)
    def _():
        zeros[...] = jnp.zeros(zeros.shape, zeros.dtype)
        for phase in ("start", "wait"):
            for e in range(N_EXPERTS):
                @pl.when(last_ref[e] >= 0)
                def _(e=e, phase=phase):
                    getattr(zero_block(last_ref[e]), phase)()

            def tail(m, carry, phase=phase):
                getattr(zero_block(m), phase)()
                return carry

            lax.fori_loop(nr_ref[0], n_blocks, tail, 0)

    def row_copy(t, row):
        return pltpu.make_async_copy(h_ref.at[pl.ds(t, 1)], xg_hbm.at[pl.ds(row, 1)], sem)

    def issue(t, carry):
        for k in range(TOP_K):
            row_copy(t, dest_ref[0, 0, t * TOP_K + k]).start()
        return carry

    lax.fori_loop(0, tokens, issue, 0)

    def drain(t, carry):
        for k in range(TOP_K):
            row_copy(0, 0).wait()
        return carry

    lax.fori_loop(0, tokens, drain, 0)


def _dispatch(h2p, dest, last_blk, n_real, n_blocks):
    t, dp = h2p.shape
    tokens = _tile(t, 256)
    steps = t // tokens
    return pl.pallas_call(
        functools.partial(_dispatch_kernel, tokens=tokens, n_blocks=n_blocks),
        grid_spec=pltpu.PrefetchScalarGridSpec(
            num_scalar_prefetch=2, grid=(steps,),
            in_specs=[pl.BlockSpec((1, 1, tokens * TOP_K), lambda i, last, nr: (i, 0, 0), memory_space=pltpu.SMEM),
                      pl.BlockSpec((tokens, dp), lambda i, last, nr: (i, 0))],
            out_specs=pl.BlockSpec(memory_space=pl.ANY),
            scratch_shapes=[pltpu.VMEM((MOE_BLOCK, dp), h2p.dtype),
                            pltpu.SemaphoreType.DMA(()), pltpu.SemaphoreType.DMA(())]),
        out_shape=jax.ShapeDtypeStruct((n_blocks * MOE_BLOCK, dp), h2p.dtype),
        compiler_params=_params(("arbitrary",), [((tokens, dp), h2p.dtype)], [((MOE_BLOCK, dp), h2p.dtype)]),
        name="moe_dispatch",
    )(last_blk, n_real, dest.reshape(steps, 1, tokens * TOP_K), h2p)


def _gate_up_kernel(be_ref, nr_ref, x_ref, wg_ref, wl_ref, bg_ref, bl_ref, wd_ref, o_ref, wdb_ref, *, n_cast):
    real = pl.program_id(1) < nr_ref[0]

    @pl.when(jnp.logical_and(pl.program_id(0) == 0, pl.program_id(1) < n_cast))
    def _():
        wdb_ref[...] = wd_ref[...].astype(BF16)

    @pl.when(real)
    def _():
        lo, hi = _unpack_bf16_pair(x_ref[...])
        half = lo.shape[1]

        def proj(w_ref, b_ref):
            return (jnp.dot(lo, w_ref[0, :half, :], preferred_element_type=F32)
                    + jnp.dot(hi, w_ref[0, half:, :], preferred_element_type=F32) + b_ref[0])

        glu = jnp.minimum(proj(wg_ref, bg_ref), SWIGLU_LIMIT)
        lin = jnp.clip(proj(wl_ref, bl_ref), -SWIGLU_LIMIT, SWIGLU_LIMIT)
        o_ref[...] = (glu * jax.nn.sigmoid(SWIGLU_ALPHA * glu) * (lin + 1.0)).astype(o_ref.dtype)

    @pl.when(jnp.logical_not(real))
    def _():
        o_ref[...] = jnp.zeros(o_ref.shape, o_ref.dtype)


def _gate_up(xg, w_g, w_l, b_g, b_l, w_d, block_e, n_real, n_blocks):
    rows, dp = xg.shape
    d, f = w_g.shape[1], w_g.shape[2]
    passes = 2
    tf = f // passes
    blk = lambda p, m, be, nr: jnp.minimum(m, nr[0] - 1)
    wmap = lambda p, m, be, nr: (be[blk(p, m, be, nr)], 0, p)
    n_exp, wf, wn = w_d.shape
    tr = min(r for r in range(BF16_SUBLANES, wf + 1, BF16_SUBLANES)
             if wf % r == 0 and n_exp * wf // r <= n_blocks)
    per_exp = wf // tr
    n_cast = n_exp * per_exp

    def cmap(p, m, be, nr):
        s = jnp.where(p == 0, jnp.minimum(m, n_cast - 1), n_cast - 1)
        return (s // per_exp, s % per_exp, 0)

    return pl.pallas_call(
        functools.partial(_gate_up_kernel, n_cast=n_cast),
        grid_spec=pltpu.PrefetchScalarGridSpec(
            num_scalar_prefetch=2, grid=(passes, n_blocks),
            in_specs=[pl.BlockSpec((MOE_BLOCK, dp), lambda p, m, be, nr: (blk(p, m, be, nr), 0)),
                      pl.BlockSpec((1, d, tf), wmap), pl.BlockSpec((1, d, tf), wmap),
                      pl.BlockSpec((1, 1, tf), wmap), pl.BlockSpec((1, 1, tf), wmap),
                      pl.BlockSpec((1, tr, wn), cmap)],
            out_specs=[pl.BlockSpec((MOE_BLOCK, tf), lambda p, m, be, nr: (m, p)),
                       pl.BlockSpec((1, tr, wn), cmap)]),
        out_shape=[jax.ShapeDtypeStruct((rows, f), BF16), jax.ShapeDtypeStruct(w_d.shape, BF16)],
        compiler_params=_params(("arbitrary", "arbitrary"),
                                [((MOE_BLOCK, dp), xg.dtype), ((d, tf), BF16), ((d, tf), BF16),
                                 ((MOE_BLOCK, tf), BF16), ((tr, wn), F32), ((tr, wn), BF16)],
                                [((MOE_BLOCK, d), BF16)] + [((MOE_BLOCK, tf), F32)] * 4),
        name="moe_gate_up",
    )(block_e, n_real, xg, w_g, w_l, b_g, b_l, w_d)


def _down_kernel(be_ref, nr_ref, a_ref, w_ref, b_ref, o_ref):
    real = pl.program_id(0) < nr_ref[0]

    @pl.when(real)
    def _():
        y = jnp.dot(a_ref[...], w_ref[0], preferred_element_type=F32) + b_ref[0]
        half = y.shape[1] // 2
        o_ref[...] = _pack_bf16_pair(y[:, :half], y[:, half:])

    @pl.when(jnp.logical_not(real))
    def _():
        o_ref[...] = jnp.zeros(o_ref.shape, o_ref.dtype)


def _down(act, w_d, b_d, block_e, n_real, n_blocks):
    rows, f = act.shape
    d = w_d.shape[2]
    blk = lambda m, be, nr: jnp.minimum(m, nr[0] - 1)
    wmap = lambda m, be, nr: (be[blk(m, be, nr)], 0, 0)
    return pl.pallas_call(
        _down_kernel,
        grid_spec=pltpu.PrefetchScalarGridSpec(
            num_scalar_prefetch=2, grid=(n_blocks,),
            in_specs=[pl.BlockSpec((MOE_BLOCK, f), lambda m, be, nr: (m, 0)),
                      pl.BlockSpec((1, f, d), wmap), pl.BlockSpec((1, 1, d), wmap)],
            out_specs=pl.BlockSpec((MOE_BLOCK, d // 2), lambda m, be, nr: (m, 0))),
        out_shape=jax.ShapeDtypeStruct((rows, d // 2), jnp.uint32),
        compiler_params=_params(("arbitrary",),
                                [((MOE_BLOCK, f), BF16), ((f, d), BF16), ((MOE_BLOCK, d // 2), jnp.uint32)],
                                [((MOE_BLOCK, d), F32)] * 2),
        name="moe_down",
    )(block_e, n_real, act, w_d, b_d)


def _combine_kernel(dcur_ref, dnext_ref, w_ref, x1_ref, g_ref, mod_ref, y_hbm, o_ref, buf, sem,
                    *, tokens, steps):
    step = pl.program_id(0)
    slot = step % 2

    def row_copy(slot_, k, t, row):
        return pltpu.make_async_copy(y_hbm.at[pl.ds(row, 1)], buf.at[slot_, k, pl.ds(t, 1)], sem.at[slot_])

    def start_tile(dest_ref, slot_):
        def body(t, carry):
            for k in range(TOP_K):
                row_copy(slot_, k, t, dest_ref[0, 0, t * TOP_K + k]).start()
            return carry
        lax.fori_loop(0, tokens, body, 0)

    @pl.when(step == 0)
    def _():
        start_tile(dcur_ref, 0)

    @pl.when(step + 1 < steps)
    def _():
        start_tile(dnext_ref, 1 - slot)

    def drain(t, carry):
        for k in range(TOP_K):
            row_copy(slot, k, 0, 0).wait()
        return carry

    lax.fori_loop(0, tokens, drain, 0)

    w = w_ref[...]
    acc_lo = acc_hi = None
    for k in range(TOP_K):
        word = buf[slot, k]
        lo = lax.bitcast_convert_type(word << 16, F32) * w[:, k:k + 1]
        hi = lax.bitcast_convert_type(word & jnp.uint32(0xFFFF0000), F32) * w[:, k:k + 1]
        acc_lo = lo if acc_lo is None else acc_lo + lo
        acc_hi = hi if acc_hi is None else acc_hi + hi
    acc = jnp.concatenate([acc_lo, acc_hi], axis=1)
    m = mod_ref[0]
    o_ref[...] = x1_ref[...] + m[5:6, :] * _rms(acc, g_ref[...])


def _combine(y_sorted, dest, top_w, x1, g_post, mod, seq):
    t, d = x1.shape
    dp = y_sorted.shape[1]
    tokens = _tile(seq, 256)
    steps = t // tokens
    row = lambda i: (i, 0)
    dest3 = dest.reshape(steps, 1, tokens * TOP_K)
    return pl.pallas_call(
        functools.partial(_combine_kernel, tokens=tokens, steps=steps),
        grid=(steps,),
        in_specs=[pl.BlockSpec((1, 1, tokens * TOP_K), lambda i: (i, 0, 0), memory_space=pltpu.SMEM),
                  pl.BlockSpec((1, 1, tokens * TOP_K), lambda i: (jnp.minimum(i + 1, steps - 1), 0, 0),
                               memory_space=pltpu.SMEM),
                  pl.BlockSpec((tokens, LANES), row),
                  pl.BlockSpec((tokens, d), row),
                  pl.BlockSpec((1, d), lambda i: (0, 0)),
                  pl.BlockSpec((1, 6, d), lambda i: (i * tokens // seq, 0, 0)),
                  pl.BlockSpec(memory_space=pl.ANY)],
        out_specs=pl.BlockSpec((tokens, d), row),
        out_shape=jax.ShapeDtypeStruct((t, d), F32),
        scratch_shapes=[pltpu.VMEM((2, TOP_K, tokens, dp), y_sorted.dtype), pltpu.SemaphoreType.DMA((2,))],
        compiler_params=_params(("arbitrary",), [((tokens, d), F32)] * 2,
                                [((2, TOP_K, tokens, dp), y_sorted.dtype)] + [((tokens, d), F32)] * 3),
        name="moe_combine",
    )(dest3, dest3, top_w, x1, g_post.reshape(1, d), mod, y_sorted)


def _moe(h2, top_idx, top_w, x1, g_post, mod, w_g, w_l, b_gu, w_dn, b_dn, seq):
    t, d = x1.shape
    f = w_dn.shape[1]
    n_blocks = -(-(t * TOP_K) // MOE_BLOCK) + N_EXPERTS
    dest, block_e, n_real, last_blk = _routing(top_idx, n_blocks)
    xg = _dispatch(h2, dest, last_blk, n_real, n_blocks)
    b_g = b_gu[:, 0::2].reshape(N_EXPERTS, 1, f)
    b_l = b_gu[:, 1::2].reshape(N_EXPERTS, 1, f)
    act, w_dn_bf16 = _gate_up(xg, w_g, w_l, b_g, b_l, w_dn, block_e, n_real, n_blocks)
    y_sorted = _down(act, w_dn_bf16, b_dn.reshape(N_EXPERTS, 1, d), block_e, n_real, n_blocks)
    return _combine(y_sorted, dest, top_w, x1, g_post, mod, seq)


def kernel(x, c, w_ada, b_ada, g_pre_mix, g_post_mix, g_pre_ffn, g_post_ffn, w_in, b_in, sinks, rel_bias,
           w_o_swa, w_o_moba, w_out, w_router, b_router, w_gate_up, b_gate_up, w_down, b_down):
    bsz, seq, d = x.shape
    depth = w_ada.shape[0]
    assert seq % MOBA_BLOCK == 0 and d % 1024 == 0
    x2 = x.reshape(bsz * seq, d)
    cos_t, sin_t = _rope_tables(seq)
    bias_tiles = _bias_tiles(rel_bias)
    for l in range(depth):
        mod = _ada_mod(c, w_ada[l], b_ada[l])
        h = _prenorm(x2, g_pre_mix[l], mod, seq, shift_row=0, scale_row=1)
        proj = _matmul(h, w_in[l].astype(BF16), b_in[l], BF16, 1024, 768, "in_proj")
        o_a = _swa_attention(proj, sinks[l], cos_t, sin_t, bsz, seq, d)
        o_b, w_g, w_l = _moba_attention(proj, _kmean(proj, bsz, seq, d), bias_tiles, w_gate_up[l], bsz, seq, d)
        merged = _merge(o_a, o_b, w_o_swa[l].astype(BF16), w_o_moba[l].astype(BF16), proj, d)
        y = _matmul(merged, w_out[l].astype(BF16), jnp.zeros((d,), F32), F32, 1024, 1024, "out_proj")
        x1, h2, idx, wts = _post_mix(y, x2, g_post_mix[l], g_pre_ffn[l], mod, w_router[l], b_router[l], seq)
        x2 = _moe(h2, idx[:, :TOP_K], wts, x1, g_post_ffn[l], mod,
                  w_g, w_l, b_gate_up[l], w_down[l], b_down[l], seq)
    return x2.reshape(bsz, seq, d)
```

```python
import functools
import math

import numpy as np
import jax
import jax.numpy as jnp
from jax import lax
from jax.experimental import pallas as pl
from jax.experimental.pallas import tpu as pltpu

F32 = jnp.float32
BF16 = jnp.bfloat16
I32 = jnp.int32

HEAD_DIM_SWA = 64
SWA_GROUP = 8
WINDOW = 128
ROPE_THETA = 150000.0
HEAD_DIM_MOBA = 128
MOBA_GROUP = 4
MOBA_BLOCK = 256
MOBA_TOPK = 3
REL_BUCKETS = 32
REL_MAX_DIST = 1024
N_EXPERTS = 32
TOP_K = 4
SWIGLU_LIMIT = 7.0
SWIGLU_ALPHA = 1.702
MOE_BLOCK = 256
RMS_EPS = 1e-6

LANES = 128
SUBLANES = 8
BF16_SUBLANES = 16
VMEM_BYTES_V7X = 64 * 1024 * 1024
VMEM_CAP = VMEM_BYTES_V7X - 8 * 1024 * 1024

NEG_INF = float("-inf")
LOG2E = 1.0 / math.log(2.0)
N_BIAS_TILES = (REL_MAX_DIST + MOBA_BLOCK - 1) // MOBA_BLOCK + 2


def _nbytes(shape, dtype):
    return int(np.prod(shape)) * jnp.dtype(dtype).itemsize


def _params(semantics, blocks, temps=()):
    need = 2 * sum(_nbytes(s, d) for s, d in blocks) + sum(_nbytes(s, d) for s, d in temps)
    need = max(need + need // 4, 16 * 1024 * 1024)
    return pltpu.CompilerParams(dimension_semantics=semantics, vmem_limit_bytes=min(need, VMEM_CAP))


def _tile(n, pref):
    t = min(n, pref)
    assert n % t == 0, (n, pref)
    return t


def _split_bf16(x):
    hi = x.astype(BF16)
    return hi, (x - hi.astype(F32)).astype(BF16)


def _dot_split(a, w):
    (a_hi, a_lo), (w_hi, w_lo) = a, w
    return (jnp.dot(a_hi, w_hi, preferred_element_type=F32)
            + (jnp.dot(a_hi, w_lo, preferred_element_type=F32) + jnp.dot(a_lo, w_hi, preferred_element_type=F32)))


def _ada_kernel(c_ref, w_ref, b_ref, o_ref):
    c = c_ref[...]
    s = c * jax.nn.sigmoid(c)
    o_ref[...] = _dot_split(_split_bf16(s), _split_bf16(w_ref[...])) + b_ref[...]


def _ada_mod(c, w, b):
    bsz, d = c.shape
    n = w.shape[1]
    rows = -(-bsz // SUBLANES) * SUBLANES
    cp = jnp.pad(c, ((0, rows - bsz), (0, 0)))
    tn = _tile(n, 512)
    out = pl.pallas_call(
        _ada_kernel,
        grid=(n // tn,),
        in_specs=[pl.BlockSpec((rows, d), lambda j: (0, 0)),
                  pl.BlockSpec((d, tn), lambda j: (0, j)),
                  pl.BlockSpec((1, tn), lambda j: (0, j))],
        out_specs=pl.BlockSpec((rows, tn), lambda j: (0, j)),
        out_shape=jax.ShapeDtypeStruct((rows, n), F32),
        compiler_params=_params(("arbitrary",), [((d, tn), F32), ((rows, d), F32)], [((d, tn), F32)]),
        name="ada_mod",
    )(cp, w, b.reshape(1, n))
    return out[:bsz].reshape(bsz, 6, d)


def _rms(x, g):
    return x * lax.rsqrt(jnp.mean(x * x, axis=-1, keepdims=True) + RMS_EPS) * g


def _prenorm_kernel(x_ref, g_ref, mod_ref, o_ref, *, shift_row, scale_row):
    m = mod_ref[0]
    y = _rms(x_ref[...], g_ref[...])
    o_ref[...] = (y * (1.0 + m[scale_row:scale_row + 1, :]) + m[shift_row:shift_row + 1, :]).astype(o_ref.dtype)


def _prenorm(x2, g, mod, seq, shift_row, scale_row):
    t, d = x2.shape
    tm = _tile(seq, 256)
    return pl.pallas_call(
        functools.partial(_prenorm_kernel, shift_row=shift_row, scale_row=scale_row),
        grid=(t // tm,),
        in_specs=[pl.BlockSpec((tm, d), lambda i: (i, 0)),
                  pl.BlockSpec((1, d), lambda i: (0, 0)),
                  pl.BlockSpec((1, 6, d), lambda i: (i * tm // seq, 0, 0))],
        out_specs=pl.BlockSpec((tm, d), lambda i: (i, 0)),
        out_shape=jax.ShapeDtypeStruct((t, d), BF16),
        compiler_params=_params(("parallel",), [((tm, d), F32), ((tm, d), BF16)], [((tm, d), F32)] * 2),
        name="prenorm",
    )(x2, g.reshape(1, d), mod)


def _mm_kernel(a_ref, w_ref, b_ref, o_ref):
    acc = jnp.dot(a_ref[...], w_ref[...], preferred_element_type=F32)
    o_ref[...] = (acc + b_ref[...]).astype(o_ref.dtype)


def _matmul(a, w, bias, out_dtype, tm, tn, name):
    m, k = a.shape
    n = w.shape[1]
    tm, tn = _tile(m, tm), _tile(n, tn)
    return pl.pallas_call(
        _mm_kernel,
        grid=(m // tm, n // tn),
        in_specs=[pl.BlockSpec((tm, k), lambda i, j: (i, 0)),
                  pl.BlockSpec((k, tn), lambda i, j: (0, j)),
                  pl.BlockSpec((1, tn), lambda i, j: (0, j))],
        out_specs=pl.BlockSpec((tm, tn), lambda i, j: (i, j)),
        out_shape=jax.ShapeDtypeStruct((m, n), out_dtype),
        compiler_params=_params(("parallel", "arbitrary"),
                                [((tm, k), a.dtype), ((k, tn), w.dtype), ((tm, tn), out_dtype)],
                                [((tm, tn), F32)]),
        name=name,
    )(a, w, bias.reshape(1, n).astype(F32))


def _rope_tables(seq):
    half = HEAD_DIM_SWA // 2
    inv = ROPE_THETA ** (-jnp.arange(half, dtype=F32) / half)
    ang = jnp.arange(seq, dtype=F32)[:, None] * inv[None, :]
    cos, sin = jnp.cos(ang), jnp.sin(ang)
    cos_t = jnp.concatenate([cos, cos, cos, cos], axis=-1)
    sin_t = jnp.concatenate([-sin, sin, -sin, sin], axis=-1)
    return cos_t, sin_t


def _swa_kernel(sinks_ref, q_ref, kp_ref, kc_ref, vp_ref, vc_ref, cq_ref, sq_ref, cp_ref, sp_ref, o_ref,
                *, n_kv):
    n = pl.program_id(1)
    w = WINDOW
    half = HEAD_DIM_SWA // 2
    lane = lax.broadcasted_iota(I32, (1, LANES), 1)
    first_half = (lane % HEAD_DIM_SWA) < half
    low_head = lane < HEAD_DIM_SWA

    def rope(xf, cos, sin):
        partner = jnp.where(first_half, pltpu.roll(xf, LANES - half, 1), pltpu.roll(xf, half, 1))
        return xf * cos + partner * sin

    cos_q, sin_q = cq_ref[...], sq_ref[...]
    cos_k = jnp.concatenate([cp_ref[...], cos_q], axis=0)
    sin_k = jnp.concatenate([sp_ref[...], sin_q], axis=0)
    ri = lax.broadcasted_iota(I32, (w, 2 * w), 0)
    cj = lax.broadcasted_iota(I32, (w, 2 * w), 1)
    diff = w + ri - cj
    mask = (diff >= 0) & (diff < w) & ((cj >= w) | (n > 0))
    pairs = SWA_GROUP // 2

    for g in range(n_kv):
        slab = g // 2
        cols = slice(slab * LANES, (slab + 1) * LANES)
        k2 = rope(jnp.concatenate([kp_ref[:, cols], kc_ref[:, cols]], axis=0).astype(F32), cos_k, sin_k)
        v2 = jnp.concatenate([vp_ref[:, cols], vc_ref[:, cols]], axis=0).astype(F32)
        k2s, v2s = pltpu.roll(k2, HEAD_DIM_SWA, 1), pltpu.roll(v2, HEAD_DIM_SWA, 1)
        if g % 2 == 0:
            k_lo, v_lo = jnp.where(low_head, k2, 0.0), jnp.where(low_head, v2, 0.0)
            k_hi, v_hi = jnp.where(low_head, 0.0, k2s), jnp.where(low_head, 0.0, v2s)
        else:
            k_lo, v_lo = jnp.where(low_head, k2s, 0.0), jnp.where(low_head, v2s, 0.0)
            k_hi, v_hi = jnp.where(low_head, 0.0, k2), jnp.where(low_head, 0.0, v2)
        kv = ((k_lo.astype(BF16), v_lo.astype(BF16)), (k_hi.astype(BF16), v_hi.astype(BF16)))
        scores = []
        for p in range(pairs):
            qcols = slice((g * pairs + p) * LANES, (g * pairs + p + 1) * LANES)
            q = rope(q_ref[:, qcols].astype(F32), cos_q, sin_q) * (1.0 / math.sqrt(HEAD_DIM_SWA))
            q = q.astype(BF16)
            scores.append([lax.dot_general(q, kk, (((1,), (1,)), ((), ())), preferred_element_type=F32)
                           for kk, _ in kv])
        for p in range(pairs):
            qs = g * pairs + p
            qcols = slice(qs * LANES, (qs + 1) * LANES)
            o = jnp.zeros((w, LANES), F32)
            for hh, (kk, vv) in enumerate(kv):
                s = jnp.where(mask, scores[p][hh], NEG_INF)
                sink = sinks_ref[2 * qs + hh]
                m = jnp.maximum(jnp.max(s, axis=-1, keepdims=True), sink)
                e = jnp.exp(s - m)
                den = jnp.sum(e, axis=-1, keepdims=True) + jnp.exp(sink - m)
                o = o + jnp.dot(e.astype(BF16), vv, preferred_element_type=F32) / den
            o_ref[:, qcols] = o.astype(o_ref.dtype)


def _swa_attention(proj, sinks, cos_t, sin_t, bsz, seq, d):
    q_w, kv_w = d // 2, d // 16
    n_kv = kv_w // HEAD_DIM_SWA
    assert n_kv % 2 == 0 and q_w % kv_w == 0
    nb = seq // WINDOW
    k_col, v_col = q_w // kv_w, q_w // kv_w + 1
    cur = lambda b, n: b * nb + n
    prev = lambda b, n: b * nb + jnp.maximum(n - 1, 0)
    return pl.pallas_call(
        functools.partial(_swa_kernel, n_kv=n_kv),
        grid=(bsz, nb),
        in_specs=[pl.BlockSpec(memory_space=pltpu.SMEM),
                  pl.BlockSpec((WINDOW, q_w), lambda b, n: (cur(b, n), 0)),
                  pl.BlockSpec((WINDOW, kv_w), lambda b, n: (prev(b, n), k_col)),
                  pl.BlockSpec((WINDOW, kv_w), lambda b, n: (cur(b, n), k_col)),
                  pl.BlockSpec((WINDOW, kv_w), lambda b, n: (prev(b, n), v_col)),
                  pl.BlockSpec((WINDOW, kv_w), lambda b, n: (cur(b, n), v_col)),
                  pl.BlockSpec((WINDOW, LANES), lambda b, n: (n, 0)),
                  pl.BlockSpec((WINDOW, LANES), lambda b, n: (n, 0)),
                  pl.BlockSpec((WINDOW, LANES), lambda b, n: (jnp.maximum(n - 1, 0), 0)),
                  pl.BlockSpec((WINDOW, LANES), lambda b, n: (jnp.maximum(n - 1, 0), 0))],
        out_specs=pl.BlockSpec((WINDOW, q_w), lambda b, n: (cur(b, n), 0)),
        out_shape=jax.ShapeDtypeStruct((bsz * seq, q_w), BF16),
        compiler_params=_params(("parallel", "arbitrary"),
                                [((WINDOW, q_w), BF16)] * 2 + [((WINDOW, kv_w), BF16)] * 4,
                                [((2 * WINDOW, 2 * WINDOW), F32)] * 16),
        name="swa_attn",
    )(sinks, proj, proj, proj, proj, proj, cos_t, sin_t, cos_t, sin_t)


def _t5_thresholds():
    exact = REL_BUCKETS // 2
    d = np.arange(exact, 2 * REL_MAX_DIST, dtype=np.float64)
    large = exact + np.floor(np.log(d / exact) / math.log(REL_MAX_DIST / exact) * (REL_BUCKETS - exact)).astype(np.int64)
    large = np.minimum(large, REL_BUCKETS - 1)
    return [int(exact + np.argmax(large >= b)) for b in range(exact + 1, REL_BUCKETS)]


def _bias_tile_kernel(table_ref, o_ref):
    dd = pl.program_id(0)
    h = pl.program_id(1)
    exact = REL_BUCKETS // 2
    key = lax.broadcasted_iota(I32, (MOBA_BLOCK, MOBA_BLOCK), 0)
    qry = lax.broadcasted_iota(I32, (MOBA_BLOCK, MOBA_BLOCK), 1)
    dist = jnp.maximum(dd * MOBA_BLOCK + qry - key, 0)
    bucket = jnp.minimum(dist, exact)
    for thr in _t5_thresholds():
        bucket = bucket + (dist >= thr).astype(I32)
    val = jnp.full((MOBA_BLOCK, MOBA_BLOCK), table_ref[REL_BUCKETS - 1, h], F32)
    for b in range(REL_BUCKETS - 2, -1, -1):
        val = jnp.where(bucket == b, table_ref[b, h], val)
    o_ref[0, 0] = val * LOG2E


def _bias_tiles(rel_bias):
    n_heads = rel_bias.shape[1]
    return pl.pallas_call(
        _bias_tile_kernel,
        grid=(N_BIAS_TILES, n_heads),
        in_specs=[pl.BlockSpec(memory_space=pltpu.SMEM)],
        out_specs=pl.BlockSpec((1, 1, MOBA_BLOCK, MOBA_BLOCK), lambda dd, h: (h, dd, 0, 0)),
        out_shape=jax.ShapeDtypeStruct((n_heads, N_BIAS_TILES, MOBA_BLOCK, MOBA_BLOCK), F32),
        compiler_params=_params(("arbitrary", "arbitrary"), [((MOBA_BLOCK, MOBA_BLOCK), F32)],
                                [((MOBA_BLOCK, MOBA_BLOCK), F32)] * 4),
        name="moba_bias_tiles",
    )(rel_bias)


def _kmean_kernel(k_ref, o_ref, *, nblk):
    for j in range(nblk):
        rows = k_ref[j * MOBA_BLOCK:(j + 1) * MOBA_BLOCK, :].astype(F32)
        o_ref[j:j + 1, :] = jnp.mean(rows, axis=0, keepdims=True)


def _kmean(proj, bsz, seq, d):
    kv_w = d // 8
    nblk = seq // MOBA_BLOCK
    col = (9 * d // 8) // kv_w
    return pl.pallas_call(
        functools.partial(_kmean_kernel, nblk=nblk),
        grid=(bsz,),
        in_specs=[pl.BlockSpec((seq, kv_w), lambda b: (b, col))],
        out_specs=pl.BlockSpec((nblk, kv_w), lambda b: (b, 0)),
        out_shape=jax.ShapeDtypeStruct((bsz * nblk, kv_w), F32),
        compiler_params=_params(("parallel",), [((seq, kv_w), BF16)], [((MOBA_BLOCK, kv_w), F32)] * 2),
        name="moba_kmean",
    )(proj)


def _deinterleave_perm():
    perm = np.zeros((2 * LANES, 2 * LANES), np.float32)
    perm[2 * np.arange(LANES), np.arange(LANES)] = 1.0
    perm[2 * np.arange(LANES) + 1, LANES + np.arange(LANES)] = 1.0
    return jnp.asarray(perm, BF16)


def _deinterleave_block(w_ref, p_ref, wg_ref, wl_ref):
    perm = p_ref[...]
    group = perm.shape[0]
    half = group // 2
    for c in range(w_ref.shape[2] // group):
        t = jnp.dot(w_ref[0, :, c * group:(c + 1) * group].astype(BF16), perm, preferred_element_type=F32)
        wg_ref[0, :, c * half:(c + 1) * half] = t[:, :half].astype(BF16)
        wl_ref[0, :, c * half:(c + 1) * half] = t[:, half:].astype(BF16)


def _moba_kernel(q_ref, k_ref, v_ref, km_ref, bias_ref, w_ref, p_ref, o_ref, wg_ref, wl_ref, vt_ref, *, nblk):
    i = pl.program_id(2)
    mb, dh = MOBA_BLOCK, HEAD_DIM_MOBA
    scale = 1.0 / math.sqrt(dh)
    nt = (((1,), (1,)), ((), ()))
    blk = lax.broadcasted_iota(I32, (nblk, mb), 0)
    key = lax.broadcasted_iota(I32, (mb, mb), 0)
    qry = lax.broadcasted_iota(I32, (mb, mb), 1)
    causal = qry >= key

    @pl.when(i == 0)
    def _():
        for j in range(nblk):
            vt_ref[:dh, j * mb:(j + 1) * mb] = v_ref[j * mb:(j + 1) * mb, :].astype(F32).T.astype(BF16)
        vt_ref[dh:, :] = jnp.ones((vt_ref.shape[0] - dh, vt_ref.shape[1]), BF16)

    kmean = km_ref[...]
    km_hi = kmean.astype(BF16)
    rest = kmean - km_hi.astype(F32)
    km_mid = rest.astype(BF16)
    km_lo = (rest - km_mid.astype(F32)).astype(BF16)
    own = pl.multiple_of(i * mb, mb)
    k_own, vt_own = k_ref[pl.ds(own, mb), :], vt_ref[:, pl.ds(own, mb)]

    qs = [q_ref[:, hh * dh:(hh + 1) * dh] for hh in range(MOBA_GROUP)]
    gates = [lax.dot_general(km_hi, q, nt, preferred_element_type=F32)
             + lax.dot_general(km_mid, q, nt, preferred_element_type=F32)
             + lax.dot_general(km_lo, q, nt, preferred_element_type=F32) for q in qs]
    own_scores = [lax.dot_general(k_own, q, nt, preferred_element_type=F32) for q in qs]
    _deinterleave_block(w_ref, p_ref, wg_ref, wl_ref)
    picks, init = [], []
    for hh in range(MOBA_GROUP):
        gate = jnp.where(blk < i, gates[hh], NEG_INF)
        pick = []
        for t in range(MOBA_TOPK):
            gmax = jnp.max(gate, axis=0, keepdims=True)
            idx = jnp.min(jnp.where(gate == gmax, blk, nblk), axis=0, keepdims=True)
            pick.append(jnp.where(t < i, idx, -1))
            gate = jnp.where(blk == idx, NEG_INF, gate)

        s = jnp.where(causal, own_scores[hh] * (scale * LOG2E) + bias_ref[hh, 0], NEG_INF)
        m0 = jnp.max(s, axis=0, keepdims=True)
        a0 = jnp.dot(vt_own, jnp.exp2(s - m0).astype(BF16), preferred_element_type=F32)
        picks.append(pick)
        init.append((m0, a0))

    def past(jj, carry):
        ja = 2 * jj
        start = pl.multiple_of(ja * mb, mb)
        kj, vtj = k_ref[pl.ds(start, 2 * mb), :], vt_ref[:, pl.ds(start, 2 * mb)]
        tile_a = jnp.minimum(i - ja, N_BIAS_TILES - 1)
        tile_b = jnp.minimum(i - ja - 1, N_BIAS_TILES - 1)
        scores = [lax.dot_general(kj, qs[hh], nt, preferred_element_type=F32) for hh in range(MOBA_GROUP)]
        out = []
        for hh in range(MOBA_GROUP):
            m, acc = carry[hh]
            p0, p1, p2 = picks[hh]
            sa = scores[hh][:mb] * (scale * LOG2E) + bias_ref[hh, tile_a]
            sb = scores[hh][mb:] * (scale * LOG2E) + bias_ref[hh, tile_b]
            sa = jnp.where((p0 == ja) | (p1 == ja) | (p2 == ja), sa, NEG_INF)
            sb = jnp.where((p0 == ja + 1) | (p1 == ja + 1) | (p2 == ja + 1), sb, NEG_INF)
            m_new = jnp.maximum(m, jnp.maximum(jnp.max(sa, axis=0, keepdims=True),
                                               jnp.max(sb, axis=0, keepdims=True)))
            e = jnp.concatenate([jnp.exp2(sa - m_new), jnp.exp2(sb - m_new)], axis=0).astype(BF16)
            acc = jnp.exp2(m - m_new) * acc + jnp.dot(vtj, e, preferred_element_type=F32)
            out.append((m_new, acc))
        return tuple(out)

    final = lax.fori_loop(0, (i + 1) // 2, past, tuple(init))
    for hh in range(MOBA_GROUP):
        _, acc = final[hh]
        o_ref[:, hh * dh:(hh + 1) * dh] = (acc[:dh] / acc[dh:dh + 1]).T.astype(o_ref.dtype)


def _moba_attention(proj, kmean, bias_tiles, w_gu, bsz, seq, d):
    gw = MOBA_GROUP * HEAD_DIM_MOBA
    n_kv = (d // 8) // HEAD_DIM_MOBA
    nblk = seq // MOBA_BLOCK
    q_off, k_off, v_off = 5 * d // 8, 9 * d // 8, 10 * d // 8
    assert q_off % gw == 0 and MOBA_TOPK == 3 and nblk >= 2
    qc, kc, vc = q_off // gw, k_off // HEAD_DIM_MOBA, v_off // HEAD_DIM_MOBA
    tile_shape = (MOBA_GROUP, N_BIAS_TILES, MOBA_BLOCK, MOBA_BLOCK)
    n_exp, wd, f2 = w_gu.shape
    steps = n_kv * bsz * nblk
    tk = n_exp * wd // steps
    assert n_exp * wd % steps == 0 and wd % tk == 0 and tk % BF16_SUBLANES == 0
    per_exp = wd // tk
    wmap = lambda g, b, i: (((g * bsz + b) * nblk + i) // per_exp, ((g * bsz + b) * nblk + i) % per_exp, 0)
    w_out = jax.ShapeDtypeStruct((n_exp, wd, f2 // 2), BF16)
    return pl.pallas_call(
        functools.partial(_moba_kernel, nblk=nblk),
        grid=(n_kv, bsz, nblk),
        in_specs=[pl.BlockSpec((MOBA_BLOCK, gw), lambda g, b, i: (b * nblk + i, qc + g)),
                  pl.BlockSpec((seq, HEAD_DIM_MOBA), lambda g, b, i: (b, kc + g)),
                  pl.BlockSpec((seq, HEAD_DIM_MOBA), lambda g, b, i: (b, vc + g)),
                  pl.BlockSpec((nblk, HEAD_DIM_MOBA), lambda g, b, i: (b, g)),
                  pl.BlockSpec(tile_shape, lambda g, b, i: (g, 0, 0, 0)),
                  pl.BlockSpec((1, tk, f2), wmap),
                  pl.BlockSpec((2 * LANES, 2 * LANES), lambda g, b, i: (0, 0))],
        out_specs=[pl.BlockSpec((MOBA_BLOCK, gw), lambda g, b, i: (b * nblk + i, g)),
                   pl.BlockSpec((1, tk, f2 // 2), wmap), pl.BlockSpec((1, tk, f2 // 2), wmap)],
        out_shape=[jax.ShapeDtypeStruct((bsz * seq, d // 2), BF16), w_out, w_out],
        scratch_shapes=[pltpu.VMEM((HEAD_DIM_MOBA + BF16_SUBLANES, seq), BF16)],
        compiler_params=_params(("parallel", "parallel", "arbitrary"),
                                [(tile_shape, F32), ((seq, HEAD_DIM_MOBA), BF16), ((seq, HEAD_DIM_MOBA), BF16),
                                 ((MOBA_BLOCK, gw), BF16), ((MOBA_BLOCK, gw), BF16),
                                 ((tk, f2), F32), ((tk, f2), BF16)],
                                [((2 * MOBA_BLOCK, MOBA_BLOCK), F32)] * 8
                                + [((HEAD_DIM_MOBA + BF16_SUBLANES, seq), BF16)] + [((tk, 2 * LANES), F32)] * 4),
        name="moba_attn",
    )(proj, proj, proj, kmean, bias_tiles, w_gu, _deinterleave_perm())


def _merge_kernel(oa_ref, ob_ref, wa_ref, wb_ref, ga_ref, gb_ref, o_ref):
    ya = jnp.dot(oa_ref[...], wa_ref[...], preferred_element_type=F32)
    yb = jnp.dot(ob_ref[...], wb_ref[...], preferred_element_type=F32)
    merged = jax.nn.sigmoid(ga_ref[...].astype(F32)) * ya + jax.nn.sigmoid(gb_ref[...].astype(F32)) * yb
    o_ref[...] = merged.astype(o_ref.dtype)


def _merge(o_a, o_b, w_a, w_b, proj, d):
    t, kdim = o_a.shape
    tm, tn = _tile(t, 1024), _tile(d, 512)
    ga_col, gb_col = (11 * d // 8) // tn, (19 * d // 8) // tn
    assert (11 * d // 8) % tn == 0 and (19 * d // 8) % tn == 0
    return pl.pallas_call(
        _merge_kernel,
        grid=(t // tm, d // tn),
        in_specs=[pl.BlockSpec((tm, kdim), lambda i, j: (i, 0)),
                  pl.BlockSpec((tm, kdim), lambda i, j: (i, 0)),
                  pl.BlockSpec((kdim, tn), lambda i, j: (0, j)),
                  pl.BlockSpec((kdim, tn), lambda i, j: (0, j)),
                  pl.BlockSpec((tm, tn), lambda i, j: (i, ga_col + j)),
                  pl.BlockSpec((tm, tn), lambda i, j: (i, gb_col + j))],
        out_specs=pl.BlockSpec((tm, tn), lambda i, j: (i, j)),
        out_shape=jax.ShapeDtypeStruct((t, d), BF16),
        compiler_params=_params(("parallel", "arbitrary"),
                                [((tm, kdim), BF16)] * 2 + [((kdim, tn), BF16)] * 2 + [((tm, tn), BF16)] * 3,
                                [((tm, tn), F32)] * 3),
        name="merge",
    )(o_a, o_b, w_a, w_b, proj, proj)


def _post_mix_kernel(y_ref, x_ref, gpost_ref, gpre_ref, mod_ref, wrh_ref, wrl_ref, br_ref,
                     x1_ref, h2_ref, idx_ref, wts_ref):
    m = mod_ref[0]
    x1 = x_ref[...] + m[2:3, :] * _rms(y_ref[...], gpost_ref[...])
    x1_ref[...] = x1
    h2 = _rms(x1, gpre_ref[...]) * (1.0 + m[4:5, :]) + m[3:4, :]
    half = h2.shape[1] // 2
    h2_ref[...] = _pack_bf16_pair(h2[:, :half], h2[:, half:])
    logits = _dot_split(_split_bf16(h2), (wrh_ref[...], wrl_ref[...])) + br_ref[...]
    lane = lax.broadcasted_iota(I32, logits.shape, 1)
    logits = jnp.where(lane < N_EXPERTS, logits, NEG_INF)
    idx_out = jnp.zeros(logits.shape, I32)
    val_out = jnp.zeros(logits.shape, F32)
    top = None
    den = jnp.zeros((logits.shape[0], 1), F32)
    for k in range(TOP_K):
        vmax = jnp.max(logits, axis=-1, keepdims=True)
        idx = jnp.min(jnp.where(logits == vmax, lane, LANES), axis=-1, keepdims=True)
        top = vmax if top is None else top
        e = jnp.exp(vmax - top)
        den = den + e
        idx_out = jnp.where(lane == k, idx, idx_out)
        val_out = jnp.where(lane == k, e, val_out)
        logits = jnp.where(lane == idx, NEG_INF, logits)
    idx_ref[...] = idx_out
    wts_ref[...] = val_out / den


def _post_mix(y, x2, g_post, g_pre, mod, w_router, b_router, seq):
    t, d = x2.shape
    tm = _tile(seq, 256)
    wr_hi, wr_lo = _split_bf16(jnp.pad(w_router, ((0, 0), (0, LANES - N_EXPERTS))))
    br = jnp.pad(b_router, (0, LANES - N_EXPERTS)).reshape(1, LANES)
    row = lambda i: (i, 0)
    const = lambda i: (0, 0)
    return pl.pallas_call(
        _post_mix_kernel,
        grid=(t // tm,),
        in_specs=[pl.BlockSpec((tm, d), row), pl.BlockSpec((tm, d), row),
                  pl.BlockSpec((1, d), const), pl.BlockSpec((1, d), const),
                  pl.BlockSpec((1, 6, d), lambda i: (i * tm // seq, 0, 0)),
                  pl.BlockSpec((d, LANES), const), pl.BlockSpec((d, LANES), const),
                  pl.BlockSpec((1, LANES), const)],
        out_specs=[pl.BlockSpec((tm, d), row), pl.BlockSpec((tm, d // 2), row),
                   pl.BlockSpec((tm, LANES), row), pl.BlockSpec((tm, LANES), row)],
        out_shape=[jax.ShapeDtypeStruct((t, d), F32), jax.ShapeDtypeStruct((t, d // 2), jnp.uint32),
                   jax.ShapeDtypeStruct((t, LANES), I32), jax.ShapeDtypeStruct((t, LANES), F32)],
        compiler_params=_params(("parallel",), [((tm, d), F32)] * 4 + [((d, LANES), F32)],
                                [((tm, d), F32)] * 3),
        name="post_mix_router",
    )(y, x2, g_post.reshape(1, d), g_pre.reshape(1, d), mod, wr_hi, wr_lo, br)


def _routing(top_idx, n_blocks):
    flat_e = top_idx.reshape(-1)
    onehot = (flat_e[:, None] == jnp.arange(N_EXPERTS, dtype=I32)[None, :]).astype(I32)
    csum = jnp.cumsum(onehot, axis=0)
    counts = csum[-1]
    rank = jnp.sum(csum * onehot, axis=1) - 1
    padded = (counts + MOE_BLOCK - 1) // MOE_BLOCK * MOE_BLOCK
    pad_end = jnp.cumsum(padded)
    pad_start = pad_end - padded
    dest = jnp.sum(onehot * pad_start[None, :], axis=1) + rank
    n_real = (pad_end[-1] // MOE_BLOCK).astype(I32)
    blk_start = jnp.arange(n_blocks, dtype=I32) * MOE_BLOCK
    block_e = jnp.minimum(jnp.sum((pad_end[None, :] <= blk_start[:, None]).astype(I32), axis=1), N_EXPERTS - 1)
    last_blk = jnp.where(counts > 0, pad_end // MOE_BLOCK - 1, -1).astype(I32)
    return dest.astype(I32), block_e.astype(I32), n_real.reshape(1), last_blk


def _pack_bf16_pair(lo, hi):
    lo_bits = lax.bitcast_convert_type(lo.astype(BF16).astype(F32), jnp.uint32) >> 16
    hi_bits = lax.bitcast_convert_type(hi.astype(BF16).astype(F32), jnp.uint32) & jnp.uint32(0xFFFF0000)
    return hi_bits | lo_bits


def _unpack_bf16_pair(p):
    lo = lax.bitcast_convert_type(p << 16, F32).astype(BF16)
    hi = lax.bitcast_convert_type(p & jnp.uint32(0xFFFF0000), F32).astype(BF16)
    return lo, hi


def _dispatch_kernel(last_ref, nr_ref, dest_ref, h_ref, xg_hbm, zeros, zsem, sem, *, tokens, n_blocks):
    def zero_block(m):
        rows = pl.ds(pl.multiple_of(m * MOE_BLOCK, MOE_BLOCK), MOE_BLOCK)
        return pltpu.make_async_copy(zeros, xg_hbm.at[rows], zsem)

    @pl.when(pl.program_id(0) == 0)
    def _():
        zeros[...] = jnp.zeros(zeros.shape, zeros.dtype)
        for phase in ("start", "wait"):
            for e in range(N_EXPERTS):
                @pl.when(last_ref[e] >= 0)
                def _(e=e, phase=phase):
                    getattr(zero_block(last_ref[e]), phase)()

            def tail(m, carry, phase=phase):
                getattr(zero_block(m), phase)()
                return carry

            lax.fori_loop(nr_ref[0], n_blocks, tail, 0)

    def row_copy(t, row):
        return pltpu.make_async_copy(h_ref.at[pl.ds(t, 1)], xg_hbm.at[pl.ds(row, 1)], sem)

    def issue(t, carry):
        for k in range(TOP_K):
            row_copy(t, dest_ref[0, 0, t * TOP_K + k]).start()
        return carry

    lax.fori_loop(0, tokens, issue, 0)

    def drain(t, carry):
        for k in range(TOP_K):
            row_copy(0, 0).wait()
        return carry

    lax.fori_loop(0, tokens, drain, 0)


def _dispatch(h2p, dest, last_blk, n_real, n_blocks):
    t, dp = h2p.shape
    tokens = _tile(t, 256)
    steps = t // tokens
    return pl.pallas_call(
        functools.partial(_dispatch_kernel, tokens=tokens, n_blocks=n_blocks),
        grid_spec=pltpu.PrefetchScalarGridSpec(
            num_scalar_prefetch=2, grid=(steps,),
            in_specs=[pl.BlockSpec((1, 1, tokens * TOP_K), lambda i, last, nr: (i, 0, 0), memory_space=pltpu.SMEM),
                      pl.BlockSpec((tokens, dp), lambda i, last, nr: (i, 0))],
            out_specs=pl.BlockSpec(memory_space=pl.ANY),
            scratch_shapes=[pltpu.VMEM((MOE_BLOCK, dp), h2p.dtype),
                            pltpu.SemaphoreType.DMA(()), pltpu.SemaphoreType.DMA(())]),
        out_shape=jax.ShapeDtypeStruct((n_blocks * MOE_BLOCK, dp), h2p.dtype),
        compiler_params=_params(("arbitrary",), [((tokens, dp), h2p.dtype)], [((MOE_BLOCK, dp), h2p.dtype)]),
        name="moe_dispatch",
    )(last_blk, n_real, dest.reshape(steps, 1, tokens * TOP_K), h2p)


def _gate_up_kernel(be_ref, nr_ref, x_ref, wg_ref, wl_ref, bg_ref, bl_ref, wd_ref, o_ref, wdb_ref, *, n_cast):
    real = pl.program_id(1) < nr_ref[0]

    @pl.when(jnp.logical_and(pl.program_id(0) == 0, pl.program_id(1) < n_cast))
    def _():
        wdb_ref[...] = wd_ref[...].astype(BF16)

    @pl.when(real)
    def _():
        lo, hi = _unpack_bf16_pair(x_ref[...])
        half = lo.shape[1]

        def proj(w_ref, b_ref):
            return (jnp.dot(lo, w_ref[0, :half, :], preferred_element_type=F32)
                    + jnp.dot(hi, w_ref[0, half:, :], preferred_element_type=F32) + b_ref[0])

        glu = jnp.minimum(proj(wg_ref, bg_ref), SWIGLU_LIMIT)
        lin = jnp.clip(proj(wl_ref, bl_ref), -SWIGLU_LIMIT, SWIGLU_LIMIT)
        o_ref[...] = (glu * jax.nn.sigmoid(SWIGLU_ALPHA * glu) * (lin + 1.0)).astype(o_ref.dtype)

    @pl.when(jnp.logical_not(real))
    def _():
        o_ref[...] = jnp.zeros(o_ref.shape, o_ref.dtype)


def _gate_up(xg, w_g, w_l, b_g, b_l, w_d, block_e, n_real, n_blocks):
    rows, dp = xg.shape
    d, f = w_g.shape[1], w_g.shape[2]
    passes = 2
    tf = f // passes
    blk = lambda p, m, be, nr: jnp.minimum(m, nr[0] - 1)
    wmap = lambda p, m, be, nr: (be[blk(p, m, be, nr)], 0, p)
    n_exp, wf, wn = w_d.shape
    tr = min(r for r in range(BF16_SUBLANES, wf + 1, BF16_SUBLANES)
             if wf % r == 0 and n_exp * wf // r <= n_blocks)
    per_exp = wf // tr
    n_cast = n_exp * per_exp

    def cmap(p, m, be, nr):
        s = jnp.where(p == 0, jnp.minimum(m, n_cast - 1), n_cast - 1)
        return (s // per_exp, s % per_exp, 0)

    return pl.pallas_call(
        functools.partial(_gate_up_kernel, n_cast=n_cast),
        grid_spec=pltpu.PrefetchScalarGridSpec(
            num_scalar_prefetch=2, grid=(passes, n_blocks),
            in_specs=[pl.BlockSpec((MOE_BLOCK, dp), lambda p, m, be, nr: (blk(p, m, be, nr), 0)),
                      pl.BlockSpec((1, d, tf), wmap), pl.BlockSpec((1, d, tf), wmap),
                      pl.BlockSpec((1, 1, tf), wmap), pl.BlockSpec((1, 1, tf), wmap),
                      pl.BlockSpec((1, tr, wn), cmap)],
            out_specs=[pl.BlockSpec((MOE_BLOCK, tf), lambda p, m, be, nr: (m, p)),
                       pl.BlockSpec((1, tr, wn), cmap)]),
        out_shape=[jax.ShapeDtypeStruct((rows, f), BF16), jax.ShapeDtypeStruct(w_d.shape, BF16)],
        compiler_params=_params(("arbitrary", "arbitrary"),
                                [((MOE_BLOCK, dp), xg.dtype), ((d, tf), BF16), ((d, tf), BF16),
                                 ((MOE_BLOCK, tf), BF16), ((tr, wn), F32), ((tr, wn), BF16)],
                                [((MOE_BLOCK, d), BF16)] + [((MOE_BLOCK, tf), F32)] * 4),
        name="moe_gate_up",
    )(block_e, n_real, xg, w_g, w_l, b_g, b_l, w_d)


def _down_kernel(be_ref, nr_ref, a_ref, w_ref, b_ref, o_ref):
    real = pl.program_id(0) < nr_ref[0]

    @pl.when(real)
    def _():
        y = jnp.dot(a_ref[...], w_ref[0], preferred_element_type=F32) + b_ref[0]
        half = y.shape[1] // 2
        o_ref[...] = _pack_bf16_pair(y[:, :half], y[:, half:])

    @pl.when(jnp.logical_not(real))
    def _():
        o_ref[...] = jnp.zeros(o_ref.shape, o_ref.dtype)


def _down(act, w_d, b_d, block_e, n_real, n_blocks):
    rows, f = act.shape
    d = w_d.shape[2]
    blk = lambda m, be, nr: jnp.minimum(m, nr[0] - 1)
    wmap = lambda m, be, nr: (be[blk(m, be, nr)], 0, 0)
    return pl.pallas_call(
        _down_kernel,
        grid_spec=pltpu.PrefetchScalarGridSpec(
            num_scalar_prefetch=2, grid=(n_blocks,),
            in_specs=[pl.BlockSpec((MOE_BLOCK, f), lambda m, be, nr: (m, 0)),
                      pl.BlockSpec((1, f, d), wmap), pl.BlockSpec((1, 1, d), wmap)],
            out_specs=pl.BlockSpec((MOE_BLOCK, d // 2), lambda m, be, nr: (m, 0))),
        out_shape=jax.ShapeDtypeStruct((rows, d // 2), jnp.uint32),
        compiler_params=_params(("arbitrary",),
                                [((MOE_BLOCK, f), BF16), ((f, d), BF16), ((MOE_BLOCK, d // 2), jnp.uint32)],
                                [((MOE_BLOCK, d), F32)] * 2),
        name="moe_down",
    )(block_e, n_real, act, w_d, b_d)


def _combine_kernel(dcur_ref, dnext_ref, w_ref, x1_ref, g_ref, mod_ref, y_hbm, o_ref, buf, sem,
                    *, tokens, steps):
    step = pl.program_id(0)
    slot = step % 2

    def row_copy(slot_, k, t, row):
        return pltpu.make_async_copy(y_hbm.at[pl.ds(row, 1)], buf.at[slot_, k, pl.ds(t, 1)], sem.at[slot_])

    def start_tile(dest_ref, slot_):
        def body(t, carry):
            for k in range(TOP_K):
                row_copy(slot_, k, t, dest_ref[0, 0, t * TOP_K + k]).start()
            return carry
        lax.fori_loop(0, tokens, body, 0)

    @pl.when(step == 0)
    def _():
        start_tile(dcur_ref, 0)

    @pl.when(step + 1 < steps)
    def _():
        start_tile(dnext_ref, 1 - slot)

    def drain(t, carry):
        for k in range(TOP_K):
            row_copy(slot, k, 0, 0).wait()
        return carry

    lax.fori_loop(0, tokens, drain, 0)

    w = w_ref[...]
    acc_lo = acc_hi = None
    for k in range(TOP_K):
        word = buf[slot, k]
        lo = lax.bitcast_convert_type(word << 16, F32) * w[:, k:k + 1]
        hi = lax.bitcast_convert_type(word & jnp.uint32(0xFFFF0000), F32) * w[:, k:k + 1]
        acc_lo = lo if acc_lo is None else acc_lo + lo
        acc_hi = hi if acc_hi is None else acc_hi + hi
    acc = jnp.concatenate([acc_lo, acc_hi], axis=1)
    m = mod_ref[0]
    o_ref[...] = x1_ref[...] + m[5:6, :] * _rms(acc, g_ref[...])


def _combine(y_sorted, dest, top_w, x1, g_post, mod, seq):
    t, d = x1.shape
    dp = y_sorted.shape[1]
    tokens = _tile(seq, 256)
    steps = t // tokens
    row = lambda i: (i, 0)
    dest3 = dest.reshape(steps, 1, tokens * TOP_K)
    return pl.pallas_call(
        functools.partial(_combine_kernel, tokens=tokens, steps=steps),
        grid=(steps,),
        in_specs=[pl.BlockSpec((1, 1, tokens * TOP_K), lambda i: (i, 0, 0), memory_space=pltpu.SMEM),
                  pl.BlockSpec((1, 1, tokens * TOP_K), lambda i: (jnp.minimum(i + 1, steps - 1), 0, 0),
                               memory_space=pltpu.SMEM),
                  pl.BlockSpec((tokens, LANES), row),
                  pl.BlockSpec((tokens, d), row),
                  pl.BlockSpec((1, d), lambda i: (0, 0)),
                  pl.BlockSpec((1, 6, d), lambda i: (i * tokens // seq, 0, 0)),
                  pl.BlockSpec(memory_space=pl.ANY)],
        out_specs=pl.BlockSpec((tokens, d), row),
        out_shape=jax.ShapeDtypeStruct((t, d), F32),
        scratch_shapes=[pltpu.VMEM((2, TOP_K, tokens, dp), y_sorted.dtype), pltpu.SemaphoreType.DMA((2,))],
        compiler_params=_params(("arbitrary",), [((tokens, d), F32)] * 2,
                                [((2, TOP_K, tokens, dp), y_sorted.dtype)] + [((tokens, d), F32)] * 3),
        name="moe_combine",
    )(dest3, dest3, top_w, x1, g_post.reshape(1, d), mod, y_sorted)


def _moe(h2, top_idx, top_w, x1, g_post, mod, w_g, w_l, b_gu, w_dn, b_dn, seq):
    t, d = x1.shape
    f = w_dn.shape[1]
    n_blocks = -(-(t * TOP_K) // MOE_BLOCK) + N_EXPERTS
    dest, block_e, n_real, last_blk = _routing(top_idx, n_blocks)
    xg = _dispatch(h2, dest, last_blk, n_real, n_blocks)
    b_g = b_gu[:, 0::2].reshape(N_EXPERTS, 1, f)
    b_l = b_gu[:, 1::2].reshape(N_EXPERTS, 1, f)
    act, w_dn_bf16 = _gate_up(xg, w_g, w_l, b_g, b_l, w_dn, block_e, n_real, n_blocks)
    y_sorted = _down(act, w_dn_bf16, b_dn.reshape(N_EXPERTS, 1, d), block_e, n_real, n_blocks)
    return _combine(y_sorted, dest, top_w, x1, g_post, mod, seq)


def kernel(x, c, w_ada, b_ada, g_pre_mix, g_post_mix, g_pre_ffn, g_post_ffn, w_in, b_in, sinks, rel_bias,
           w_o_swa, w_o_moba, w_out, w_router, b_router, w_gate_up, b_gate_up, w_down, b_down):
    bsz, seq, d = x.shape
    depth = w_ada.shape[0]
    assert seq % MOBA_BLOCK == 0 and d % 1024 == 0
    x2 = x.reshape(bsz * seq, d)
    cos_t, sin_t = _rope_tables(seq)
    bias_tiles = _bias_tiles(rel_bias)
    for l in range(depth):
        mod = _ada_mod(c, w_ada[l], b_ada[l])
        h = _prenorm(x2, g_pre_mix[l], mod, seq, shift_row=0, scale_row=1)
        proj = _matmul(h, w_in[l].astype(BF16), b_in[l], BF16, 1024, 768, "in_proj")
        o_a = _swa_attention(proj, sinks[l], cos_t, sin_t, bsz, seq, d)
        o_b, w_g, w_l = _moba_attention(proj, _kmean(proj, bsz, seq, d), bias_tiles, w_gate_up[l], bsz, seq, d)
        merged = _merge(o_a, o_b, w_o_swa[l].astype(BF16), w_o_moba[l].astype(BF16), proj, d)
        y = _matmul(merged, w_out[l].astype(BF16), jnp.zeros((d,), F32), F32, 1024, 1024, "out_proj")
        x1, h2, idx, wts = _post_mix(y, x2, g_post_mix[l], g_pre_ffn[l], mod, w_router[l], b_router[l], seq)
        x2 = _moe(h2, idx[:, :TOP_K], wts, x1, g_post_ffn[l], mod,
                  w_g, w_l, b_gate_up[l], w_down[l], b_down[l], seq)
    return x2.reshape(bsz, seq, d)
```

```python
import functools
import math

import numpy as np
import jax
import jax.numpy as jnp
from jax import lax
from jax.experimental import pallas as pl
from jax.experimental.pallas import tpu as pltpu

F32 = jnp.float32
BF16 = jnp.bfloat16
I32 = jnp.int32

HEAD_DIM_SWA = 64
SWA_GROUP = 8
WINDOW = 128
ROPE_THETA = 150000.0
HEAD_DIM_MOBA = 128
MOBA_GROUP = 4
MOBA_BLOCK = 256
MOBA_TOPK = 3
REL_BUCKETS = 32
REL_MAX_DIST = 1024
N_EXPERTS = 32
TOP_K = 4
SWIGLU_LIMIT = 7.0
SWIGLU_ALPHA = 1.702
MOE_BLOCK = 256
RMS_EPS = 1e-6

LANES = 128
SUBLANES = 8
BF16_SUBLANES = 16
VMEM_BYTES_V7X = 64 * 1024 * 1024
VMEM_CAP = VMEM_BYTES_V7X - 8 * 1024 * 1024

NEG_INF = float("-inf")
LOG2E = 1.0 / math.log(2.0)
N_BIAS_TILES = (REL_MAX_DIST + MOBA_BLOCK - 1) // MOBA_BLOCK + 2


def _nbytes(shape, dtype):
    return int(np.prod(shape)) * jnp.dtype(dtype).itemsize


def _params(semantics, blocks, temps=()):
    need = 2 * sum(_nbytes(s, d) for s, d in blocks) + sum(_nbytes(s, d) for s, d in temps)
    need = max(need + need // 4, 16 * 1024 * 1024)
    return pltpu.CompilerParams(dimension_semantics=semantics, vmem_limit_bytes=min(need, VMEM_CAP))


def _tile(n, pref):
    t = min(n, pref)
    assert n % t == 0, (n, pref)
    return t


def _split_bf16(x):
    hi = x.astype(BF16)
    return hi, (x - hi.astype(F32)).astype(BF16)


def _dot_split(a, w):
    (a_hi, a_lo), (w_hi, w_lo) = a, w
    return (jnp.dot(a_hi, w_hi, preferred_element_type=F32)
            + (jnp.dot(a_hi, w_lo, preferred_element_type=F32) + jnp.dot(a_lo, w_hi, preferred_element_type=F32)))


def _ada_kernel(c_ref, w_ref, b_ref, o_ref):
    c = c_ref[...]
    s = c * jax.nn.sigmoid(c)
    o_ref[...] = _dot_split(_split_bf16(s), _split_bf16(w_ref[...])) + b_ref[...]


def _ada_mod(c, w, b):
    bsz, d = c.shape
    n = w.shape[1]
    rows = -(-bsz // SUBLANES) * SUBLANES
    cp = jnp.pad(c, ((0, rows - bsz), (0, 0)))
    tn = _tile(n, 512)
    out = pl.pallas_call(
        _ada_kernel,
        grid=(n // tn,),
        in_specs=[pl.BlockSpec((rows, d), lambda j: (0, 0)),
                  pl.BlockSpec((d, tn), lambda j: (0, j)),
                  pl.BlockSpec((1, tn), lambda j: (0, j))],
        out_specs=pl.BlockSpec((rows, tn), lambda j: (0, j)),
        out_shape=jax.ShapeDtypeStruct((rows, n), F32),
        compiler_params=_params(("arbitrary",), [((d, tn), F32), ((rows, d), F32)], [((d, tn), F32)]),
        name="ada_mod",
    )(cp, w, b.reshape(1, n))
    return out[:bsz].reshape(bsz, 6, d)


def _rms(x, g):
    return x * lax.rsqrt(jnp.mean(x * x, axis=-1, keepdims=True) + RMS_EPS) * g


def _prenorm_kernel(x_ref, g_ref, mod_ref, o_ref, *, shift_row, scale_row):
    m = mod_ref[0]
    y = _rms(x_ref[...], g_ref[...])
    o_ref[...] = (y * (1.0 + m[scale_row:scale_row + 1, :]) + m[shift_row:shift_row + 1, :]).astype(o_ref.dtype)


def _prenorm(x2, g, mod, seq, shift_row, scale_row):
    t, d = x2.shape
    tm = _tile(seq, 256)
    return pl.pallas_call(
        functools.partial(_prenorm_kernel, shift_row=shift_row, scale_row=scale_row),
        grid=(t // tm,),
        in_specs=[pl.BlockSpec((tm, d), lambda i: (i, 0)),
                  pl.BlockSpec((1, d), lambda i: (0, 0)),
                  pl.BlockSpec((1, 6, d), lambda i: (i * tm // seq, 0, 0))],
        out_specs=pl.BlockSpec((tm, d), lambda i: (i, 0)),
        out_shape=jax.ShapeDtypeStruct((t, d), BF16),
        compiler_params=_params(("parallel",), [((tm, d), F32), ((tm, d), BF16)], [((tm, d), F32)] * 2),
        name="prenorm",
    )(x2, g.reshape(1, d), mod)


def _mm_kernel(a_ref, w_ref, b_ref, o_ref):
    acc = jnp.dot(a_ref[...], w_ref[...], preferred_element_type=F32)
    o_ref[...] = (acc + b_ref[...]).astype(o_ref.dtype)


def _matmul(a, w, bias, out_dtype, tm, tn, name):
    m, k = a.shape
    n = w.shape[1]
    tm, tn = _tile(m, tm), _tile(n, tn)
    return pl.pallas_call(
        _mm_kernel,
        grid=(m // tm, n // tn),
        in_specs=[pl.BlockSpec((tm, k), lambda i, j: (i, 0)),
                  pl.BlockSpec((k, tn), lambda i, j: (0, j)),
                  pl.BlockSpec((1, tn), lambda i, j: (0, j))],
        out_specs=pl.BlockSpec((tm, tn), lambda i, j: (i, j)),
        out_shape=jax.ShapeDtypeStruct((m, n), out_dtype),
        compiler_params=_params(("parallel", "arbitrary"),
                                [((tm, k), a.dtype), ((k, tn), w.dtype), ((tm, tn), out_dtype)],
                                [((tm, tn), F32)]),
        name=name,
    )(a, w, bias.reshape(1, n).astype(F32))


def _rope_tables(seq):
    half = HEAD_DIM_SWA // 2
    inv = ROPE_THETA ** (-jnp.arange(half, dtype=F32) / half)
    ang = jnp.arange(seq, dtype=F32)[:, None] * inv[None, :]
    cos, sin = jnp.cos(ang), jnp.sin(ang)
    cos_t = jnp.concatenate([cos, cos, cos, cos], axis=-1)
    sin_t = jnp.concatenate([-sin, sin, -sin, sin], axis=-1)
    return cos_t, sin_t


def _swa_kernel(sinks_ref, q_ref, kp_ref, kc_ref, vp_ref, vc_ref, cq_ref, sq_ref, cp_ref, sp_ref, o_ref,
                *, n_kv):
    n = pl.program_id(1)
    w = WINDOW
    half = HEAD_DIM_SWA // 2
    lane = lax.broadcasted_iota(I32, (1, LANES), 1)
    first_half = (lane % HEAD_DIM_SWA) < half
    low_head = lane < HEAD_DIM_SWA

    def rope(xf, cos, sin):
        partner = jnp.where(first_half, pltpu.roll(xf, LANES - half, 1), pltpu.roll(xf, half, 1))
        return xf * cos + partner * sin

    cos_q, sin_q = cq_ref[...], sq_ref[...]
    cos_k = jnp.concatenate([cp_ref[...], cos_q], axis=0)
    sin_k = jnp.concatenate([sp_ref[...], sin_q], axis=0)
    ri = lax.broadcasted_iota(I32, (w, 2 * w), 0)
    cj = lax.broadcasted_iota(I32, (w, 2 * w), 1)
    diff = w + ri - cj
    mask = (diff >= 0) & (diff < w) & ((cj >= w) | (n > 0))
    pairs = SWA_GROUP // 2

    for g in range(n_kv):
        slab = g // 2
        cols = slice(slab * LANES, (slab + 1) * LANES)
        k2 = rope(jnp.concatenate([kp_ref[:, cols], kc_ref[:, cols]], axis=0).astype(F32), cos_k, sin_k)
        v2 = jnp.concatenate([vp_ref[:, cols], vc_ref[:, cols]], axis=0).astype(F32)
        k2s, v2s = pltpu.roll(k2, HEAD_DIM_SWA, 1), pltpu.roll(v2, HEAD_DIM_SWA, 1)
        if g % 2 == 0:
            k_lo, v_lo = jnp.where(low_head, k2, 0.0), jnp.where(low_head, v2, 0.0)
            k_hi, v_hi = jnp.where(low_head, 0.0, k2s), jnp.where(low_head, 0.0, v2s)
        else:
            k_lo, v_lo = jnp.where(low_head, k2s, 0.0), jnp.where(low_head, v2s, 0.0)
            k_hi, v_hi = jnp.where(low_head, 0.0, k2), jnp.where(low_head, 0.0, v2)
        kv = ((k_lo.astype(BF16), v_lo.astype(BF16)), (k_hi.astype(BF16), v_hi.astype(BF16)))
        scores = []
        for p in range(pairs):
            qcols = slice((g * pairs + p) * LANES, (g * pairs + p + 1) * LANES)
            q = rope(q_ref[:, qcols].astype(F32), cos_q, sin_q) * (1.0 / math.sqrt(HEAD_DIM_SWA))
            q = q.astype(BF16)
            scores.append([lax.dot_general(q, kk, (((1,), (1,)), ((), ())), preferred_element_type=F32)
                           for kk, _ in kv])
        for p in range(pairs):
            qs = g * pairs + p
            qcols = slice(qs * LANES, (qs + 1) * LANES)
            o = jnp.zeros((w, LANES), F32)
            for hh, (kk, vv) in enumerate(kv):
                s = jnp.where(mask, scores[p][hh], NEG_INF)
                sink = sinks_ref[2 * qs + hh]
                m = jnp.maximum(jnp.max(s, axis=-1, keepdims=True), sink)
                e = jnp.exp(s - m)
                den = jnp.sum(e, axis=-1, keepdims=True) + jnp.exp(sink - m)
                o = o + jnp.dot(e.astype(BF16), vv, preferred_element_type=F32) / den
            o_ref[:, qcols] = o.astype(o_ref.dtype)


def _swa_attention(proj, sinks, cos_t, sin_t, bsz, seq, d):
    q_w, kv_w = d // 2, d // 16
    n_kv = kv_w // HEAD_DIM_SWA
    assert n_kv % 2 == 0 and q_w % kv_w == 0
    nb = seq // WINDOW
    k_col, v_col = q_w // kv_w, q_w // kv_w + 1
    cur = lambda b, n: b * nb + n
    prev = lambda b, n: b * nb + jnp.maximum(n - 1, 0)
    return pl.pallas_call(
        functools.partial(_swa_kernel, n_kv=n_kv),
        grid=(bsz, nb),
        in_specs=[pl.BlockSpec(memory_space=pltpu.SMEM),
                  pl.BlockSpec((WINDOW, q_w), lambda b, n: (cur(b, n), 0)),
                  pl.BlockSpec((WINDOW, kv_w), lambda b, n: (prev(b, n), k_col)),
                  pl.BlockSpec((WINDOW, kv_w), lambda b, n: (cur(b, n), k_col)),
                  pl.BlockSpec((WINDOW, kv_w), lambda b, n: (prev(b, n), v_col)),
                  pl.BlockSpec((WINDOW, kv_w), lambda b, n: (cur(b, n), v_col)),
                  pl.BlockSpec((WINDOW, LANES), lambda b, n: (n, 0)),
                  pl.BlockSpec((WINDOW, LANES), lambda b, n: (n, 0)),
                  pl.BlockSpec((WINDOW, LANES), lambda b, n: (jnp.maximum(n - 1, 0), 0)),
                  pl.BlockSpec((WINDOW, LANES), lambda b, n: (jnp.maximum(n - 1, 0), 0))],
        out_specs=pl.BlockSpec((WINDOW, q_w), lambda b, n: (cur(b, n), 0)),
        out_shape=jax.ShapeDtypeStruct((bsz * seq, q_w), BF16),
        compiler_params=_params(("parallel", "arbitrary"),
                                [((WINDOW, q_w), BF16)] * 2 + [((WINDOW, kv_w), BF16)] * 4,
                                [((2 * WINDOW, 2 * WINDOW), F32)] * 16),
        name="swa_attn",
    )(sinks, proj, proj, proj, proj, proj, cos_t, sin_t, cos_t, sin_t)


def _t5_thresholds():
    exact = REL_BUCKETS // 2
    d = np.arange(exact, 2 * REL_MAX_DIST, dtype=np.float64)
    large = exact + np.floor(np.log(d / exact) / math.log(REL_MAX_DIST / exact) * (REL_BUCKETS - exact)).astype(np.int64)
    large = np.minimum(large, REL_BUCKETS - 1)
    return [int(exact + np.argmax(large >= b)) for b in range(exact + 1, REL_BUCKETS)]


def _bias_tile_kernel(table_ref, o_ref):
    dd = pl.program_id(0)
    exact = REL_BUCKETS // 2
    key = lax.broadcasted_iota(I32, (MOBA_BLOCK, MOBA_BLOCK), 0)
    qry = lax.broadcasted_iota(I32, (MOBA_BLOCK, MOBA_BLOCK), 1)
    dist = jnp.maximum(dd * MOBA_BLOCK + qry - key, 0)
    bucket = jnp.minimum(dist, exact)
    for thr in _t5_thresholds():
        bucket = bucket + (dist >= thr).astype(I32)
    for h in range(o_ref.shape[0]):
        val = jnp.full((MOBA_BLOCK, MOBA_BLOCK), table_ref[REL_BUCKETS - 1, h], F32)
        for b in range(REL_BUCKETS - 2, -1, -1):
            val = jnp.where(bucket == b, table_ref[b, h], val)
        o_ref[h, 0] = val * LOG2E


def _bias_tiles(rel_bias):
    n_heads = rel_bias.shape[1]
    return pl.pallas_call(
        _bias_tile_kernel,
        grid=(N_BIAS_TILES,),
        in_specs=[pl.BlockSpec(memory_space=pltpu.SMEM)],
        out_specs=pl.BlockSpec((n_heads, 1, MOBA_BLOCK, MOBA_BLOCK), lambda dd: (0, dd, 0, 0)),
        out_shape=jax.ShapeDtypeStruct((n_heads, N_BIAS_TILES, MOBA_BLOCK, MOBA_BLOCK), F32),
        compiler_params=_params(("arbitrary",), [((n_heads, MOBA_BLOCK, MOBA_BLOCK), F32)],
                                [((MOBA_BLOCK, MOBA_BLOCK), F32)] * 4),
        name="moba_bias_tiles",
    )(rel_bias)


def _kmean_kernel(k_ref, o_ref, *, nblk):
    for j in range(nblk):
        rows = k_ref[j * MOBA_BLOCK:(j + 1) * MOBA_BLOCK, :].astype(F32)
        o_ref[j:j + 1, :] = jnp.mean(rows, axis=0, keepdims=True)


def _kmean(proj, bsz, seq, d):
    kv_w = d // 8
    nblk = seq // MOBA_BLOCK
    col = (9 * d // 8) // kv_w
    return pl.pallas_call(
        functools.partial(_kmean_kernel, nblk=nblk),
        grid=(bsz,),
        in_specs=[pl.BlockSpec((seq, kv_w), lambda b: (b, col))],
        out_specs=pl.BlockSpec((nblk, kv_w), lambda b: (b, 0)),
        out_shape=jax.ShapeDtypeStruct((bsz * nblk, kv_w), F32),
        compiler_params=_params(("parallel",), [((seq, kv_w), BF16)], [((MOBA_BLOCK, kv_w), F32)] * 2),
        name="moba_kmean",
    )(proj)


def _deinterleave_perm():
    perm = np.zeros((2 * LANES, 2 * LANES), np.float32)
    perm[2 * np.arange(LANES), np.arange(LANES)] = 1.0
    perm[2 * np.arange(LANES) + 1, LANES + np.arange(LANES)] = 1.0
    return jnp.asarray(perm, BF16)


def _deinterleave_block(w_ref, p_ref, wg_ref, wl_ref):
    perm = p_ref[...]
    group = perm.shape[0]
    half = group // 2
    for c in range(w_ref.shape[2] // group):
        t = jnp.dot(w_ref[0, :, c * group:(c + 1) * group].astype(BF16), perm, preferred_element_type=F32)
        wg_ref[0, :, c * half:(c + 1) * half] = t[:, :half].astype(BF16)
        wl_ref[0, :, c * half:(c + 1) * half] = t[:, half:].astype(BF16)


def _moba_kernel(q_ref, k_ref, v_ref, km_ref, bias_ref, w_ref, p_ref, o_ref, wg_ref, wl_ref, vt_ref, *, nblk):
    i = pl.program_id(2)
    mb, dh = MOBA_BLOCK, HEAD_DIM_MOBA
    scale = 1.0 / math.sqrt(dh)
    nt = (((1,), (1,)), ((), ()))
    blk = lax.broadcasted_iota(I32, (nblk, mb), 0)
    key = lax.broadcasted_iota(I32, (mb, mb), 0)
    qry = lax.broadcasted_iota(I32, (mb, mb), 1)
    causal = qry >= key

    @pl.when(i == 0)
    def _():
        for j in range(nblk):
            vt_ref[:dh, j * mb:(j + 1) * mb] = v_ref[j * mb:(j + 1) * mb, :].astype(F32).T.astype(BF16)
        vt_ref[dh:, :] = jnp.ones((vt_ref.shape[0] - dh, vt_ref.shape[1]), BF16)

    kmean = km_ref[...]
    km_hi = kmean.astype(BF16)
    rest = kmean - km_hi.astype(F32)
    km_mid = rest.astype(BF16)
    km_lo = (rest - km_mid.astype(F32)).astype(BF16)
    own = pl.multiple_of(i * mb, mb)
    k_own, vt_own = k_ref[pl.ds(own, mb), :], vt_ref[:, pl.ds(own, mb)]

    qs = [q_ref[:, hh * dh:(hh + 1) * dh] for hh in range(MOBA_GROUP)]
    gates = [lax.dot_general(km_hi, q, nt, preferred_element_type=F32)
             + lax.dot_general(km_mid, q, nt, preferred_element_type=F32)
             + lax.dot_general(km_lo, q, nt, preferred_element_type=F32) for q in qs]
    own_scores = [lax.dot_general(k_own, q, nt, preferred_element_type=F32) for q in qs]
    _deinterleave_block(w_ref, p_ref, wg_ref, wl_ref)
    picks, init = [], []
    for hh in range(MOBA_GROUP):
        gate = jnp.where(blk < i, gates[hh], NEG_INF)
        pick = []
        for t in range(MOBA_TOPK):
            gmax = jnp.max(gate, axis=0, keepdims=True)
            idx = jnp.min(jnp.where(gate == gmax, blk, nblk), axis=0, keepdims=True)
            pick.append(jnp.where(t < i, idx, -1))
            gate = jnp.where(blk == idx, NEG_INF, gate)

        s = jnp.where(causal, own_scores[hh] * (scale * LOG2E) + bias_ref[hh, 0], NEG_INF)
        m0 = jnp.max(s, axis=0, keepdims=True)
        a0 = jnp.dot(vt_own, jnp.exp2(s - m0).astype(BF16), preferred_element_type=F32)
        picks.append(pick)
        init.append((m0, a0))

    def past(jj, carry):
        ja = 2 * jj
        start = pl.multiple_of(ja * mb, mb)
        kj, vtj = k_ref[pl.ds(start, 2 * mb), :], vt_ref[:, pl.ds(start, 2 * mb)]
        tile_a = jnp.minimum(i - ja, N_BIAS_TILES - 1)
        tile_b = jnp.minimum(i - ja - 1, N_BIAS_TILES - 1)
        scores = [lax.dot_general(kj, qs[hh], nt, preferred_element_type=F32) for hh in range(MOBA_GROUP)]
        out = []
        for hh in range(MOBA_GROUP):
            m, acc = carry[hh]
            p0, p1, p2 = picks[hh]
            sa = scores[hh][:mb] * (scale * LOG2E) + bias_ref[hh, tile_a]
            sb = scores[hh][mb:] * (scale * LOG2E) + bias_ref[hh, tile_b]
            sa = jnp.where((p0 == ja) | (p1 == ja) | (p2 == ja), sa, NEG_INF)
            sb = jnp.where((p0 == ja + 1) | (p1 == ja + 1) | (p2 == ja + 1), sb, NEG_INF)
            m_new = jnp.maximum(m, jnp.maximum(jnp.max(sa, axis=0, keepdims=True),
                                               jnp.max(sb, axis=0, keepdims=True)))
            e = jnp.concatenate([jnp.exp2(sa - m_new), jnp.exp2(sb - m_new)], axis=0).astype(BF16)
            acc = jnp.exp2(m - m_new) * acc + jnp.dot(vtj, e, preferred_element_type=F32)
            out.append((m_new, acc))
        return tuple(out)

    final = lax.fori_loop(0, (i + 1) // 2, past, tuple(init))
    for hh in range(MOBA_GROUP):
        _, acc = final[hh]
        o_ref[:, hh * dh:(hh + 1) * dh] = (acc[:dh] / acc[dh:dh + 1]).T.astype(o_ref.dtype)


def _moba_attention(proj, kmean, bias_tiles, w_gu, bsz, seq, d):
    gw = MOBA_GROUP * HEAD_DIM_MOBA
    n_kv = (d // 8) // HEAD_DIM_MOBA
    nblk = seq // MOBA_BLOCK
    q_off, k_off, v_off = 5 * d // 8, 9 * d // 8, 10 * d // 8
    assert q_off % gw == 0 and MOBA_TOPK == 3 and nblk >= 2
    qc, kc, vc = q_off // gw, k_off // HEAD_DIM_MOBA, v_off // HEAD_DIM_MOBA
    tile_shape = (MOBA_GROUP, N_BIAS_TILES, MOBA_BLOCK, MOBA_BLOCK)
    n_exp, wd, f2 = w_gu.shape
    steps = n_kv * bsz * nblk
    tk = n_exp * wd // steps
    assert n_exp * wd % steps == 0 and wd % tk == 0 and tk % BF16_SUBLANES == 0
    per_exp = wd // tk
    wmap = lambda g, b, i: (((g * bsz + b) * nblk + i) // per_exp, ((g * bsz + b) * nblk + i) % per_exp, 0)
    w_out = jax.ShapeDtypeStruct((n_exp, wd, f2 // 2), BF16)
    return pl.pallas_call(
        functools.partial(_moba_kernel, nblk=nblk),
        grid=(n_kv, bsz, nblk),
        in_specs=[pl.BlockSpec((MOBA_BLOCK, gw), lambda g, b, i: (b * nblk + i, qc + g)),
                  pl.BlockSpec((seq, HEAD_DIM_MOBA), lambda g, b, i: (b, kc + g)),
                  pl.BlockSpec((seq, HEAD_DIM_MOBA), lambda g, b, i: (b, vc + g)),
                  pl.BlockSpec((nblk, HEAD_DIM_MOBA), lambda g, b, i: (b, g)),
                  pl.BlockSpec(tile_shape, lambda g, b, i: (g, 0, 0, 0)),
                  pl.BlockSpec((1, tk, f2), wmap),
                  pl.BlockSpec((2 * LANES, 2 * LANES), lambda g, b, i: (0, 0))],
        out_specs=[pl.BlockSpec((MOBA_BLOCK, gw), lambda g, b, i: (b * nblk + i, g)),
                   pl.BlockSpec((1, tk, f2 // 2), wmap), pl.BlockSpec((1, tk, f2 // 2), wmap)],
        out_shape=[jax.ShapeDtypeStruct((bsz * seq, d // 2), BF16), w_out, w_out],
        scratch_shapes=[pltpu.VMEM((HEAD_DIM_MOBA + BF16_SUBLANES, seq), BF16)],
        compiler_params=_params(("parallel", "parallel", "arbitrary"),
                                [(tile_shape, F32), ((seq, HEAD_DIM_MOBA), BF16), ((seq, HEAD_DIM_MOBA), BF16),
                                 ((MOBA_BLOCK, gw), BF16), ((MOBA_BLOCK, gw), BF16),
                                 ((tk, f2), F32), ((tk, f2), BF16)],
                                [((2 * MOBA_BLOCK, MOBA_BLOCK), F32)] * 8
                                + [((HEAD_DIM_MOBA + BF16_SUBLANES, seq), BF16)] + [((tk, 2 * LANES), F32)] * 4),
        name="moba_attn",
    )(proj, proj, proj, kmean, bias_tiles, w_gu, _deinterleave_perm())


def _merge_kernel(oa_ref, ob_ref, wa_ref, wb_ref, ga_ref, gb_ref, o_ref):
    ya = jnp.dot(oa_ref[...], wa_ref[...], preferred_element_type=F32)
    yb = jnp.dot(ob_ref[...], wb_ref[...], preferred_element_type=F32)
    merged = jax.nn.sigmoid(ga_ref[...].astype(F32)) * ya + jax.nn.sigmoid(gb_ref[...].astype(F32)) * yb
    o_ref[...] = merged.astype(o_ref.dtype)


def _merge(o_a, o_b, w_a, w_b, proj, d):
    t, kdim = o_a.shape
    tm, tn = _tile(t, 1024), _tile(d, 512)
    ga_col, gb_col = (11 * d // 8) // tn, (19 * d // 8) // tn
    assert (11 * d // 8) % tn == 0 and (19 * d // 8) % tn == 0
    return pl.pallas_call(
        _merge_kernel,
        grid=(t // tm, d // tn),
        in_specs=[pl.BlockSpec((tm, kdim), lambda i, j: (i, 0)),
                  pl.BlockSpec((tm, kdim), lambda i, j: (i, 0)),
                  pl.BlockSpec((kdim, tn), lambda i, j: (0, j)),
                  pl.BlockSpec((kdim, tn), lambda i, j: (0, j)),
                  pl.BlockSpec((tm, tn), lambda i, j: (i, ga_col + j)),
                  pl.BlockSpec((tm, tn), lambda i, j: (i, gb_col + j))],
        out_specs=pl.BlockSpec((tm, tn), lambda i, j: (i, j)),
        out_shape=jax.ShapeDtypeStruct((t, d), BF16),
        compiler_params=_params(("parallel", "arbitrary"),
                                [((tm, kdim), BF16)] * 2 + [((kdim, tn), BF16)] * 2 + [((tm, tn), BF16)] * 3,
                                [((tm, tn), F32)] * 3),
        name="merge",
    )(o_a, o_b, w_a, w_b, proj, proj)


def _post_mix_kernel(y_ref, x_ref, gpost_ref, gpre_ref, mod_ref, wrh_ref, wrl_ref, br_ref,
                     x1_ref, h2_ref, idx_ref, wts_ref):
    m = mod_ref[0]
    x1 = x_ref[...] + m[2:3, :] * _rms(y_ref[...], gpost_ref[...])
    x1_ref[...] = x1
    h2 = _rms(x1, gpre_ref[...]) * (1.0 + m[4:5, :]) + m[3:4, :]
    half = h2.shape[1] // 2
    h2_ref[...] = _pack_bf16_pair(h2[:, :half], h2[:, half:])
    logits = _dot_split(_split_bf16(h2), (wrh_ref[...], wrl_ref[...])) + br_ref[...]
    lane = lax.broadcasted_iota(I32, logits.shape, 1)
    logits = jnp.where(lane < N_EXPERTS, logits, NEG_INF)
    idx_out = jnp.zeros(logits.shape, I32)
    val_out = jnp.zeros(logits.shape, F32)
    top = None
    den = jnp.zeros((logits.shape[0], 1), F32)
    for k in range(TOP_K):
        vmax = jnp.max(logits, axis=-1, keepdims=True)
        idx = jnp.min(jnp.where(logits == vmax, lane, LANES), axis=-1, keepdims=True)
        top = vmax if top is None else top
        e = jnp.exp(vmax - top)
        den = den + e
        idx_out = jnp.where(lane == k, idx, idx_out)
        val_out = jnp.where(lane == k, e, val_out)
        logits = jnp.where(lane == idx, NEG_INF, logits)
    idx_ref[...] = idx_out
    wts_ref[...] = val_out / den


def _post_mix(y, x2, g_post, g_pre, mod, w_router, b_router, seq):
    t, d = x2.shape
    tm = _tile(seq, 256)
    wr_hi, wr_lo = _split_bf16(jnp.pad(w_router, ((0, 0), (0, LANES - N_EXPERTS))))
    br = jnp.pad(b_router, (0, LANES - N_EXPERTS)).reshape(1, LANES)
    row = lambda i: (i, 0)
    const = lambda i: (0, 0)
    return pl.pallas_call(
        _post_mix_kernel,
        grid=(t // tm,),
        in_specs=[pl.BlockSpec((tm, d), row), pl.BlockSpec((tm, d), row),
                  pl.BlockSpec((1, d), const), pl.BlockSpec((1, d), const),
                  pl.BlockSpec((1, 6, d), lambda i: (i * tm // seq, 0, 0)),
                  pl.BlockSpec((d, LANES), const), pl.BlockSpec((d, LANES), const),
                  pl.BlockSpec((1, LANES), const)],
        out_specs=[pl.BlockSpec((tm, d), row), pl.BlockSpec((tm, d // 2), row),
                   pl.BlockSpec((tm, LANES), row), pl.BlockSpec((tm, LANES), row)],
        out_shape=[jax.ShapeDtypeStruct((t, d), F32), jax.ShapeDtypeStruct((t, d // 2), jnp.uint32),
                   jax.ShapeDtypeStruct((t, LANES), I32), jax.ShapeDtypeStruct((t, LANES), F32)],
        compiler_params=_params(("parallel",), [((tm, d), F32)] * 4 + [((d, LANES), F32)],
                                [((tm, d), F32)] * 3),
        name="post_mix_router",
    )(y, x2, g_post.reshape(1, d), g_pre.reshape(1, d), mod, wr_hi, wr_lo, br)


def _routing(top_idx, n_blocks):
    flat_e = top_idx.reshape(-1)
    onehot = (flat_e[:, None] == jnp.arange(N_EXPERTS, dtype=I32)[None, :]).astype(I32)
    csum = jnp.cumsum(onehot, axis=0)
    counts = csum[-1]
    rank = jnp.sum(csum * onehot, axis=1) - 1
    padded = (counts + MOE_BLOCK - 1) // MOE_BLOCK * MOE_BLOCK
    pad_end = jnp.cumsum(padded)
    pad_start = pad_end - padded
    dest = jnp.sum(onehot * pad_start[None, :], axis=1) + rank
    n_real = (pad_end[-1] // MOE_BLOCK).astype(I32)
    blk_start = jnp.arange(n_blocks, dtype=I32) * MOE_BLOCK
    block_e = jnp.minimum(jnp.sum((pad_end[None, :] <= blk_start[:, None]).astype(I32), axis=1), N_EXPERTS - 1)
    last_blk = jnp.where(counts > 0, pad_end // MOE_BLOCK - 1, -1).astype(I32)
    return dest.astype(I32), block_e.astype(I32), n_real.reshape(1), last_blk


def _pack_bf16_pair(lo, hi):
    lo_bits = lax.bitcast_convert_type(lo.astype(BF16).astype(F32), jnp.uint32) >> 16
    hi_bits = lax.bitcast_convert_type(hi.astype(BF16).astype(F32), jnp.uint32) & jnp.uint32(0xFFFF0000)
    return hi_bits | lo_bits


def _unpack_bf16_pair(p):
    lo = lax.bitcast_convert_type(p << 16, F32).astype(BF16)
    hi = lax.bitcast_convert_type(p & jnp.uint32(0xFFFF0000), F32).astype(BF16)
    return lo, hi


def _dispatch_kernel(last_ref, nr_ref, dest_ref, h_ref, xg_hbm, zeros, zsem, sem, *, tokens, n_blocks):
    def zero_block(m):
        rows = pl.ds(pl.multiple_of(m * MOE_BLOCK, MOE_BLOCK), MOE_BLOCK)
        return pltpu.make_async_copy(zeros, xg_hbm.at[rows], zsem)

    @pl.when(pl.program_id(0) == 0)
    def _():
        zeros[...] = jnp.zeros(zeros.shape, zeros.dtype)
        for phase in ("start", "wait"):
            for e in range(N_EXPERTS):
                @pl.when(last_ref[e] >= 0)
                def _(e=e, phase=phase):
                    getattr(zero_block(last_ref[e]), phase)()

            def tail(m, carry, phase=phase):
                getattr(zero_block(m), phase)()
                return carry

            lax.fori_loop(nr_ref[0], n_blocks, tail, 0)

    def row_copy(t, row):
        return pltpu.make_async_copy(h_ref.at[pl.ds(t, 1)], xg_hbm.at[pl.ds(row, 1)], sem)

    def issue(t, carry):
        for k in range(TOP_K):
            row_copy(t, dest_ref[0, 0, t * TOP_K + k]).start()
        return carry

    lax.fori_loop(0, tokens, issue, 0)

    def drain(t, carry):
        for k in range(TOP_K):
            row_copy(0, 0).wait()
        return carry

    lax.fori_loop(0, tokens, drain, 0)


def _dispatch(h2p, dest, last_blk, n_real, n_blocks):
    t, dp = h2p.shape
    tokens = _tile(t, 512)
    steps = t // tokens
    return pl.pallas_call(
        functools.partial(_dispatch_kernel, tokens=tokens, n_blocks=n_blocks),
        grid_spec=pltpu.PrefetchScalarGridSpec(
            num_scalar_prefetch=2, grid=(steps,),
            in_specs=[pl.BlockSpec((1, 1, tokens * TOP_K), lambda i, last, nr: (i, 0, 0), memory_space=pltpu.SMEM),
                      pl.BlockSpec((tokens, dp), lambda i, last, nr: (i, 0))],
            out_specs=pl.BlockSpec(memory_space=pl.ANY),
            scratch_shapes=[pltpu.VMEM((MOE_BLOCK, dp), h2p.dtype),
                            pltpu.SemaphoreType.DMA(()), pltpu.SemaphoreType.DMA(())]),
        out_shape=jax.ShapeDtypeStruct((n_blocks * MOE_BLOCK, dp), h2p.dtype),
        compiler_params=_params(("arbitrary",), [((tokens, dp), h2p.dtype)], [((MOE_BLOCK, dp), h2p.dtype)]),
        name="moe_dispatch",
    )(last_blk, n_real, dest.reshape(steps, 1, tokens * TOP_K), h2p)


def _gate_up_kernel(be_ref, nr_ref, x_ref, wg_ref, wl_ref, bg_ref, bl_ref, wd_ref, o_ref, wdb_ref, *, n_cast):
    real = pl.program_id(1) < nr_ref[0]

    @pl.when(jnp.logical_and(pl.program_id(0) == 0, pl.program_id(1) < n_cast))
    def _():
        wdb_ref[...] = wd_ref[...].astype(BF16)

    @pl.when(real)
    def _():
        lo, hi = _unpack_bf16_pair(x_ref[...])
        half = lo.shape[1]

        def proj(w_ref, b_ref):
            return (jnp.dot(lo, w_ref[0, :half, :], preferred_element_type=F32)
                    + jnp.dot(hi, w_ref[0, half:, :], preferred_element_type=F32) + b_ref[0])

        glu = jnp.minimum(proj(wg_ref, bg_ref), SWIGLU_LIMIT)
        lin = jnp.clip(proj(wl_ref, bl_ref), -SWIGLU_LIMIT, SWIGLU_LIMIT)
        o_ref[...] = (glu * jax.nn.sigmoid(SWIGLU_ALPHA * glu) * (lin + 1.0)).astype(o_ref.dtype)

    @pl.when(jnp.logical_not(real))
    def _():
        o_ref[...] = jnp.zeros(o_ref.shape, o_ref.dtype)


def _gate_up(xg, w_g, w_l, b_g, b_l, w_d, block_e, n_real, n_blocks):
    rows, dp = xg.shape
    d, f = w_g.shape[1], w_g.shape[2]
    passes = 2
    tf = f // passes
    blk = lambda p, m, be, nr: jnp.minimum(m, nr[0] - 1)
    wmap = lambda p, m, be, nr: (be[blk(p, m, be, nr)], 0, p)
    n_exp, wf, wn = w_d.shape
    tr = min(r for r in range(BF16_SUBLANES, wf + 1, BF16_SUBLANES)
             if wf % r == 0 and n_exp * wf // r <= n_blocks)
    per_exp = wf // tr
    n_cast = n_exp * per_exp

    def cmap(p, m, be, nr):
        s = jnp.where(p == 0, jnp.minimum(m, n_cast - 1), n_cast - 1)
        return (s // per_exp, s % per_exp, 0)

    return pl.pallas_call(
        functools.partial(_gate_up_kernel, n_cast=n_cast),
        grid_spec=pltpu.PrefetchScalarGridSpec(
            num_scalar_prefetch=2, grid=(passes, n_blocks),
            in_specs=[pl.BlockSpec((MOE_BLOCK, dp), lambda p, m, be, nr: (blk(p, m, be, nr), 0)),
                      pl.BlockSpec((1, d, tf), wmap), pl.BlockSpec((1, d, tf), wmap),
                      pl.BlockSpec((1, 1, tf), wmap), pl.BlockSpec((1, 1, tf), wmap),
                      pl.BlockSpec((1, tr, wn), cmap)],
            out_specs=[pl.BlockSpec((MOE_BLOCK, tf), lambda p, m, be, nr: (m, p)),
                       pl.BlockSpec((1, tr, wn), cmap)]),
        out_shape=[jax.ShapeDtypeStruct((rows, f), BF16), jax.ShapeDtypeStruct(w_d.shape, BF16)],
        compiler_params=_params(("arbitrary", "arbitrary"),
                                [((MOE_BLOCK, dp), xg.dtype), ((d, tf), BF16), ((d, tf), BF16),
                                 ((MOE_BLOCK, tf), BF16), ((tr, wn), F32), ((tr, wn), BF16)],
                                [((MOE_BLOCK, d), BF16)] + [((MOE_BLOCK, tf), F32)] * 4),
        name="moe_gate_up",
    )(block_e, n_real, xg, w_g, w_l, b_g, b_l, w_d)


def _down_kernel(be_ref, nr_ref, a_ref, w_ref, b_ref, o_ref):
    real = pl.program_id(0) < nr_ref[0]

    @pl.when(real)
    def _():
        y = jnp.dot(a_ref[...], w_ref[0], preferred_element_type=F32) + b_ref[0]
        half = y.shape[1] // 2
        o_ref[...] = _pack_bf16_pair(y[:, :half], y[:, half:])

    @pl.when(jnp.logical_not(real))
    def _():
        o_ref[...] = jnp.zeros(o_ref.shape, o_ref.dtype)


def _down(act, w_d, b_d, block_e, n_real, n_blocks):
    rows, f = act.shape
    d = w_d.shape[2]
    blk = lambda m, be, nr: jnp.minimum(m, nr[0] - 1)
    wmap = lambda m, be, nr: (be[blk(m, be, nr)], 0, 0)
    return pl.pallas_call(
        _down_kernel,
        grid_spec=pltpu.PrefetchScalarGridSpec(
            num_scalar_prefetch=2, grid=(n_blocks,),
            in_specs=[pl.BlockSpec((MOE_BLOCK, f), lambda m, be, nr: (m, 0)),
                      pl.BlockSpec((1, f, d), wmap), pl.BlockSpec((1, 1, d), wmap)],
            out_specs=pl.BlockSpec((MOE_BLOCK, d // 2), lambda m, be, nr: (m, 0))),
        out_shape=jax.ShapeDtypeStruct((rows, d // 2), jnp.uint32),
        compiler_params=_params(("arbitrary",),
                                [((MOE_BLOCK, f), BF16), ((f, d), BF16), ((MOE_BLOCK, d // 2), jnp.uint32)],
                                [((MOE_BLOCK, d), F32)] * 2),
        name="moe_down",
    )(block_e, n_real, act, w_d, b_d)


def _combine_kernel(dcur_ref, dnext_ref, w_ref, x1_ref, g_ref, mod_ref, y_hbm, o_ref, buf, sem,
                    *, tokens, steps):
    step = pl.program_id(0)
    slot = step % 2

    def row_copy(slot_, k, t, row):
        return pltpu.make_async_copy(y_hbm.at[pl.ds(row, 1)], buf.at[slot_, k, pl.ds(t, 1)], sem.at[slot_])

    def start_tile(dest_ref, slot_):
        def body(t, carry):
            for k in range(TOP_K):
                row_copy(slot_, k, t, dest_ref[0, 0, t * TOP_K + k]).start()
            return carry
        lax.fori_loop(0, tokens, body, 0)

    @pl.when(step == 0)
    def _():
        start_tile(dcur_ref, 0)

    @pl.when(step + 1 < steps)
    def _():
        start_tile(dnext_ref, 1 - slot)

    def drain(t, carry):
        for k in range(TOP_K):
            row_copy(slot, k, 0, 0).wait()
        return carry

    lax.fori_loop(0, tokens, drain, 0)

    w = w_ref[...]
    acc_lo = acc_hi = None
    for k in range(TOP_K):
        word = buf[slot, k]
        lo = lax.bitcast_convert_type(word << 16, F32) * w[:, k:k + 1]
        hi = lax.bitcast_convert_type(word & jnp.uint32(0xFFFF0000), F32) * w[:, k:k + 1]
        acc_lo = lo if acc_lo is None else acc_lo + lo
        acc_hi = hi if acc_hi is None else acc_hi + hi
    acc = jnp.concatenate([acc_lo, acc_hi], axis=1)
    m = mod_ref[0]
    o_ref[...] = x1_ref[...] + m[5:6, :] * _rms(acc, g_ref[...])


def _combine(y_sorted, dest, top_w, x1, g_post, mod, seq):
    t, d = x1.shape
    dp = y_sorted.shape[1]
    tokens = _tile(seq, 256)
    steps = t // tokens
    row = lambda i: (i, 0)
    dest3 = dest.reshape(steps, 1, tokens * TOP_K)
    return pl.pallas_call(
        functools.partial(_combine_kernel, tokens=tokens, steps=steps),
        grid=(steps,),
        in_specs=[pl.BlockSpec((1, 1, tokens * TOP_K), lambda i: (i, 0, 0), memory_space=pltpu.SMEM),
                  pl.BlockSpec((1, 1, tokens * TOP_K), lambda i: (jnp.minimum(i + 1, steps - 1), 0, 0),
                               memory_space=pltpu.SMEM),
                  pl.BlockSpec((tokens, LANES), row),
                  pl.BlockSpec((tokens, d), row),
                  pl.BlockSpec((1, d), lambda i: (0, 0)),
                  pl.BlockSpec((1, 6, d), lambda i: (i * tokens // seq, 0, 0)),
                  pl.BlockSpec(memory_space=pl.ANY)],
        out_specs=pl.BlockSpec((tokens, d), row),
        out_shape=jax.ShapeDtypeStruct((t, d), F32),
        scratch_shapes=[pltpu.VMEM((2, TOP_K, tokens, dp), y_sorted.dtype), pltpu.SemaphoreType.DMA((2,))],
        compiler_params=_params(("arbitrary",), [((tokens, d), F32)] * 2,
                                [((2, TOP_K, tokens, dp), y_sorted.dtype)] + [((tokens, d), F32)] * 3),
        name="moe_combine",
    )(dest3, dest3, top_w, x1, g_post.reshape(1, d), mod, y_sorted)


def _moe(h2, top_idx, top_w, x1, g_post, mod, w_g, w_l, b_gu, w_dn, b_dn, seq):
    t, d = x1.shape
    f = w_dn.shape[1]
    n_blocks = -(-(t * TOP_K) // MOE_BLOCK) + N_EXPERTS
    dest, block_e, n_real, last_blk = _routing(top_idx, n_blocks)
    xg = _dispatch(h2, dest, last_blk, n_real, n_blocks)
    b_g = b_gu[:, 0::2].reshape(N_EXPERTS, 1, f)
    b_l = b_gu[:, 1::2].reshape(N_EXPERTS, 1, f)
    act, w_dn_bf16 = _gate_up(xg, w_g, w_l, b_g, b_l, w_dn, block_e, n_real, n_blocks)
    y_sorted = _down(act, w_dn_bf16, b_dn.reshape(N_EXPERTS, 1, d), block_e, n_real, n_blocks)
    return _combine(y_sorted, dest, top_w, x1, g_post, mod, seq)


def kernel(x, c, w_ada, b_ada, g_pre_mix, g_post_mix, g_pre_ffn, g_post_ffn, w_in, b_in, sinks, rel_bias,
           w_o_swa, w_o_moba, w_out, w_router, b_router, w_gate_up, b_gate_up, w_down, b_down):
    bsz, seq, d = x.shape
    depth = w_ada.shape[0]
    assert seq % MOBA_BLOCK == 0 and d % 1024 == 0
    x2 = x.reshape(bsz * seq, d)
    cos_t, sin_t = _rope_tables(seq)
    bias_tiles = _bias_tiles(rel_bias)
    for l in range(depth):
        mod = _ada_mod(c, w_ada[l], b_ada[l])
        h = _prenorm(x2, g_pre_mix[l], mod, seq, shift_row=0, scale_row=1)
        proj = _matmul(h, w_in[l].astype(BF16), b_in[l], BF16, 1024, 768, "in_proj")
        o_a = _swa_attention(proj, sinks[l], cos_t, sin_t, bsz, seq, d)
        o_b, w_g, w_l = _moba_attention(proj, _kmean(proj, bsz, seq, d), bias_tiles, w_gate_up[l], bsz, seq, d)
        merged = _merge(o_a, o_b, w_o_swa[l].astype(BF16), w_o_moba[l].astype(BF16), proj, d)
        y = _matmul(merged, w_out[l].astype(BF16), jnp.zeros((d,), F32), F32, 1024, 1024, "out_proj")
        x1, h2, idx, wts = _post_mix(y, x2, g_post_mix[l], g_pre_ffn[l], mod, w_router[l], b_router[l], seq)
        x2 = _moe(h2, idx[:, :TOP_K], wts, x1, g_post_ffn[l], mod,
                  w_g, w_l, b_gate_up[l], w_down[l], b_down[l], seq)
    return x2.reshape(bsz, seq, d)
```

```python
import functools
import math

import numpy as np
import jax
import jax.numpy as jnp
from jax import lax
from jax.experimental import pallas as pl
from jax.experimental.pallas import tpu as pltpu

F32 = jnp.float32
BF16 = jnp.bfloat16
I32 = jnp.int32

HEAD_DIM_SWA = 64
SWA_GROUP = 8
WINDOW = 128
ROPE_THETA = 150000.0
HEAD_DIM_MOBA = 128
MOBA_GROUP = 4
MOBA_BLOCK = 256
MOBA_TOPK = 3
REL_BUCKETS = 32
REL_MAX_DIST = 1024
N_EXPERTS = 32
TOP_K = 4
SWIGLU_LIMIT = 7.0
SWIGLU_ALPHA = 1.702
MOE_BLOCK = 256
RMS_EPS = 1e-6

LANES = 128
SUBLANES = 8
BF16_SUBLANES = 16
VMEM_BYTES_V7X = 64 * 1024 * 1024
VMEM_CAP = VMEM_BYTES_V7X - 8 * 1024 * 1024

NEG_INF = float("-inf")
LOG2E = 1.0 / math.log(2.0)
N_BIAS_TILES = (REL_MAX_DIST + MOBA_BLOCK - 1) // MOBA_BLOCK + 2


def _nbytes(shape, dtype):
    return int(np.prod(shape)) * jnp.dtype(dtype).itemsize


def _params(semantics, blocks, temps=()):
    need = 2 * sum(_nbytes(s, d) for s, d in blocks) + sum(_nbytes(s, d) for s, d in temps)
    need = max(need + need // 4, 16 * 1024 * 1024)
    return pltpu.CompilerParams(dimension_semantics=semantics, vmem_limit_bytes=min(need, VMEM_CAP))


def _tile(n, pref):
    t = min(n, pref)
    assert n % t == 0, (n, pref)
    return t


def _split_bf16(x):
    hi = x.astype(BF16)
    return hi, (x - hi.astype(F32)).astype(BF16)


def _dot_split(a, w):
    (a_hi, a_lo), (w_hi, w_lo) = a, w
    return (jnp.dot(a_hi, w_hi, preferred_element_type=F32)
            + (jnp.dot(a_hi, w_lo, preferred_element_type=F32) + jnp.dot(a_lo, w_hi, preferred_element_type=F32)))


def _ada_kernel(c_ref, w_ref, b_ref, o_ref):
    c = c_ref[...]
    s = c * jax.nn.sigmoid(c)
    o_ref[...] = _dot_split(_split_bf16(s), _split_bf16(w_ref[...])) + b_ref[...]


def _ada_mod(c, w, b):
    bsz, d = c.shape
    n = w.shape[1]
    rows = -(-bsz // SUBLANES) * SUBLANES
    cp = jnp.pad(c, ((0, rows - bsz), (0, 0)))
    tn = _tile(n, 512)
    out = pl.pallas_call(
        _ada_kernel,
        grid=(n // tn,),
        in_specs=[pl.BlockSpec((rows, d), lambda j: (0, 0)),
                  pl.BlockSpec((d, tn), lambda j: (0, j)),
                  pl.BlockSpec((1, tn), lambda j: (0, j))],
        out_specs=pl.BlockSpec((rows, tn), lambda j: (0, j)),
        out_shape=jax.ShapeDtypeStruct((rows, n), F32),
        compiler_params=_params(("arbitrary",), [((d, tn), F32), ((rows, d), F32)], [((d, tn), F32)]),
        name="ada_mod",
    )(cp, w, b.reshape(1, n))
    return out[:bsz].reshape(bsz, 6, d)


def _rms(x, g):
    return x * lax.rsqrt(jnp.mean(x * x, axis=-1, keepdims=True) + RMS_EPS) * g


def _prenorm_kernel(x_ref, g_ref, mod_ref, o_ref, *, shift_row, scale_row):
    m = mod_ref[0]
    y = _rms(x_ref[...], g_ref[...])
    o_ref[...] = (y * (1.0 + m[scale_row:scale_row + 1, :]) + m[shift_row:shift_row + 1, :]).astype(o_ref.dtype)


def _prenorm(x2, g, mod, seq, shift_row, scale_row):
    t, d = x2.shape
    tm = _tile(seq, 256)
    return pl.pallas_call(
        functools.partial(_prenorm_kernel, shift_row=shift_row, scale_row=scale_row),
        grid=(t // tm,),
        in_specs=[pl.BlockSpec((tm, d), lambda i: (i, 0)),
                  pl.BlockSpec((1, d), lambda i: (0, 0)),
                  pl.BlockSpec((1, 6, d), lambda i: (i * tm // seq, 0, 0))],
        out_specs=pl.BlockSpec((tm, d), lambda i: (i, 0)),
        out_shape=jax.ShapeDtypeStruct((t, d), BF16),
        compiler_params=_params(("parallel",), [((tm, d), F32), ((tm, d), BF16)], [((tm, d), F32)] * 2),
        name="prenorm",
    )(x2, g.reshape(1, d), mod)


def _mm_kernel(a_ref, w_ref, b_ref, o_ref):
    acc = jnp.dot(a_ref[...], w_ref[...], preferred_element_type=F32)
    o_ref[...] = (acc + b_ref[...]).astype(o_ref.dtype)


def _matmul(a, w, bias, out_dtype, tm, tn, name):
    m, k = a.shape
    n = w.shape[1]
    tm, tn = _tile(m, tm), _tile(n, tn)
    return pl.pallas_call(
        _mm_kernel,
        grid=(m // tm, n // tn),
        in_specs=[pl.BlockSpec((tm, k), lambda i, j: (i, 0)),
                  pl.BlockSpec((k, tn), lambda i, j: (0, j)),
                  pl.BlockSpec((1, tn), lambda i, j: (0, j))],
        out_specs=pl.BlockSpec((tm, tn), lambda i, j: (i, j)),
        out_shape=jax.ShapeDtypeStruct((m, n), out_dtype),
        compiler_params=_params(("parallel", "arbitrary"),
                                [((tm, k), a.dtype), ((k, tn), w.dtype), ((tm, tn), out_dtype)],
                                [((tm, tn), F32)]),
        name=name,
    )(a, w, bias.reshape(1, n).astype(F32))


def _rope_tables(seq):
    half = HEAD_DIM_SWA // 2
    inv = ROPE_THETA ** (-jnp.arange(half, dtype=F32) / half)
    ang = jnp.arange(seq, dtype=F32)[:, None] * inv[None, :]
    cos, sin = jnp.cos(ang), jnp.sin(ang)
    cos_t = jnp.concatenate([cos, cos, cos, cos], axis=-1)
    sin_t = jnp.concatenate([-sin, sin, -sin, sin], axis=-1)
    return cos_t, sin_t


def _swa_kernel(sinks_ref, q_ref, kp_ref, kc_ref, vp_ref, vc_ref, cq_ref, sq_ref, cp_ref, sp_ref, o_ref,
                *, n_kv):
    n = pl.program_id(1)
    w = WINDOW
    half = HEAD_DIM_SWA // 2
    lane = lax.broadcasted_iota(I32, (1, LANES), 1)
    first_half = (lane % HEAD_DIM_SWA) < half
    low_head = lane < HEAD_DIM_SWA

    def rope(xf, cos, sin):
        partner = jnp.where(first_half, pltpu.roll(xf, LANES - half, 1), pltpu.roll(xf, half, 1))
        return xf * cos + partner * sin

    cos_q, sin_q = cq_ref[...], sq_ref[...]
    cos_k = jnp.concatenate([cp_ref[...], cos_q], axis=0)
    sin_k = jnp.concatenate([sp_ref[...], sin_q], axis=0)
    ri = lax.broadcasted_iota(I32, (w, 2 * w), 0)
    cj = lax.broadcasted_iota(I32, (w, 2 * w), 1)
    diff = w + ri - cj
    mask = (diff >= 0) & (diff < w) & ((cj >= w) | (n > 0))
    pairs = SWA_GROUP // 2

    for g in range(n_kv):
        slab = g // 2
        cols = slice(slab * LANES, (slab + 1) * LANES)
        k2 = rope(jnp.concatenate([kp_ref[:, cols], kc_ref[:, cols]], axis=0).astype(F32), cos_k, sin_k)
        v2 = jnp.concatenate([vp_ref[:, cols], vc_ref[:, cols]], axis=0).astype(F32)
        k2s, v2s = pltpu.roll(k2, HEAD_DIM_SWA, 1), pltpu.roll(v2, HEAD_DIM_SWA, 1)
        if g % 2 == 0:
            k_lo, v_lo = jnp.where(low_head, k2, 0.0), jnp.where(low_head, v2, 0.0)
            k_hi, v_hi = jnp.where(low_head, 0.0, k2s), jnp.where(low_head, 0.0, v2s)
        else:
            k_lo, v_lo = jnp.where(low_head, k2s, 0.0), jnp.where(low_head, v2s, 0.0)
            k_hi, v_hi = jnp.where(low_head, 0.0, k2), jnp.where(low_head, 0.0, v2)
        kv = ((k_lo.astype(BF16), v_lo.astype(BF16)), (k_hi.astype(BF16), v_hi.astype(BF16)))
        scores = []
        for p in range(pairs):
            qcols = slice((g * pairs + p) * LANES, (g * pairs + p + 1) * LANES)
            q = rope(q_ref[:, qcols].astype(F32), cos_q, sin_q) * (1.0 / math.sqrt(HEAD_DIM_SWA))
            q = q.astype(BF16)
            scores.append([lax.dot_general(q, kk, (((1,), (1,)), ((), ())), preferred_element_type=F32)
                           for kk, _ in kv])
        for p in range(pairs):
            qs = g * pairs + p
            qcols = slice(qs * LANES, (qs + 1) * LANES)
            o = jnp.zeros((w, LANES), F32)
            for hh, (kk, vv) in enumerate(kv):
                s = jnp.where(mask, scores[p][hh], NEG_INF)
                sink = sinks_ref[2 * qs + hh]
                m = jnp.maximum(jnp.max(s, axis=-1, keepdims=True), sink)
                e = jnp.exp(s - m)
                den = jnp.sum(e, axis=-1, keepdims=True) + jnp.exp(sink - m)
                o = o + jnp.dot(e.astype(BF16), vv, preferred_element_type=F32) / den
            o_ref[:, qcols] = o.astype(o_ref.dtype)


def _swa_attention(proj, sinks, cos_t, sin_t, bsz, seq, d):
    q_w, kv_w = d // 2, d // 16
    n_kv = kv_w // HEAD_DIM_SWA
    assert n_kv % 2 == 0 and q_w % kv_w == 0
    nb = seq // WINDOW
    k_col, v_col = q_w // kv_w, q_w // kv_w + 1
    cur = lambda b, n: b * nb + n
    prev = lambda b, n: b * nb + jnp.maximum(n - 1, 0)
    return pl.pallas_call(
        functools.partial(_swa_kernel, n_kv=n_kv),
        grid=(bsz, nb),
        in_specs=[pl.BlockSpec(memory_space=pltpu.SMEM),
                  pl.BlockSpec((WINDOW, q_w), lambda b, n: (cur(b, n), 0)),
                  pl.BlockSpec((WINDOW, kv_w), lambda b, n: (prev(b, n), k_col)),
                  pl.BlockSpec((WINDOW, kv_w), lambda b, n: (cur(b, n), k_col)),
                  pl.BlockSpec((WINDOW, kv_w), lambda b, n: (prev(b, n), v_col)),
                  pl.BlockSpec((WINDOW, kv_w), lambda b, n: (cur(b, n), v_col)),
                  pl.BlockSpec((WINDOW, LANES), lambda b, n: (n, 0)),
                  pl.BlockSpec((WINDOW, LANES), lambda b, n: (n, 0)),
                  pl.BlockSpec((WINDOW, LANES), lambda b, n: (jnp.maximum(n - 1, 0), 0)),
                  pl.BlockSpec((WINDOW, LANES), lambda b, n: (jnp.maximum(n - 1, 0), 0))],
        out_specs=pl.BlockSpec((WINDOW, q_w), lambda b, n: (cur(b, n), 0)),
        out_shape=jax.ShapeDtypeStruct((bsz * seq, q_w), BF16),
        compiler_params=_params(("parallel", "arbitrary"),
                                [((WINDOW, q_w), BF16)] * 2 + [((WINDOW, kv_w), BF16)] * 4,
                                [((2 * WINDOW, 2 * WINDOW), F32)] * 16),
        name="swa_attn",
    )(sinks, proj, proj, proj, proj, proj, cos_t, sin_t, cos_t, sin_t)


def _t5_thresholds():
    exact = REL_BUCKETS // 2
    d = np.arange(exact, 2 * REL_MAX_DIST, dtype=np.float64)
    large = exact + np.floor(np.log(d / exact) / math.log(REL_MAX_DIST / exact) * (REL_BUCKETS - exact)).astype(np.int64)
    large = np.minimum(large, REL_BUCKETS - 1)
    return [int(exact + np.argmax(large >= b)) for b in range(exact + 1, REL_BUCKETS)]


def _bias_tile_kernel(table_ref, o_ref):
    dd = pl.program_id(0)
    exact = REL_BUCKETS // 2
    key = lax.broadcasted_iota(I32, (MOBA_BLOCK, MOBA_BLOCK), 0)
    qry = lax.broadcasted_iota(I32, (MOBA_BLOCK, MOBA_BLOCK), 1)
    dist = jnp.maximum(dd * MOBA_BLOCK + qry - key, 0)
    bucket = jnp.minimum(dist, exact)
    for thr in _t5_thresholds():
        bucket = bucket + (dist >= thr).astype(I32)
    for h in range(o_ref.shape[0]):
        val = jnp.full((MOBA_BLOCK, MOBA_BLOCK), table_ref[REL_BUCKETS - 1, h], F32)
        for b in range(REL_BUCKETS - 2, -1, -1):
            val = jnp.where(bucket == b, table_ref[b, h], val)
        o_ref[h, 0] = val * LOG2E


def _bias_tiles(rel_bias):
    n_heads = rel_bias.shape[1]
    return pl.pallas_call(
        _bias_tile_kernel,
        grid=(N_BIAS_TILES,),
        in_specs=[pl.BlockSpec(memory_space=pltpu.SMEM)],
        out_specs=pl.BlockSpec((n_heads, 1, MOBA_BLOCK, MOBA_BLOCK), lambda dd: (0, dd, 0, 0)),
        out_shape=jax.ShapeDtypeStruct((n_heads, N_BIAS_TILES, MOBA_BLOCK, MOBA_BLOCK), F32),
        compiler_params=_params(("arbitrary",), [((n_heads, MOBA_BLOCK, MOBA_BLOCK), F32)],
                                [((MOBA_BLOCK, MOBA_BLOCK), F32)] * 4),
        name="moba_bias_tiles",
    )(rel_bias)


def _kmean_kernel(k_ref, o_ref, *, nblk):
    for j in range(nblk):
        rows = k_ref[j * MOBA_BLOCK:(j + 1) * MOBA_BLOCK, :].astype(F32)
        o_ref[j:j + 1, :] = jnp.mean(rows, axis=0, keepdims=True)


def _kmean(proj, bsz, seq, d):
    kv_w = d // 8
    nblk = seq // MOBA_BLOCK
    col = (9 * d // 8) // kv_w
    return pl.pallas_call(
        functools.partial(_kmean_kernel, nblk=nblk),
        grid=(bsz,),
        in_specs=[pl.BlockSpec((seq, kv_w), lambda b: (b, col))],
        out_specs=pl.BlockSpec((nblk, kv_w), lambda b: (b, 0)),
        out_shape=jax.ShapeDtypeStruct((bsz * nblk, kv_w), F32),
        compiler_params=_params(("parallel",), [((seq, kv_w), BF16)], [((MOBA_BLOCK, kv_w), F32)] * 2),
        name="moba_kmean",
    )(proj)


def _deinterleave_perm():
    perm = np.zeros((2 * LANES, 2 * LANES), np.float32)
    perm[2 * np.arange(LANES), np.arange(LANES)] = 1.0
    perm[2 * np.arange(LANES) + 1, LANES + np.arange(LANES)] = 1.0
    return jnp.asarray(perm, BF16)


def _deinterleave_block(w_ref, p_ref, wg_ref, wl_ref):
    perm = p_ref[...]
    group = perm.shape[0]
    half = group // 2
    for c in range(w_ref.shape[2] // group):
        t = jnp.dot(w_ref[0, :, c * group:(c + 1) * group].astype(BF16), perm, preferred_element_type=F32)
        wg_ref[0, :, c * half:(c + 1) * half] = t[:, :half].astype(BF16)
        wl_ref[0, :, c * half:(c + 1) * half] = t[:, half:].astype(BF16)


def _moba_kernel(q_ref, k_ref, v_ref, km_ref, bias_ref, w_ref, p_ref, o_ref, wg_ref, wl_ref, vt_ref, *, nblk):
    i = pl.program_id(2)
    mb, dh = MOBA_BLOCK, HEAD_DIM_MOBA
    scale = 1.0 / math.sqrt(dh)
    nt = (((1,), (1,)), ((), ()))
    blk = lax.broadcasted_iota(I32, (nblk, mb), 0)
    key = lax.broadcasted_iota(I32, (mb, mb), 0)
    qry = lax.broadcasted_iota(I32, (mb, mb), 1)
    causal = qry >= key

    @pl.when(i == 0)
    def _():
        for j in range(nblk):
            vt_ref[:dh, j * mb:(j + 1) * mb] = v_ref[j * mb:(j + 1) * mb, :].astype(F32).T.astype(BF16)
        vt_ref[dh:, :] = jnp.ones((vt_ref.shape[0] - dh, vt_ref.shape[1]), BF16)

    kmean = km_ref[...]
    km_hi = kmean.astype(BF16)
    rest = kmean - km_hi.astype(F32)
    km_mid = rest.astype(BF16)
    km_lo = (rest - km_mid.astype(F32)).astype(BF16)
    own = pl.multiple_of(i * mb, mb)
    k_own, vt_own = k_ref[pl.ds(own, mb), :], vt_ref[:, pl.ds(own, mb)]

    qs = [q_ref[:, hh * dh:(hh + 1) * dh] for hh in range(MOBA_GROUP)]
    gates = [lax.dot_general(km_hi, q, nt, preferred_element_type=F32)
             + lax.dot_general(km_mid, q, nt, preferred_element_type=F32)
             + lax.dot_general(km_lo, q, nt, preferred_element_type=F32) for q in qs]
    own_scores = [lax.dot_general(k_own, q, nt, preferred_element_type=F32) for q in qs]
    _deinterleave_block(w_ref, p_ref, wg_ref, wl_ref)
    picks, init = [], []
    for hh in range(MOBA_GROUP):
        gate = jnp.where(blk < i, gates[hh], NEG_INF)
        pick = []
        for t in range(MOBA_TOPK):
            gmax = jnp.max(gate, axis=0, keepdims=True)
            idx = jnp.min(jnp.where(gate == gmax, blk, nblk), axis=0, keepdims=True)
            pick.append(jnp.where(t < i, idx, -1))
            gate = jnp.where(blk == idx, NEG_INF, gate)

        s = jnp.where(causal, own_scores[hh] * (scale * LOG2E) + bias_ref[hh, 0], NEG_INF)
        m0 = jnp.max(s, axis=0, keepdims=True)
        a0 = jnp.dot(vt_own, jnp.exp2(s - m0).astype(BF16), preferred_element_type=F32)
        picks.append(pick)
        init.append((m0, a0))

    def past(jj, carry):
        ja = 2 * jj
        start = pl.multiple_of(ja * mb, mb)
        kj, vtj = k_ref[pl.ds(start, 2 * mb), :], vt_ref[:, pl.ds(start, 2 * mb)]
        tile_a = jnp.minimum(i - ja, N_BIAS_TILES - 1)
        tile_b = jnp.minimum(i - ja - 1, N_BIAS_TILES - 1)
        scores = [lax.dot_general(kj, qs[hh], nt, preferred_element_type=F32) for hh in range(MOBA_GROUP)]
        out = []
        for hh in range(MOBA_GROUP):
            m, acc = carry[hh]
            p0, p1, p2 = picks[hh]
            sa = scores[hh][:mb] * (scale * LOG2E) + bias_ref[hh, tile_a]
            sb = scores[hh][mb:] * (scale * LOG2E) + bias_ref[hh, tile_b]
            sa = jnp.where((p0 == ja) | (p1 == ja) | (p2 == ja), sa, NEG_INF)
            sb = jnp.where((p0 == ja + 1) | (p1 == ja + 1) | (p2 == ja + 1), sb, NEG_INF)
            m_new = jnp.maximum(m, jnp.maximum(jnp.max(sa, axis=0, keepdims=True),
                                               jnp.max(sb, axis=0, keepdims=True)))
            e = jnp.concatenate([jnp.exp2(sa - m_new), jnp.exp2(sb - m_new)], axis=0).astype(BF16)
            acc = jnp.exp2(m - m_new) * acc + jnp.dot(vtj, e, preferred_element_type=F32)
            out.append((m_new, acc))
        return tuple(out)

    final = lax.fori_loop(0, (i + 1) // 2, past, tuple(init))
    for hh in range(MOBA_GROUP):
        _, acc = final[hh]
        o_ref[:, hh * dh:(hh + 1) * dh] = (acc[:dh] / acc[dh:dh + 1]).T.astype(o_ref.dtype)


def _moba_attention(proj, kmean, bias_tiles, w_gu, bsz, seq, d):
    gw = MOBA_GROUP * HEAD_DIM_MOBA
    n_kv = (d // 8) // HEAD_DIM_MOBA
    nblk = seq // MOBA_BLOCK
    q_off, k_off, v_off = 5 * d // 8, 9 * d // 8, 10 * d // 8
    assert q_off % gw == 0 and MOBA_TOPK == 3 and nblk >= 2
    qc, kc, vc = q_off // gw, k_off // HEAD_DIM_MOBA, v_off // HEAD_DIM_MOBA
    tile_shape = (MOBA_GROUP, N_BIAS_TILES, MOBA_BLOCK, MOBA_BLOCK)
    n_exp, wd, f2 = w_gu.shape
    steps = n_kv * bsz * nblk
    tk = n_exp * wd // steps
    assert n_exp * wd % steps == 0 and wd % tk == 0 and tk % BF16_SUBLANES == 0
    per_exp = wd // tk
    wmap = lambda g, b, i: (((g * bsz + b) * nblk + i) // per_exp, ((g * bsz + b) * nblk + i) % per_exp, 0)
    w_out = jax.ShapeDtypeStruct((n_exp, wd, f2 // 2), BF16)
    return pl.pallas_call(
        functools.partial(_moba_kernel, nblk=nblk),
        grid=(n_kv, bsz, nblk),
        in_specs=[pl.BlockSpec((MOBA_BLOCK, gw), lambda g, b, i: (b * nblk + i, qc + g)),
                  pl.BlockSpec((seq, HEAD_DIM_MOBA), lambda g, b, i: (b, kc + g)),
                  pl.BlockSpec((seq, HEAD_DIM_MOBA), lambda g, b, i: (b, vc + g)),
                  pl.BlockSpec((nblk, HEAD_DIM_MOBA), lambda g, b, i: (b, g)),
                  pl.BlockSpec(tile_shape, lambda g, b, i: (g, 0, 0, 0)),
                  pl.BlockSpec((1, tk, f2), wmap),
                  pl.BlockSpec((2 * LANES, 2 * LANES), lambda g, b, i: (0, 0))],
        out_specs=[pl.BlockSpec((MOBA_BLOCK, gw), lambda g, b, i: (b * nblk + i, g)),
                   pl.BlockSpec((1, tk, f2 // 2), wmap), pl.BlockSpec((1, tk, f2 // 2), wmap)],
        out_shape=[jax.ShapeDtypeStruct((bsz * seq, d // 2), BF16), w_out, w_out],
        scratch_shapes=[pltpu.VMEM((HEAD_DIM_MOBA + BF16_SUBLANES, seq), BF16)],
        compiler_params=_params(("parallel", "parallel", "arbitrary"),
                                [(tile_shape, F32), ((seq, HEAD_DIM_MOBA), BF16), ((seq, HEAD_DIM_MOBA), BF16),
                                 ((MOBA_BLOCK, gw), BF16), ((MOBA_BLOCK, gw), BF16),
                                 ((tk, f2), F32), ((tk, f2), BF16)],
                                [((2 * MOBA_BLOCK, MOBA_BLOCK), F32)] * 8
                                + [((HEAD_DIM_MOBA + BF16_SUBLANES, seq), BF16)] + [((tk, 2 * LANES), F32)] * 4),
        name="moba_attn",
    )(proj, proj, proj, kmean, bias_tiles, w_gu, _deinterleave_perm())


def _merge_kernel(oa_ref, ob_ref, wa_ref, wb_ref, ga_ref, gb_ref, o_ref):
    ya = jnp.dot(oa_ref[...], wa_ref[...], preferred_element_type=F32)
    yb = jnp.dot(ob_ref[...], wb_ref[...], preferred_element_type=F32)
    merged = jax.nn.sigmoid(ga_ref[...].astype(F32)) * ya + jax.nn.sigmoid(gb_ref[...].astype(F32)) * yb
    o_ref[...] = merged.astype(o_ref.dtype)


def _merge(o_a, o_b, w_a, w_b, proj, d):
    t, kdim = o_a.shape
    tm, tn = _tile(t, 1024), _tile(d, 512)
    ga_col, gb_col = (11 * d // 8) // tn, (19 * d // 8) // tn
    assert (11 * d // 8) % tn == 0 and (19 * d // 8) % tn == 0
    return pl.pallas_call(
        _merge_kernel,
        grid=(t // tm, d // tn),
        in_specs=[pl.BlockSpec((tm, kdim), lambda i, j: (i, 0)),
                  pl.BlockSpec((tm, kdim), lambda i, j: (i, 0)),
                  pl.BlockSpec((kdim, tn), lambda i, j: (0, j)),
                  pl.BlockSpec((kdim, tn), lambda i, j: (0, j)),
                  pl.BlockSpec((tm, tn), lambda i, j: (i, ga_col + j)),
                  pl.BlockSpec((tm, tn), lambda i, j: (i, gb_col + j))],
        out_specs=pl.BlockSpec((tm, tn), lambda i, j: (i, j)),
        out_shape=jax.ShapeDtypeStruct((t, d), BF16),
        compiler_params=_params(("parallel", "arbitrary"),
                                [((tm, kdim), BF16)] * 2 + [((kdim, tn), BF16)] * 2 + [((tm, tn), BF16)] * 3,
                                [((tm, tn), F32)] * 3),
        name="merge",
    )(o_a, o_b, w_a, w_b, proj, proj)


def _post_mix_kernel(y_ref, x_ref, gpost_ref, gpre_ref, mod_ref, wrh_ref, wrl_ref, br_ref,
                     x1_ref, h2_ref, idx_ref, wts_ref):
    m = mod_ref[0]
    x1 = x_ref[...] + m[2:3, :] * _rms(y_ref[...], gpost_ref[...])
    x1_ref[...] = x1
    h2 = _rms(x1, gpre_ref[...]) * (1.0 + m[4:5, :]) + m[3:4, :]
    half = h2.shape[1] // 2
    h2_ref[...] = _pack_bf16_pair(h2[:, :half], h2[:, half:])
    logits = _dot_split(_split_bf16(h2), (wrh_ref[...], wrl_ref[...])) + br_ref[...]
    lane = lax.broadcasted_iota(I32, logits.shape, 1)
    logits = jnp.where(lane < N_EXPERTS, logits, NEG_INF)
    idx_out = jnp.zeros(logits.shape, I32)
    val_out = jnp.zeros(logits.shape, F32)
    top = None
    den = jnp.zeros((logits.shape[0], 1), F32)
    for k in range(TOP_K):
        vmax = jnp.max(logits, axis=-1, keepdims=True)
        idx = jnp.min(jnp.where(logits == vmax, lane, LANES), axis=-1, keepdims=True)
        top = vmax if top is None else top
        e = jnp.exp(vmax - top)
        den = den + e
        idx_out = jnp.where(lane == k, idx, idx_out)
        val_out = jnp.where(lane == k, e, val_out)
        logits = jnp.where(lane == idx, NEG_INF, logits)
    idx_ref[...] = idx_out
    wts_ref[...] = val_out / den


def _post_mix(y, x2, g_post, g_pre, mod, w_router, b_router, seq):
    t, d = x2.shape
    tm = _tile(seq, 256)
    wr_hi, wr_lo = _split_bf16(jnp.pad(w_router, ((0, 0), (0, LANES - N_EXPERTS))))
    br = jnp.pad(b_router, (0, LANES - N_EXPERTS)).reshape(1, LANES)
    row = lambda i: (i, 0)
    const = lambda i: (0, 0)
    return pl.pallas_call(
        _post_mix_kernel,
        grid=(t // tm,),
        in_specs=[pl.BlockSpec((tm, d), row), pl.BlockSpec((tm, d), row),
                  pl.BlockSpec((1, d), const), pl.BlockSpec((1, d), const),
                  pl.BlockSpec((1, 6, d), lambda i: (i * tm // seq, 0, 0)),
                  pl.BlockSpec((d, LANES), const), pl.BlockSpec((d, LANES), const),
                  pl.BlockSpec((1, LANES), const)],
        out_specs=[pl.BlockSpec((tm, d), row), pl.BlockSpec((tm, d // 2), row),
                   pl.BlockSpec((tm, LANES), row), pl.BlockSpec((tm, LANES), row)],
        out_shape=[jax.ShapeDtypeStruct((t, d), F32), jax.ShapeDtypeStruct((t, d // 2), jnp.uint32),
                   jax.ShapeDtypeStruct((t, LANES), I32), jax.ShapeDtypeStruct((t, LANES), F32)],
        compiler_params=_params(("parallel",), [((tm, d), F32)] * 4 + [((d, LANES), F32)],
                                [((tm, d), F32)] * 3),
        name="post_mix_router",
    )(y, x2, g_post.reshape(1, d), g_pre.reshape(1, d), mod, wr_hi, wr_lo, br)


def _routing(top_idx, n_blocks):
    flat_e = top_idx.reshape(-1)
    onehot = (flat_e[:, None] == jnp.arange(N_EXPERTS, dtype=I32)[None, :]).astype(I32)
    csum = jnp.cumsum(onehot, axis=0)
    counts = csum[-1]
    rank = jnp.sum(csum * onehot, axis=1) - 1
    padded = (counts + MOE_BLOCK - 1) // MOE_BLOCK * MOE_BLOCK
    pad_end = jnp.cumsum(padded)
    pad_start = pad_end - padded
    dest = jnp.sum(onehot * pad_start[None, :], axis=1) + rank
    n_real = (pad_end[-1] // MOE_BLOCK).astype(I32)
    blk_start = jnp.arange(n_blocks, dtype=I32) * MOE_BLOCK
    block_e = jnp.minimum(jnp.sum((pad_end[None, :] <= blk_start[:, None]).astype(I32), axis=1), N_EXPERTS - 1)
    last_blk = jnp.where(counts > 0, pad_end // MOE_BLOCK - 1, -1).astype(I32)
    return dest.astype(I32), block_e.astype(I32), n_real.reshape(1), last_blk


def _pack_bf16_pair(lo, hi):
    lo_bits = lax.bitcast_convert_type(lo.astype(BF16).astype(F32), jnp.uint32) >> 16
    hi_bits = lax.bitcast_convert_type(hi.astype(BF16).astype(F32), jnp.uint32) & jnp.uint32(0xFFFF0000)
    return hi_bits | lo_bits


def _unpack_bf16_pair(p):
    lo = lax.bitcast_convert_type(p << 16, F32).astype(BF16)
    hi = lax.bitcast_convert_type(p & jnp.uint32(0xFFFF0000), F32).astype(BF16)
    return lo, hi


def _dispatch_kernel(last_ref, nr_ref, dest_ref, h_ref, xg_hbm, zeros, zsem, sem, *, tokens, n_blocks):
    def zero_block(m):
        rows = pl.ds(pl.multiple_of(m * MOE_BLOCK, MOE_BLOCK), MOE_BLOCK)
        return pltpu.make_async_copy(zeros, xg_hbm.at[rows], zsem)

    @pl.when(pl.program_id(0) == 0)
    def _():
        zeros[...] = jnp.zeros(zeros.shape, zeros.dtype)
        for phase in ("start", "wait"):
            for e in range(N_EXPERTS):
                @pl.when(last_ref[e] >= 0)
                def _(e=e, phase=phase):
                    getattr(zero_block(last_ref[e]), phase)()

            def tail(m, carry, phase=phase):
                getattr(zero_block(m), phase)()
                return carry

            lax.fori_loop(nr_ref[0], n_blocks, tail, 0)

    def row_copy(t, row):
        return pltpu.make_async_copy(h_ref.at[pl.ds(t, 1)], xg_hbm.at[pl.ds(row, 1)], sem)

    def issue(t, carry):
        for k in range(TOP_K):
            row_copy(t, dest_ref[0, 0, t * TOP_K + k]).start(priority=k % 2)
        return carry

    lax.fori_loop(0, tokens, issue, 0)

    def drain(t, carry):
        for k in range(TOP_K):
            row_copy(0, 0).wait()
        return carry

    lax.fori_loop(0, tokens, drain, 0)


def _dispatch(h2p, dest, last_blk, n_real, n_blocks):
    t, dp = h2p.shape
    tokens = _tile(t, 512)
    steps = t // tokens
    return pl.pallas_call(
        functools.partial(_dispatch_kernel, tokens=tokens, n_blocks=n_blocks),
        grid_spec=pltpu.PrefetchScalarGridSpec(
            num_scalar_prefetch=2, grid=(steps,),
            in_specs=[pl.BlockSpec((1, 1, tokens * TOP_K), lambda i, last, nr: (i, 0, 0), memory_space=pltpu.SMEM),
                      pl.BlockSpec((tokens, dp), lambda i, last, nr: (i, 0))],
            out_specs=pl.BlockSpec(memory_space=pl.ANY),
            scratch_shapes=[pltpu.VMEM((MOE_BLOCK, dp), h2p.dtype),
                            pltpu.SemaphoreType.DMA(()), pltpu.SemaphoreType.DMA(())]),
        out_shape=jax.ShapeDtypeStruct((n_blocks * MOE_BLOCK, dp), h2p.dtype),
        compiler_params=_params(("arbitrary",), [((tokens, dp), h2p.dtype)], [((MOE_BLOCK, dp), h2p.dtype)]),
        name="moe_dispatch",
    )(last_blk, n_real, dest.reshape(steps, 1, tokens * TOP_K), h2p)


def _gate_up_kernel(be_ref, nr_ref, x_ref, wg_ref, wl_ref, bg_ref, bl_ref, wd_ref, o_ref, wdb_ref, *, n_cast):
    real = pl.program_id(1) < nr_ref[0]

    @pl.when(jnp.logical_and(pl.program_id(0) == 0, pl.program_id(1) < n_cast))
    def _():
        wdb_ref[...] = wd_ref[...].astype(BF16)

    @pl.when(real)
    def _():
        lo, hi = _unpack_bf16_pair(x_ref[...])
        half = lo.shape[1]

        def proj(w_ref, b_ref):
            return (jnp.dot(lo, w_ref[0, :half, :], preferred_element_type=F32)
                    + jnp.dot(hi, w_ref[0, half:, :], preferred_element_type=F32) + b_ref[0])

        glu = jnp.minimum(proj(wg_ref, bg_ref), SWIGLU_LIMIT)
        lin = jnp.clip(proj(wl_ref, bl_ref), -SWIGLU_LIMIT, SWIGLU_LIMIT)
        o_ref[...] = (glu * jax.nn.sigmoid(SWIGLU_ALPHA * glu) * (lin + 1.0)).astype(o_ref.dtype)

    @pl.when(jnp.logical_not(real))
    def _():
        o_ref[...] = jnp.zeros(o_ref.shape, o_ref.dtype)


def _gate_up(xg, w_g, w_l, b_g, b_l, w_d, block_e, n_real, n_blocks):
    rows, dp = xg.shape
    d, f = w_g.shape[1], w_g.shape[2]
    passes = 2
    tf = f // passes
    blk = lambda p, m, be, nr: jnp.minimum(m, nr[0] - 1)
    wmap = lambda p, m, be, nr: (be[blk(p, m, be, nr)], 0, p)
    n_exp, wf, wn = w_d.shape
    tr = min(r for r in range(BF16_SUBLANES, wf + 1, BF16_SUBLANES)
             if wf % r == 0 and n_exp * wf // r <= n_blocks)
    per_exp = wf // tr
    n_cast = n_exp * per_exp

    def cmap(p, m, be, nr):
        s = jnp.where(p == 0, jnp.minimum(m, n_cast - 1), n_cast - 1)
        return (s // per_exp, s % per_exp, 0)

    return pl.pallas_call(
        functools.partial(_gate_up_kernel, n_cast=n_cast),
        grid_spec=pltpu.PrefetchScalarGridSpec(
            num_scalar_prefetch=2, grid=(passes, n_blocks),
            in_specs=[pl.BlockSpec((MOE_BLOCK, dp), lambda p, m, be, nr: (blk(p, m, be, nr), 0)),
                      pl.BlockSpec((1, d, tf), wmap), pl.BlockSpec((1, d, tf), wmap),
                      pl.BlockSpec((1, 1, tf), wmap), pl.BlockSpec((1, 1, tf), wmap),
                      pl.BlockSpec((1, tr, wn), cmap)],
            out_specs=[pl.BlockSpec((MOE_BLOCK, tf), lambda p, m, be, nr: (m, p)),
                       pl.BlockSpec((1, tr, wn), cmap)]),
        out_shape=[jax.ShapeDtypeStruct((rows, f), BF16), jax.ShapeDtypeStruct(w_d.shape, BF16)],
        compiler_params=_params(("arbitrary", "arbitrary"),
                                [((MOE_BLOCK, dp), xg.dtype), ((d, tf), BF16), ((d, tf), BF16),
                                 ((MOE_BLOCK, tf), BF16), ((tr, wn), F32), ((tr, wn), BF16)],
                                [((MOE_BLOCK, d), BF16)] + [((MOE_BLOCK, tf), F32)] * 4),
        name="moe_gate_up",
    )(block_e, n_real, xg, w_g, w_l, b_g, b_l, w_d)


def _down_kernel(be_ref, nr_ref, a_ref, w_ref, b_ref, o_ref):
    real = pl.program_id(0) < nr_ref[0]

    @pl.when(real)
    def _():
        y = jnp.dot(a_ref[...], w_ref[0], preferred_element_type=F32) + b_ref[0]
        half = y.shape[1] // 2
        o_ref[...] = _pack_bf16_pair(y[:, :half], y[:, half:])

    @pl.when(jnp.logical_not(real))
    def _():
        o_ref[...] = jnp.zeros(o_ref.shape, o_ref.dtype)


def _down(act, w_d, b_d, block_e, n_real, n_blocks):
    rows, f = act.shape
    d = w_d.shape[2]
    blk = lambda m, be, nr: jnp.minimum(m, nr[0] - 1)
    wmap = lambda m, be, nr: (be[blk(m, be, nr)], 0, 0)
    return pl.pallas_call(
        _down_kernel,
        grid_spec=pltpu.PrefetchScalarGridSpec(
            num_scalar_prefetch=2, grid=(n_blocks,),
            in_specs=[pl.BlockSpec((MOE_BLOCK, f), lambda m, be, nr: (m, 0)),
                      pl.BlockSpec((1, f, d), wmap), pl.BlockSpec((1, 1, d), wmap)],
            out_specs=pl.BlockSpec((MOE_BLOCK, d // 2), lambda m, be, nr: (m, 0))),
        out_shape=jax.ShapeDtypeStruct((rows, d // 2), jnp.uint32),
        compiler_params=_params(("arbitrary",),
                                [((MOE_BLOCK, f), BF16), ((f, d), BF16), ((MOE_BLOCK, d // 2), jnp.uint32)],
                                [((MOE_BLOCK, d), F32)] * 2),
        name="moe_down",
    )(block_e, n_real, act, w_d, b_d)


def _combine_kernel(dcur_ref, dnext_ref, w_ref, x1_ref, g_ref, mod_ref, y_hbm, o_ref, buf, sem,
                    *, tokens, steps):
    step = pl.program_id(0)
    slot = step % 2

    def row_copy(slot_, k, t, row):
        return pltpu.make_async_copy(y_hbm.at[pl.ds(row, 1)], buf.at[slot_, k, pl.ds(t, 1)], sem.at[slot_])

    def start_tile(dest_ref, slot_):
        def body(t, carry):
            for k in range(TOP_K):
                row_copy(slot_, k, t, dest_ref[0, 0, t * TOP_K + k]).start(priority=k % 2)
            return carry
        lax.fori_loop(0, tokens, body, 0)

    @pl.when(step == 0)
    def _():
        start_tile(dcur_ref, 0)

    @pl.when(step + 1 < steps)
    def _():
        start_tile(dnext_ref, 1 - slot)

    def drain(t, carry):
        for k in range(TOP_K):
            row_copy(slot, k, 0, 0).wait()
        return carry

    lax.fori_loop(0, tokens, drain, 0)

    w = w_ref[...]
    acc_lo = acc_hi = None
    for k in range(TOP_K):
        word = buf[slot, k]
        lo = lax.bitcast_convert_type(word << 16, F32) * w[:, k:k + 1]
        hi = lax.bitcast_convert_type(word & jnp.uint32(0xFFFF0000), F32) * w[:, k:k + 1]
        acc_lo = lo if acc_lo is None else acc_lo + lo
        acc_hi = hi if acc_hi is None else acc_hi + hi
    acc = jnp.concatenate([acc_lo, acc_hi], axis=1)
    m = mod_ref[0]
    o_ref[...] = x1_ref[...] + m[5:6, :] * _rms(acc, g_ref[...])


def _combine(y_sorted, dest, top_w, x1, g_post, mod, seq):
    t, d = x1.shape
    dp = y_sorted.shape[1]
    tokens = _tile(seq, 256)
    steps = t // tokens
    row = lambda i: (i, 0)
    dest3 = dest.reshape(steps, 1, tokens * TOP_K)
    return pl.pallas_call(
        functools.partial(_combine_kernel, tokens=tokens, steps=steps),
        grid=(steps,),
        in_specs=[pl.BlockSpec((1, 1, tokens * TOP_K), lambda i: (i, 0, 0), memory_space=pltpu.SMEM),
                  pl.BlockSpec((1, 1, tokens * TOP_K), lambda i: (jnp.minimum(i + 1, steps - 1), 0, 0),
                               memory_space=pltpu.SMEM),
                  pl.BlockSpec((tokens, LANES), row),
                  pl.BlockSpec((tokens, d), row),
                  pl.BlockSpec((1, d), lambda i: (0, 0)),
                  pl.BlockSpec((1, 6, d), lambda i: (i * tokens // seq, 0, 0)),
                  pl.BlockSpec(memory_space=pl.ANY)],
        out_specs=pl.BlockSpec((tokens, d), row),
        out_shape=jax.ShapeDtypeStruct((t, d), F32),
        scratch_shapes=[pltpu.VMEM((2, TOP_K, tokens, dp), y_sorted.dtype), pltpu.SemaphoreType.DMA((2,))],
        compiler_params=_params(("arbitrary",), [((tokens, d), F32)] * 2,
                                [((2, TOP_K, tokens, dp), y_sorted.dtype)] + [((tokens, d), F32)] * 3),
        name="moe_combine",
    )(dest3, dest3, top_w, x1, g_post.reshape(1, d), mod, y_sorted)


def _moe(h2, top_idx, top_w, x1, g_post, mod, w_g, w_l, b_gu, w_dn, b_dn, seq):
    t, d = x1.shape
    f = w_dn.shape[1]
    n_blocks = -(-(t * TOP_K) // MOE_BLOCK) + N_EXPERTS
    dest, block_e, n_real, last_blk = _routing(top_idx, n_blocks)
    xg = _dispatch(h2, dest, last_blk, n_real, n_blocks)
    b_g = b_gu[:, 0::2].reshape(N_EXPERTS, 1, f)
    b_l = b_gu[:, 1::2].reshape(N_EXPERTS, 1, f)
    act, w_dn_bf16 = _gate_up(xg, w_g, w_l, b_g, b_l, w_dn, block_e, n_real, n_blocks)
    y_sorted = _down(act, w_dn_bf16, b_dn.reshape(N_EXPERTS, 1, d), block_e, n_real, n_blocks)
    return _combine(y_sorted, dest, top_w, x1, g_post, mod, seq)


def kernel(x, c, w_ada, b_ada, g_pre_mix, g_post_mix, g_pre_ffn, g_post_ffn, w_in, b_in, sinks, rel_bias,
           w_o_swa, w_o_moba, w_out, w_router, b_router, w_gate_up, b_gate_up, w_down, b_down):
    bsz, seq, d = x.shape
    depth = w_ada.shape[0]
    assert seq % MOBA_BLOCK == 0 and d % 1024 == 0
    x2 = x.reshape(bsz * seq, d)
    cos_t, sin_t = _rope_tables(seq)
    bias_tiles = _bias_tiles(rel_bias)
    for l in range(depth):
        mod = _ada_mod(c, w_ada[l], b_ada[l])
        h = _prenorm(x2, g_pre_mix[l], mod, seq, shift_row=0, scale_row=1)
        proj = _matmul(h, w_in[l].astype(BF16), b_in[l], BF16, 1024, 768, "in_proj")
        o_a = _swa_attention(proj, sinks[l], cos_t, sin_t, bsz, seq, d)
        o_b, w_g, w_l = _moba_attention(proj, _kmean(proj, bsz, seq, d), bias_tiles, w_gate_up[l], bsz, seq, d)
        merged = _merge(o_a, o_b, w_o_swa[l].astype(BF16), w_o_moba[l].astype(BF16), proj, d)
        y = _matmul(merged, w_out[l].astype(BF16), jnp.zeros((d,), F32), F32, 1024, 1024, "out_proj")
        x1, h2, idx, wts = _post_mix(y, x2, g_post_mix[l], g_pre_ffn[l], mod, w_router[l], b_router[l], seq)
        x2 = _moe(h2, idx[:, :TOP_K], wts, x1, g_post_ffn[l], mod,
                  w_g, w_l, b_gate_up[l], w_down[l], b_down[l], seq)
    return x2.reshape(bsz, seq, d)
```
